```python
import math
import jax, jax.numpy as jnp
from jax import lax
import numpy as np


D_MODEL = 1024
BATCH = 8
SEQ = 4096
DEPTH = 1

CHUNK = 64
Q_BLOCK = 128
D_MIX = D_MODEL
GLA_WIDTH = D_MIX // 2
GLA_HEADS = 4
GLA_DV = GLA_WIDTH // GLA_HEADS
GLA_DK = GLA_DV // 2
GLA_LOWRANK = 16
GLA_TAU = 16.0
DIFF_WIDTH = D_MIX - GLA_WIDTH
DIFF_HEADS = 4
DIFF_DV = DIFF_WIDTH // DIFF_HEADS
DIFF_DQK = DIFF_DV // 2
ROPE_DIM = DIFF_DQK // 4
ROPE_THETA = 500000.0
MEM_LEN = 256
CROSS_HEADS = 4
CROSS_DH = D_MODEL // CROSS_HEADS
EPS = 1e-6

GLA_QK = GLA_HEADS * GLA_DK
DIFF_QK = DIFF_HEADS * 2 * DIFF_DQK
IN_SPLITS = (GLA_QK, GLA_QK, GLA_WIDTH, GLA_WIDTH, GLA_LOWRANK,
             DIFF_QK, DIFF_QK, DIFF_WIDTH, DIFF_WIDTH)
D_IN = 2 * GLA_QK + 2 * GLA_WIDTH + GLA_LOWRANK + 2 * DIFF_QK + 2 * DIFF_WIDTH

kernel_name = "hybrid_gla_diffattn_parallel_heads"


def rmsnorm(x, w):
    xf = x.astype(jnp.float32)
    y = xf * lax.rsqrt(jnp.mean(xf * xf, axis=-1, keepdims=True) + EPS)
    return (y * w.astype(jnp.float32)).astype(x.dtype)


def head_rmsnorm(o, w):
    o = o.astype(jnp.float32)
    return o * lax.rsqrt(jnp.mean(o * o, axis=-1, keepdims=True) + EPS) * w.astype(jnp.float32)


def split_cols(proj):
    outs, start = [], 0
    for width in IN_SPLITS:
        outs.append(proj[..., start:start + width])
        start += width
    return outs


def rope_tables(positions):
    pos = positions.astype(jnp.float32)
    inv_freq = ROPE_THETA ** (-(jnp.arange(0, ROPE_DIM, 2, dtype=jnp.float32) / ROPE_DIM))
    ang = pos[..., None] * inv_freq
    cos = jnp.concatenate([jnp.cos(ang), jnp.cos(ang)], axis=-1)
    sin = jnp.concatenate([jnp.sin(ang), jnp.sin(ang)], axis=-1)
    return cos, sin


def partial_rope(t, cos, sin):
    t = t.astype(jnp.float32)
    rot, rest = t[..., :ROPE_DIM], t[..., ROPE_DIM:]
    half = ROPE_DIM // 2
    r1, r2 = rot[..., :half], rot[..., half:]
    c = cos[:, :, None, None, :]
    s = sin[:, :, None, None, :]
    rotated = rot * c + jnp.concatenate([-r2, r1], axis=-1) * s
    return jnp.concatenate([rotated, rest], axis=-1)


def gla_chunked(q, k, v, g):
    B, S = q.shape[0], q.shape[1]
    N = S // CHUNK

    def to_chunks(t):
        return t.astype(jnp.float32).reshape(B, N, CHUNK, GLA_HEADS, -1).transpose(0, 3, 1, 2, 4)

    q, k, v, g = to_chunks(q), to_chunks(k), to_chunks(v), to_chunks(g)
    q = q * (GLA_DK ** -0.5)
    G = jnp.cumsum(g, axis=3)
    q_dec = q * jnp.exp(G)
    k_inv = k * jnp.exp(-G)
    causal = jnp.tril(jnp.ones((CHUNK, CHUNK), dtype=bool))
    scores = jnp.einsum('bhncd,bhnsd->bhncs', q_dec, k_inv)
    scores = jnp.where(causal, scores, 0.0)
    o_intra = jnp.einsum('bhncs,bhnse->bhnce', scores, v)

    G_last = G[:, :, :, -1:, :]
    kv_chunk = jnp.einsum('bhncd,bhnce->bhnde', k * jnp.exp(G_last - G), v)
    decay = jnp.exp(G_last[:, :, :, 0, :])

    def step(state, inp):
        dec, kv = inp
        return dec[..., None] * state + kv, state

    init = jnp.zeros((B, GLA_HEADS, GLA_DK, GLA_DV), jnp.float32)
    _, states = lax.scan(step, init, (decay.transpose(2, 0, 1, 3), kv_chunk.transpose(2, 0, 1, 3, 4)))
    states = states.transpose(1, 2, 0, 3, 4)
    o_inter = jnp.einsum('bhncd,bhnde->bhnce', q_dec, states)
    o = o_intra + o_inter
    return o.transpose(0, 2, 3, 1, 4).reshape(B, S, GLA_HEADS, GLA_DV)


def diff_attention(q, k, v, lam):
    B, S = q.shape[0], q.shape[1]
    NB = S // Q_BLOCK
    qh = q.astype(jnp.float32).transpose(0, 2, 3, 1, 4)
    kh = k.astype(jnp.float32).transpose(0, 2, 3, 1, 4)
    vh = v.astype(jnp.float32).transpose(0, 2, 1, 3)
    key_chunk = jnp.arange(S) // CHUNK
    q_blocks = qh.reshape(B, DIFF_HEADS, 2, NB, Q_BLOCK, DIFF_DQK).transpose(3, 0, 1, 2, 4, 5)
    scale = DIFF_DQK ** -0.5

    def one_block(args):
        qb, bi = args
        qpos = bi * Q_BLOCK + jnp.arange(Q_BLOCK)
        allowed = key_chunk[None, :] <= (qpos // CHUNK)[:, None]
        s = jnp.einsum('bhmqd,bhmkd->bhmqk', qb, kh) * scale
        s = jnp.where(allowed, s, -jnp.inf)
        p = jax.nn.softmax(s, axis=-1)
        a = p[:, :, 0] - lam * p[:, :, 1]
        return jnp.einsum('bhqk,bhke->bhqe', a, vh)

    o = lax.map(one_block, (q_blocks, jnp.arange(NB)))
    return o.transpose(1, 0, 3, 2, 4).reshape(B, S, DIFF_HEADS, DIFF_DV)


def cross_attention(h, mem_n, w_q, w_kv, w_o):
    B, S, _ = h.shape
    M = mem_n.shape[1]
    q = (h @ w_q).reshape(B, S, CROSS_HEADS, CROSS_DH)
    kv = mem_n @ w_kv
    k = kv[..., :D_MODEL].reshape(B, M, CROSS_HEADS, CROSS_DH)
    v = kv[..., D_MODEL:].reshape(B, M, CROSS_HEADS, CROSS_DH)
    s = jnp.einsum('bqhd,bkhd->bhqk', q, k).astype(jnp.float32) * (CROSS_DH ** -0.5)
    p = jax.nn.softmax(s, axis=-1).astype(v.dtype)
    o = jnp.einsum('bhqk,bkhd->bqhd', p, v).reshape(B, S, D_MODEL)
    return o @ w_o


def setup_inputs(seed: int = 0) -> dict:
    key = jax.random.key(seed)
    ks = jax.random.split(key, 24)
    f32 = jnp.float32
    nrm = lambda k, shape, scale: jax.random.normal(k, shape, f32) * scale
    x = jax.random.normal(ks[0], (BATCH, SEQ, D_MODEL), f32)
    mem = jax.random.normal(ks[1], (BATCH, MEM_LEN, D_MODEL), f32)
    start = jax.random.randint(ks[2], (BATCH, 1), 0, 1024, dtype=jnp.int32)
    positions = (start + jnp.arange(SEQ, dtype=jnp.int32)[None, :]).astype(jnp.int32)
    return {
        "x": x,
        "mem": mem,
        "positions": positions,
        "norm_mix_w": 1.0 + nrm(ks[3], (DEPTH, D_MODEL), 0.02),
        "w_in": nrm(ks[4], (DEPTH, D_MODEL, D_IN), D_MODEL ** -0.5),
        "w_alpha2": nrm(ks[5], (DEPTH, GLA_LOWRANK, GLA_QK), GLA_LOWRANK ** -0.5),
        "b_alpha2": nrm(ks[6], (DEPTH, GLA_QK), 0.1),
        "gla_norm_w": 1.0 + nrm(ks[7], (DEPTH, GLA_DV), 0.02),
        "lam_q1": nrm(ks[8], (DEPTH, DIFF_DQK), 0.1),
        "lam_k1": nrm(ks[9], (DEPTH, DIFF_DQK), 0.1),
        "lam_q2": nrm(ks[10], (DEPTH, DIFF_DQK), 0.1),
        "lam_k2": nrm(ks[11], (DEPTH, DIFF_DQK), 0.1),
        "diff_norm_w": 1.0 + nrm(ks[12], (DEPTH, DIFF_DV), 0.02),
        "w_out": nrm(ks[13], (DEPTH, D_MIX, D_MODEL), D_MIX ** -0.5),
        "norm_cross_w": 1.0 + nrm(ks[14], (DEPTH, D_MODEL), 0.02),
        "norm_mem_w": 1.0 + nrm(ks[15], (DEPTH, D_MODEL), 0.02),
        "w_cq": nrm(ks[16], (DEPTH, D_MODEL, D_MODEL), D_MODEL ** -0.5),
        "w_ckv": nrm(ks[17], (DEPTH, D_MODEL, 2 * D_MODEL), D_MODEL ** -0.5),
        "w_co": nrm(ks[18], (DEPTH, D_MODEL, D_MODEL), D_MODEL ** -0.5),
        "final_norm_w": 1.0 + nrm(ks[19], (D_MODEL,), 0.02),
    }


def reference(x, mem, positions, norm_mix_w, w_in, w_alpha2, b_alpha2, gla_norm_w,
              lam_q1, lam_k1, lam_q2, lam_k2, diff_norm_w, w_out,
              norm_cross_w, norm_mem_w, w_cq, w_ckv, w_co, final_norm_w):
    B, S, _ = x.shape
    f32 = jnp.float32
    cos, sin = rope_tables(positions)
    for l in range(DEPTH):
        h = rmsnorm(x, norm_mix_w[l])
        proj = h @ w_in[l]
        gq, gk, gv, ggate, ga, dq, dk, dv, dgate = split_cols(proj)

        g_logit = (ga @ w_alpha2[l] + b_alpha2[l]).astype(f32)
        g = jax.nn.log_sigmoid(g_logit) / GLA_TAU
        o_gla = gla_chunked(gq.reshape(B, S, GLA_HEADS, GLA_DK),
                            gk.reshape(B, S, GLA_HEADS, GLA_DK),
                            gv.reshape(B, S, GLA_HEADS, GLA_DV),
                            g.reshape(B, S, GLA_HEADS, GLA_DK))
        o_gla = head_rmsnorm(o_gla, gla_norm_w[l]) * jax.nn.silu(
            ggate.astype(f32).reshape(B, S, GLA_HEADS, GLA_DV))

        lam_init = 0.8 - 0.6 * math.exp(-0.3 * l)
        lam = (jnp.exp(jnp.sum(lam_q1[l].astype(f32) * lam_k1[l].astype(f32)))
               - jnp.exp(jnp.sum(lam_q2[l].astype(f32) * lam_k2[l].astype(f32))) + lam_init)
        q = partial_rope(dq.reshape(B, S, DIFF_HEADS, 2, DIFF_DQK), cos, sin)
        k = partial_rope(dk.reshape(B, S, DIFF_HEADS, 2, DIFF_DQK), cos, sin)
        o_diff = diff_attention(q, k, dv.reshape(B, S, DIFF_HEADS, DIFF_DV), lam)
        o_diff = head_rmsnorm(o_diff, diff_norm_w[l]) * (1.0 - lam_init) * jax.nn.silu(
            dgate.astype(f32).reshape(B, S, DIFF_HEADS, DIFF_DV))

        mixed = jnp.concatenate([o_gla.reshape(B, S, GLA_WIDTH),
                                 o_diff.reshape(B, S, DIFF_WIDTH)], axis=-1).astype(x.dtype)
        x = x + mixed @ w_out[l]

        hc = rmsnorm(x, norm_cross_w[l])
        mem_n = rmsnorm(mem, norm_mem_w[l])
        x = x + cross_attention(hc, mem_n, w_cq[l], w_ckv[l], w_co[l])
    return rmsnorm(x, final_norm_w)
```

```python
import functools
import math

import jax
import jax.numpy as jnp
from jax import lax
from jax.experimental import pallas as pl
from jax.experimental.pallas import tpu as pltpu

F32 = jnp.float32
BF16 = jnp.bfloat16

CHUNK = 64
GLA_HEADS = 4
GLA_DK = 64
GLA_DV = 128
GLA_QK = GLA_HEADS * GLA_DK
GLA_WIDTH = GLA_HEADS * GLA_DV
GLA_LOWRANK = 16
GLA_TAU = 16.0
DIFF_HEADS = 4
DIFF_DQK = 64
DIFF_DV = 128
DIFF_WIDTH = DIFF_HEADS * DIFF_DV
ROPE_DIM = 16
ROPE_THETA = 500000.0
CROSS_HEADS = 4
EPS = 1e-6

LANES = 128

TM_PROJ = 512
TM_GLA = 512
TQ_ATT = 256
TK_ATT = 256
TM_OUT = 512
VMEM_LIMIT = 56 * 1024 * 1024

NT_DIMS = (((1,), (1,)), ((), ()))
TN_DIMS = (((0,), (0,)), ((), ()))


def _rms_scale(xf, w):
    return xf * lax.rsqrt(jnp.mean(xf * xf, axis=-1, keepdims=True) + EPS) * w


def _silu(t):
    return t * (1.0 / (1.0 + jnp.exp(-t)))


def _in_proj_kernel(x_ref, nw_ref, wg_ref, wga_ref, wa2_ref, ba2_ref, wd_ref, cos_ref, sin_ref,
                    gq_ref, gk_ref, gv_ref, ggs_ref, g_ref, dq_ref, dk_ref, dv_ref, dgs_ref):
    h = _rms_scale(x_ref[...], nw_ref[...]).astype(BF16)

    gq_ref[...] = jnp.dot(h, wg_ref[:, 0:GLA_QK], preferred_element_type=F32) * (GLA_DK ** -0.5)
    gk_ref[...] = jnp.dot(h, wg_ref[:, GLA_QK:2 * GLA_QK], preferred_element_type=F32)
    gv_ref[...] = jnp.dot(h, wg_ref[:, 2 * GLA_QK:2 * GLA_QK + GLA_WIDTH],
                          preferred_element_type=F32).astype(BF16)
    gate = jnp.dot(h, wg_ref[:, 2 * GLA_QK + GLA_WIDTH:], preferred_element_type=F32)
    ggs_ref[...] = _silu(gate).astype(BF16)

    ga = jnp.dot(h, wga_ref[...], preferred_element_type=F32).astype(BF16)
    logit = jnp.dot(ga, wa2_ref[...], preferred_element_type=F32) + ba2_ref[...]
    log_sig = jnp.minimum(logit, 0.0) - jnp.log1p(jnp.exp(-jnp.abs(logit)))
    g_ref[...] = log_sig * (1.0 / GLA_TAU)

    cos = cos_ref[...]
    sin = sin_ref[...]
    lane = lax.broadcasted_iota(jnp.int32, cos.shape, 1)
    first_half = (lane % DIFF_DQK) < (ROPE_DIM // 2)

    def rope(t):
        partner = jnp.where(first_half, pltpu.roll(t, LANES - ROPE_DIM // 2, 1),
                            pltpu.roll(t, ROPE_DIM // 2, 1))
        return t * cos + partner * sin

    for hd in range(DIFF_HEADS):
        sl = slice(hd * DIFF_DV, (hd + 1) * DIFF_DV)
        q = jnp.dot(h, wd_ref[:, sl], preferred_element_type=F32)
        dq_ref[:, sl] = (rope(q) * (DIFF_DQK ** -0.5)).astype(BF16)
        k = jnp.dot(h, wd_ref[:, DIFF_WIDTH + hd * DIFF_DV:DIFF_WIDTH + (hd + 1) * DIFF_DV],
                    preferred_element_type=F32)
        dk_ref[:, sl] = rope(k).astype(BF16)
    dv_ref[...] = jnp.dot(h, wd_ref[:, 2 * DIFF_WIDTH:3 * DIFF_WIDTH],
                          preferred_element_type=F32).astype(BF16)
    dgate = jnp.dot(h, wd_ref[:, 3 * DIFF_WIDTH:], preferred_element_type=F32)
    dgs_ref[...] = _silu(dgate).astype(BF16)


def _in_proj(x2d, norm_w, wg, wga, wa2, ba2, wd, cos_t, sin_t):
    T, D = x2d.shape
    tm = TM_PROJ
    row = lambda i: (i, 0)
    fixed = lambda i: (0, 0)
    out_shapes = (
        jax.ShapeDtypeStruct((T, GLA_QK), F32),
        jax.ShapeDtypeStruct((T, GLA_QK), F32),
        jax.ShapeDtypeStruct((T, GLA_WIDTH), BF16),
        jax.ShapeDtypeStruct((T, GLA_WIDTH), BF16),
        jax.ShapeDtypeStruct((T, GLA_QK), F32),
        jax.ShapeDtypeStruct((T, DIFF_WIDTH), BF16),
        jax.ShapeDtypeStruct((T, DIFF_WIDTH), BF16),
        jax.ShapeDtypeStruct((T, DIFF_WIDTH), BF16),
        jax.ShapeDtypeStruct((T, DIFF_WIDTH), BF16),
    )
    return pl.pallas_call(
        _in_proj_kernel,
        grid=(T // tm,),
        in_specs=[
            pl.BlockSpec((tm, D), row),
            pl.BlockSpec((1, D), fixed),
            pl.BlockSpec(wg.shape, fixed),
            pl.BlockSpec(wga.shape, fixed),
            pl.BlockSpec(wa2.shape, fixed),
            pl.BlockSpec(ba2.shape, fixed),
            pl.BlockSpec(wd.shape, fixed),
            pl.BlockSpec((tm, LANES), row),
            pl.BlockSpec((tm, LANES), row),
        ],
        out_specs=tuple(pl.BlockSpec((tm, s.shape[1]), row) for s in out_shapes),
        out_shape=out_shapes,
        compiler_params=pltpu.CompilerParams(dimension_semantics=("parallel",),
                                             vmem_limit_bytes=VMEM_LIMIT),
        name="in_proj",
    )(x2d, norm_w, wg, wga, wa2, ba2, wd, cos_t, sin_t)


def _gla_kernel(gq_ref, gk_ref, g_ref, gv_ref, ggs_ref, nw_ref, o_ref,
                qd_s, ki_s, kl_s, dec_s, state_s):
    tm = gq_ref.shape[0]
    n_chunks = tm // CHUNK

    @pl.when(pl.program_id(1) == 0)
    def _():
        state_s[...] = jnp.zeros_like(state_s)

    g = g_ref[...]
    row_in_chunk = lax.broadcasted_iota(jnp.int32, g.shape, 0) % CHUNK
    G = g
    shift = 1
    while shift < CHUNK:
        G = G + jnp.where(row_in_chunk >= shift, pltpu.roll(G, shift, 0), 0.0)
        shift *= 2
    q = gq_ref[...]
    k = gk_ref[...]
    qd_s[...] = (q * jnp.exp(G)).astype(BF16)
    ki_s[...] = (k * jnp.exp(-G)).astype(BF16)
    for c in range(n_chunks):
        rows = slice(c * CHUNK, (c + 1) * CHUNK)
        g_last = G[(c + 1) * CHUNK - 1:(c + 1) * CHUNK, :]
        kl_s[rows, :] = (k[rows, :] * jnp.exp(g_last - G[rows, :])).astype(BF16)
        dec_s[c:c + 1, :] = jnp.exp(g_last)

    causal = (lax.broadcasted_iota(jnp.int32, (CHUNK, CHUNK), 1)
              <= lax.broadcasted_iota(jnp.int32, (CHUNK, CHUNK), 0))
    nw = nw_ref[...]

    for c in range(n_chunks):
        rows = slice(c * CHUNK, (c + 1) * CHUNK)
        for hd in range(GLA_HEADS):
            kl = slice(hd * GLA_DK, (hd + 1) * GLA_DK)
            vl = slice(hd * GLA_DV, (hd + 1) * GLA_DV)
            qd = qd_s[rows, kl]
            ki = ki_s[rows, kl]
            kk = kl_s[rows, kl]
            v = gv_ref[rows, vl]
            st = state_s[hd]
            scores = lax.dot_general(qd, ki, NT_DIMS, preferred_element_type=F32)
            scores = jnp.where(causal, scores, 0.0).astype(BF16)
            o = (jnp.dot(scores, v, preferred_element_type=F32)
                 + lax.dot_general(qd, st.astype(BF16), NT_DIMS, preferred_element_type=F32))
            kv_t = lax.dot_general(v, kk, TN_DIMS, preferred_element_type=F32)
            state_s[hd] = dec_s[c:c + 1, kl] * st + kv_t
            o = _rms_scale(o, nw) * ggs_ref[rows, vl].astype(F32)
            o_ref[rows, vl] = o.astype(BF16)


def _gla(gq, gk, g, gv, ggs, norm_w, batch, seq):
    tm = TM_GLA
    nt = seq // tm
    row = lambda b, i: (b * nt + i, 0)
    fixed = lambda b, i: (0, 0)
    T = gq.shape[0]
    return pl.pallas_call(
        _gla_kernel,
        grid=(batch, nt),
        in_specs=[
            pl.BlockSpec((tm, GLA_QK), row),
            pl.BlockSpec((tm, GLA_QK), row),
            pl.BlockSpec((tm, GLA_QK), row),
            pl.BlockSpec((tm, GLA_WIDTH), row),
            pl.BlockSpec((tm, GLA_WIDTH), row),
            pl.BlockSpec((1, GLA_DV), fixed),
        ],
        out_specs=pl.BlockSpec((tm, GLA_WIDTH), row),
        out_shape=jax.ShapeDtypeStruct((T, GLA_WIDTH), BF16),
        scratch_shapes=[
            pltpu.VMEM((tm, GLA_QK), BF16),
            pltpu.VMEM((tm, GLA_QK), BF16),
            pltpu.VMEM((tm, GLA_QK), BF16),
            pltpu.VMEM((tm // CHUNK, GLA_QK), F32),
            pltpu.VMEM((GLA_HEADS, GLA_DV, GLA_DK), F32),
        ],
        compiler_params=pltpu.CompilerParams(dimension_semantics=("parallel", "arbitrary"),
                                             vmem_limit_bytes=VMEM_LIMIT),
        name="gla",
    )(gq, gk, g, gv, ggs, norm_w)


def _diff_attn_kernel(lam_ref, q_ref, k_ref, v_ref, gs_ref, nw_ref, o_ref,
                      qq_s, m_s, l_s, acc_s, *, lam_init):
    tq = q_ref.shape[0]
    tk = TK_ATT
    i = pl.program_id(2)

    q = q_ref[...]
    lane = lax.broadcasted_iota(jnp.int32, q.shape, 1)
    zero = jnp.zeros_like(q)
    qq_s[0:tq, :] = jnp.where(lane < DIFF_DQK, q, zero)
    qq_s[tq:, :] = jnp.where(lane >= DIFF_DQK, q, zero)
    m_s[...] = jnp.full_like(m_s, -jnp.inf)
    l_s[...] = jnp.zeros_like(l_s)
    acc_s[...] = jnp.zeros_like(acc_s)

    def step(j, masked):
        start = pl.multiple_of(j * tk, tk)
        k = k_ref[pl.ds(start, tk), :]
        v = v_ref[pl.ds(start, tk), :]
        s = lax.dot_general(qq_s[...], k, NT_DIMS, preferred_element_type=F32)
        if masked:
            qc = (lax.broadcasted_iota(jnp.int32, s.shape, 0) % tq) // CHUNK
            kc = lax.broadcasted_iota(jnp.int32, s.shape, 1) // CHUNK
            s = jnp.where(kc <= qc, s, -jnp.inf)
        m_old = m_s[...]
        m_new = jnp.maximum(m_old, jnp.max(s, axis=-1, keepdims=True))
        p = jnp.exp(s - m_new)
        alpha = jnp.exp(m_old - m_new)
        l_s[...] = alpha * l_s[...] + jnp.sum(p, axis=-1, keepdims=True)
        acc_s[...] = alpha * acc_s[...] + jnp.dot(p.astype(BF16), v, preferred_element_type=F32)
        m_s[...] = m_new

    def body(j, carry):
        step(j, False)
        return carry

    lax.fori_loop(0, i, body, 0)
    step(i, True)

    lv = lam_ref[...]
    lam = (jnp.exp(jnp.sum(lv[0:1] * lv[1:2], axis=-1, keepdims=True))
           - jnp.exp(jnp.sum(lv[2:3] * lv[3:4], axis=-1, keepdims=True)) + lam_init)
    o0 = acc_s[0:tq, :] / l_s[0:tq, :]
    o1 = acc_s[tq:, :] / l_s[tq:, :]
    o = o0 - lam * o1
    o = _rms_scale(o, nw_ref[...]) * (1.0 - lam_init) * gs_ref[...].astype(F32)
    o_ref[...] = o.astype(BF16)


def _diff_attn(lam_vecs, dq, dk, dv, dgs, norm_w, batch, seq, lam_init):
    tq = TQ_ATT
    assert TK_ATT == tq and tq % CHUNK == 0
    nq = seq // tq
    T = dq.shape[0]
    qmap = lambda b, h, i: (b * nq + i, h)
    kvmap = lambda b, h, i: (b, h)
    fixed = lambda b, h, i: (0, 0)
    return pl.pallas_call(
        functools.partial(_diff_attn_kernel, lam_init=lam_init),
        grid=(batch, DIFF_HEADS, nq),
        in_specs=[
            pl.BlockSpec(lam_vecs.shape, fixed),
            pl.BlockSpec((tq, DIFF_DV), qmap),
            pl.BlockSpec((seq, DIFF_DV), kvmap),
            pl.BlockSpec((seq, DIFF_DV), kvmap),
            pl.BlockSpec((tq, DIFF_DV), qmap),
            pl.BlockSpec((1, DIFF_DV), fixed),
        ],
        out_specs=pl.BlockSpec((tq, DIFF_DV), qmap),
        out_shape=jax.ShapeDtypeStruct((T, DIFF_WIDTH), BF16),
        scratch_shapes=[
            pltpu.VMEM((2 * tq, DIFF_DV), BF16),
            pltpu.VMEM((2 * tq, 1), F32),
            pltpu.VMEM((2 * tq, 1), F32),
            pltpu.VMEM((2 * tq, DIFF_DV), F32),
        ],
        compiler_params=pltpu.CompilerParams(
            dimension_semantics=("parallel", "parallel", "arbitrary"),
            vmem_limit_bytes=VMEM_LIMIT),
        name="diff_attn",
    )(lam_vecs, dq, dk, dv, dgs, norm_w)


def _mem_kv_kernel(mem_ref, nw_ref, w_ref, k_ref, v_ref):
    d = mem_ref.shape[1]
    mn = _rms_scale(mem_ref[...], nw_ref[...]).astype(BF16)
    k_ref[...] = jnp.dot(mn, w_ref[:, :d], preferred_element_type=F32).astype(BF16)
    v_ref[...] = jnp.dot(mn, w_ref[:, d:], preferred_element_type=F32).astype(BF16)


def _mem_kv(mem2d, norm_w, w_ckv, batch, mem_len):
    D = mem2d.shape[1]
    row = lambda b: (b, 0)
    fixed = lambda b: (0, 0)
    shp = jax.ShapeDtypeStruct((batch * mem_len, D), BF16)
    return pl.pallas_call(
        _mem_kv_kernel,
        grid=(batch,),
        in_specs=[pl.BlockSpec((mem_len, D), row), pl.BlockSpec((1, D), fixed),
                  pl.BlockSpec(w_ckv.shape, fixed)],
        out_specs=(pl.BlockSpec((mem_len, D), row), pl.BlockSpec((mem_len, D), row)),
        out_shape=(shp, shp),
        compiler_params=pltpu.CompilerParams(dimension_semantics=("parallel",),
                                             vmem_limit_bytes=VMEM_LIMIT),
        name="mem_kv",
    )(mem2d, norm_w, w_ckv)


def _out_cross_kernel(x_ref, og_ref, od_ref, wo_ref, ncw_ref, wq_ref, k_ref, v_ref, wco_ref,
                      fnw_ref, o_ref, ctx_s, *, final_norm):
    d = x_ref.shape[1]
    dh = d // CROSS_HEADS
    x1 = (x_ref[...]
          + jnp.dot(og_ref[...], wo_ref[0:GLA_WIDTH, :], preferred_element_type=F32)
          + jnp.dot(od_ref[...], wo_ref[GLA_WIDTH:, :], preferred_element_type=F32))
    hc = _rms_scale(x1, ncw_ref[...]).astype(BF16)
    q = (jnp.dot(hc, wq_ref[...], preferred_element_type=F32) * (dh ** -0.5)).astype(BF16)
    for hd in range(CROSS_HEADS):
        sl = slice(hd * dh, (hd + 1) * dh)
        s = lax.dot_general(q[:, sl], k_ref[:, sl], NT_DIMS, preferred_element_type=F32)
        p = jnp.exp(s - jnp.max(s, axis=-1, keepdims=True))
        p = p / jnp.sum(p, axis=-1, keepdims=True)
        ctx_s[:, sl] = jnp.dot(p.astype(BF16), v_ref[:, sl], preferred_element_type=F32).astype(BF16)
    x2 = x1 + jnp.dot(ctx_s[...], wco_ref[...], preferred_element_type=F32)
    if final_norm:
        x2 = _rms_scale(x2, fnw_ref[...])
    o_ref[...] = x2


def _out_cross(x2d, og, od, w_out, ncw, w_cq, kc, vc, w_co, fnw, seq, mem_len, final_norm):
    T, D = x2d.shape
    tm = TM_OUT
    per_b = seq // tm
    row = lambda i: (i, 0)
    fixed = lambda i: (0, 0)
    bmap = lambda i: (i // per_b, 0)
    return pl.pallas_call(
        functools.partial(_out_cross_kernel, final_norm=final_norm),
        grid=(T // tm,),
        in_specs=[
            pl.BlockSpec((tm, D), row),
            pl.BlockSpec((tm, GLA_WIDTH), row),
            pl.BlockSpec((tm, DIFF_WIDTH), row),
            pl.BlockSpec(w_out.shape, fixed),
            pl.BlockSpec((1, D), fixed),
            pl.BlockSpec(w_cq.shape, fixed),
            pl.BlockSpec((mem_len, D), bmap),
            pl.BlockSpec((mem_len, D), bmap),
            pl.BlockSpec(w_co.shape, fixed),
            pl.BlockSpec((1, D), fixed),
        ],
        out_specs=pl.BlockSpec((tm, D), row),
        out_shape=jax.ShapeDtypeStruct((T, D), F32),
        scratch_shapes=[pltpu.VMEM((tm, D), BF16)],
        compiler_params=pltpu.CompilerParams(dimension_semantics=("parallel",),
                                             vmem_limit_bytes=VMEM_LIMIT),
        name="out_cross",
    )(x2d, og, od, w_out, ncw, w_cq, kc, vc, w_co, fnw)


def _rope_lane_tables(positions):
    pos = positions.astype(F32)
    inv_freq = ROPE_THETA ** (-(jnp.arange(0, ROPE_DIM, 2, dtype=F32) / ROPE_DIM))
    ang = pos[..., None] * inv_freq
    cos, sin = jnp.cos(ang), jnp.sin(ang)
    ones = jnp.ones(ang.shape[:-1] + (DIFF_DQK - ROPE_DIM,), F32)
    cos_map = jnp.concatenate([cos, cos, ones], axis=-1)
    sin_map = jnp.concatenate([-sin, sin, 0.0 * ones], axis=-1)
    cos_t = jnp.concatenate([cos_map, cos_map], axis=-1).reshape(-1, LANES)
    sin_t = jnp.concatenate([sin_map, sin_map], axis=-1).reshape(-1, LANES)
    return cos_t, sin_t


def kernel(x, mem, positions, norm_mix_w, w_in, w_alpha2, b_alpha2, gla_norm_w,
           lam_q1, lam_k1, lam_q2, lam_k2, diff_norm_w, w_out,
           norm_cross_w, norm_mem_w, w_cq, w_ckv, w_co, final_norm_w):
    B, S, D = x.shape
    M = mem.shape[1]
    depth = w_in.shape[0]
    T = B * S
    cos_t, sin_t = _rope_lane_tables(positions)
    x2d = x.reshape(T, D)
    mem2d = mem.reshape(B * M, D)

    c_gq, c_gk, c_gv, c_gg = 0, GLA_QK, 2 * GLA_QK, 2 * GLA_QK + GLA_WIDTH
    c_ga = c_gg + GLA_WIDTH
    c_dq = c_ga + GLA_LOWRANK

    for l in range(depth):
        wl = w_in[l]
        wg = wl[:, c_gq:c_ga].astype(BF16)
        wga = jnp.pad(wl[:, c_ga:c_dq], ((0, 0), (0, LANES - GLA_LOWRANK))).astype(BF16)
        wd = wl[:, c_dq:].astype(BF16)
        wa2 = jnp.pad(w_alpha2[l], ((0, LANES - GLA_LOWRANK), (0, 0))).astype(BF16)
        ba2 = b_alpha2[l].reshape(1, GLA_QK).astype(F32)

        gq, gk, gv, ggs, g, dq, dk, dv, dgs = _in_proj(
            x2d, norm_mix_w[l].reshape(1, D), wg, wga, wa2, ba2, wd, cos_t, sin_t)

        o_gla = _gla(gq, gk, g, gv, ggs, gla_norm_w[l].reshape(1, GLA_DV), B, S)

        lam_init = 0.8 - 0.6 * math.exp(-0.3 * l)
        lam_vecs = jnp.stack([lam_q1[l], lam_k1[l], lam_q2[l], lam_k2[l]]).astype(F32)
        o_diff = _diff_attn(lam_vecs, dq, dk, dv, dgs, diff_norm_w[l].reshape(1, DIFF_DV),
                            B, S, lam_init)

        kc, vc = _mem_kv(mem2d, norm_mem_w[l].reshape(1, D), w_ckv[l].astype(BF16), B, M)

        x2d = _out_cross(x2d, o_gla, o_diff, w_out[l].astype(BF16),
                         norm_cross_w[l].reshape(1, D), w_cq[l].astype(BF16), kc, vc,
                         w_co[l].astype(BF16), final_norm_w.reshape(1, D), S, M,
                         final_norm=(l == depth - 1))
    return x2d.reshape(B, S, D)
```

```python
import functools
import math

import jax
import jax.numpy as jnp
from jax import lax
from jax.experimental import pallas as pl
from jax.experimental.pallas import tpu as pltpu

F32 = jnp.float32
BF16 = jnp.bfloat16

CHUNK = 64
GLA_HEADS = 4
GLA_DK = 64
GLA_DV = 128
GLA_QK = GLA_HEADS * GLA_DK
GLA_WIDTH = GLA_HEADS * GLA_DV
GLA_LOWRANK = 16
GLA_TAU = 16.0
DIFF_HEADS = 4
DIFF_DQK = 64
DIFF_DV = 128
DIFF_WIDTH = DIFF_HEADS * DIFF_DV
ROPE_DIM = 16
ROPE_THETA = 500000.0
CROSS_HEADS = 4
EPS = 1e-6

LANES = 128

TM_PROJ = 512
TM_GLA = 512
TQ_ATT = 256
TK_ATT = 256
TM_OUT = 512
VMEM_LIMIT = 56 * 1024 * 1024

NT_DIMS = (((1,), (1,)), ((), ()))
TN_DIMS = (((0,), (0,)), ((), ()))


def _rms_scale(xf, w):
    return xf * lax.rsqrt(jnp.mean(xf * xf, axis=-1, keepdims=True) + EPS) * w


def _silu(t):
    return t * (1.0 / (1.0 + jnp.exp(-t)))


def _in_proj_kernel(x_ref, nw_ref, wg_ref, wga_ref, wa2_ref, ba2_ref, wd_ref, cos_ref, sin_ref,
                    gq_ref, gk_ref, gv_ref, ggs_ref, g_ref, dq_ref, dk_ref, dv_ref, dgs_ref):
    h = _rms_scale(x_ref[...], nw_ref[...]).astype(BF16)

    gq_ref[...] = jnp.dot(h, wg_ref[:, 0:GLA_QK], preferred_element_type=F32) * (GLA_DK ** -0.5)
    gk_ref[...] = jnp.dot(h, wg_ref[:, GLA_QK:2 * GLA_QK], preferred_element_type=F32)
    gv_ref[...] = jnp.dot(h, wg_ref[:, 2 * GLA_QK:2 * GLA_QK + GLA_WIDTH],
                          preferred_element_type=F32).astype(BF16)
    gate = jnp.dot(h, wg_ref[:, 2 * GLA_QK + GLA_WIDTH:], preferred_element_type=F32)
    ggs_ref[...] = _silu(gate).astype(BF16)

    ga = jnp.dot(h, wga_ref[...], preferred_element_type=F32).astype(BF16)
    logit = jnp.dot(ga, wa2_ref[...], preferred_element_type=F32) + ba2_ref[...]
    log_sig = jnp.minimum(logit, 0.0) - jnp.log1p(jnp.exp(-jnp.abs(logit)))
    g_ref[...] = log_sig * (1.0 / GLA_TAU)

    cos = cos_ref[...]
    sin = sin_ref[...]
    lane = lax.broadcasted_iota(jnp.int32, cos.shape, 1)
    first_half = (lane % DIFF_DQK) < (ROPE_DIM // 2)

    def rope(t):
        partner = jnp.where(first_half, pltpu.roll(t, LANES - ROPE_DIM // 2, 1),
                            pltpu.roll(t, ROPE_DIM // 2, 1))
        return t * cos + partner * sin

    for hd in range(DIFF_HEADS):
        sl = slice(hd * DIFF_DV, (hd + 1) * DIFF_DV)
        q = jnp.dot(h, wd_ref[:, sl], preferred_element_type=F32)
        dq_ref[:, sl] = (rope(q) * (DIFF_DQK ** -0.5)).astype(BF16)
        k = jnp.dot(h, wd_ref[:, DIFF_WIDTH + hd * DIFF_DV:DIFF_WIDTH + (hd + 1) * DIFF_DV],
                    preferred_element_type=F32)
        dk_ref[:, sl] = rope(k).astype(BF16)
    dv_ref[...] = jnp.dot(h, wd_ref[:, 2 * DIFF_WIDTH:3 * DIFF_WIDTH],
                          preferred_element_type=F32).astype(BF16)
    dgate = jnp.dot(h, wd_ref[:, 3 * DIFF_WIDTH:], preferred_element_type=F32)
    dgs_ref[...] = _silu(dgate).astype(BF16)


def _in_proj(x2d, norm_w, wg, wga, wa2, ba2, wd, cos_t, sin_t):
    T, D = x2d.shape
    tm = TM_PROJ
    row = lambda i: (i, 0)
    fixed = lambda i: (0, 0)
    out_shapes = (
        jax.ShapeDtypeStruct((T, GLA_QK), F32),
        jax.ShapeDtypeStruct((T, GLA_QK), F32),
        jax.ShapeDtypeStruct((T, GLA_WIDTH), BF16),
        jax.ShapeDtypeStruct((T, GLA_WIDTH), BF16),
        jax.ShapeDtypeStruct((T, GLA_QK), F32),
        jax.ShapeDtypeStruct((T, DIFF_WIDTH), BF16),
        jax.ShapeDtypeStruct((T, DIFF_WIDTH), BF16),
        jax.ShapeDtypeStruct((T, DIFF_WIDTH), BF16),
        jax.ShapeDtypeStruct((T, DIFF_WIDTH), BF16),
    )
    return pl.pallas_call(
        _in_proj_kernel,
        grid=(T // tm,),
        in_specs=[
            pl.BlockSpec((tm, D), row),
            pl.BlockSpec((1, D), fixed),
            pl.BlockSpec(wg.shape, fixed),
            pl.BlockSpec(wga.shape, fixed),
            pl.BlockSpec(wa2.shape, fixed),
            pl.BlockSpec(ba2.shape, fixed),
            pl.BlockSpec(wd.shape, fixed),
            pl.BlockSpec((tm, LANES), row),
            pl.BlockSpec((tm, LANES), row),
        ],
        out_specs=tuple(pl.BlockSpec((tm, s.shape[1]), row) for s in out_shapes),
        out_shape=out_shapes,
        compiler_params=pltpu.CompilerParams(dimension_semantics=("parallel",),
                                             vmem_limit_bytes=VMEM_LIMIT),
        name="in_proj",
    )(x2d, norm_w, wg, wga, wa2, ba2, wd, cos_t, sin_t)


def _gla_kernel(gq_ref, gk_ref, g_ref, gv_ref, ggs_ref, nw_ref, o_ref,
                qd_s, ki_s, kl_s, dec_s, state_s):
    tm = gq_ref.shape[0]
    n_chunks = tm // CHUNK

    @pl.when(pl.program_id(1) == 0)
    def _():
        state_s[...] = jnp.zeros_like(state_s)

    g = g_ref[...]
    row_in_chunk = lax.broadcasted_iota(jnp.int32, g.shape, 0) % CHUNK
    G = g
    shift = 1
    while shift < CHUNK:
        G = G + jnp.where(row_in_chunk >= shift, pltpu.roll(G, shift, 0), 0.0)
        shift *= 2
    q = gq_ref[...]
    k = gk_ref[...]
    qd_s[...] = (q * jnp.exp(G)).astype(BF16)
    ki_s[...] = (k * jnp.exp(-G)).astype(BF16)
    for c in range(n_chunks):
        rows = slice(c * CHUNK, (c + 1) * CHUNK)
        g_last = G[(c + 1) * CHUNK - 1:(c + 1) * CHUNK, :]
        kl_s[rows, :] = (k[rows, :] * jnp.exp(g_last - G[rows, :])).astype(BF16)
        dec_s[c:c + 1, :] = jnp.exp(g_last)

    causal = (lax.broadcasted_iota(jnp.int32, (CHUNK, CHUNK), 1)
              <= lax.broadcasted_iota(jnp.int32, (CHUNK, CHUNK), 0))
    nw = nw_ref[...]

    for c in range(n_chunks):
        rows = slice(c * CHUNK, (c + 1) * CHUNK)
        for hd in range(GLA_HEADS):
            kl = slice(hd * GLA_DK, (hd + 1) * GLA_DK)
            vl = slice(hd * GLA_DV, (hd + 1) * GLA_DV)
            qd = qd_s[rows, kl]
            ki = ki_s[rows, kl]
            kk = kl_s[rows, kl]
            v = gv_ref[rows, vl]
            st = state_s[hd]
            scores = lax.dot_general(qd, ki, NT_DIMS, preferred_element_type=F32)
            scores = jnp.where(causal, scores, 0.0).astype(BF16)
            o = (jnp.dot(scores, v, preferred_element_type=F32)
                 + lax.dot_general(qd, st.astype(BF16), NT_DIMS, preferred_element_type=F32))
            kv_t = lax.dot_general(v, kk, TN_DIMS, preferred_element_type=F32)
            state_s[hd] = dec_s[c:c + 1, kl] * st + kv_t
            o = _rms_scale(o, nw) * ggs_ref[rows, vl].astype(F32)
            o_ref[rows, vl] = o.astype(BF16)


def _gla(gq, gk, g, gv, ggs, norm_w, batch, seq):
    tm = TM_GLA
    nt = seq // tm
    row = lambda b, i: (b * nt + i, 0)
    fixed = lambda b, i: (0, 0)
    T = gq.shape[0]
    return pl.pallas_call(
        _gla_kernel,
        grid=(batch, nt),
        in_specs=[
            pl.BlockSpec((tm, GLA_QK), row),
            pl.BlockSpec((tm, GLA_QK), row),
            pl.BlockSpec((tm, GLA_QK), row),
            pl.BlockSpec((tm, GLA_WIDTH), row),
            pl.BlockSpec((tm, GLA_WIDTH), row),
            pl.BlockSpec((1, GLA_DV), fixed),
        ],
        out_specs=pl.BlockSpec((tm, GLA_WIDTH), row),
        out_shape=jax.ShapeDtypeStruct((T, GLA_WIDTH), BF16),
        scratch_shapes=[
            pltpu.VMEM((tm, GLA_QK), BF16),
            pltpu.VMEM((tm, GLA_QK), BF16),
            pltpu.VMEM((tm, GLA_QK), BF16),
            pltpu.VMEM((tm // CHUNK, GLA_QK), F32),
            pltpu.VMEM((GLA_HEADS, GLA_DV, GLA_DK), F32),
        ],
        compiler_params=pltpu.CompilerParams(dimension_semantics=("parallel", "arbitrary"),
                                             vmem_limit_bytes=VMEM_LIMIT),
        name="gla",
    )(gq, gk, g, gv, ggs, norm_w)


def _diff_attn_kernel(lam_ref, q_ref, k_ref, v_ref, gs_ref, nw_ref, o_ref,
                      qq_s, m_s, acc_s, *, lam_init):
    seq = q_ref.shape[0]
    tq = TQ_ATT
    tk = TK_ATT
    rep = tk // LANES

    lv = lam_ref[...]
    lam = (jnp.exp(jnp.sum(lv[0:1] * lv[1:2], axis=-1, keepdims=True))
           - jnp.exp(jnp.sum(lv[2:3] * lv[3:4], axis=-1, keepdims=True)) + lam_init)
    nw = nw_ref[...]
    ones = jnp.ones((tk, LANES), BF16)

    def q_tile(i, carry):
        q_start = pl.multiple_of(i * tq, tq)
        q = q_ref[pl.ds(q_start, tq), :]
        lane = lax.broadcasted_iota(jnp.int32, q.shape, 1)
        zero = jnp.zeros_like(q)
        qq_s[0:tq, :] = jnp.where(lane < DIFF_DQK, q, zero)
        qq_s[tq:, :] = jnp.where(lane >= DIFF_DQK, q, zero)
        m_s[...] = jnp.full_like(m_s, -jnp.inf)
        acc_s[...] = jnp.zeros_like(acc_s)

        def step(j, masked):
            start = pl.multiple_of(j * tk, tk)
            k = k_ref[pl.ds(start, tk), :]
            v1 = jnp.concatenate([v_ref[pl.ds(start, tk), :], ones], axis=1)
            s = lax.dot_general(qq_s[...], k, NT_DIMS, preferred_element_type=F32)
            if masked:
                qc = (lax.broadcasted_iota(jnp.int32, s.shape, 0) % tq) // CHUNK
                kc = lax.broadcasted_iota(jnp.int32, s.shape, 1) // CHUNK
                s = jnp.where(kc <= qc, s, -jnp.inf)
            m_old = m_s[...]
            m_new = jnp.maximum(m_old, jnp.max(s, axis=-1, keepdims=True))
            p = jnp.exp(s - jnp.concatenate([m_new] * rep, axis=1))
            alpha = jnp.exp(m_old - m_new)
            acc_s[...] = (jnp.concatenate([alpha, alpha], axis=1) * acc_s[...]
                          + jnp.dot(p.astype(BF16), v1, preferred_element_type=F32))
            m_s[...] = m_new

        def pair(jj, c):
            step(2 * jj, False)
            step(2 * jj + 1, False)
            return c

        lax.fori_loop(0, lax.shift_right_logical(i, 1), pair, 0)

        @pl.when((i & 1) == 1)
        def _():
            step(i - 1, False)

        step(i, True)

        o0 = acc_s[0:tq, 0:DIFF_DV] / acc_s[0:tq, DIFF_DV:]
        o1 = acc_s[tq:, 0:DIFF_DV] / acc_s[tq:, DIFF_DV:]
        o = o0 - lam * o1
        o = _rms_scale(o, nw) * (1.0 - lam_init) * gs_ref[pl.ds(q_start, tq), :].astype(F32)
        o_ref[pl.ds(q_start, tq), :] = o.astype(BF16)
        return carry

    lax.fori_loop(0, seq // tq, q_tile, 0)


def _diff_attn(lam_vecs, dq, dk, dv, dgs, norm_w, batch, seq, lam_init):
    tq = TQ_ATT
    assert TK_ATT == tq and tq % CHUNK == 0 and seq % tq == 0
    T = dq.shape[0]
    bh = lambda b, h: (b, h)
    fixed = lambda b, h: (0, 0)
    blk = pl.BlockSpec((seq, DIFF_DV), bh)
    return pl.pallas_call(
        functools.partial(_diff_attn_kernel, lam_init=lam_init),
        grid=(batch, DIFF_HEADS),
        in_specs=[pl.BlockSpec(lam_vecs.shape, fixed), blk, blk, blk, blk,
                  pl.BlockSpec((1, DIFF_DV), fixed)],
        out_specs=blk,
        out_shape=jax.ShapeDtypeStruct((T, DIFF_WIDTH), BF16),
        scratch_shapes=[
            pltpu.VMEM((2 * tq, DIFF_DV), BF16),
            pltpu.VMEM((2 * tq, LANES), F32),
            pltpu.VMEM((2 * tq, 2 * DIFF_DV), F32),
        ],
        compiler_params=pltpu.CompilerParams(dimension_semantics=("parallel", "parallel"),
                                             vmem_limit_bytes=VMEM_LIMIT),
        name="diff_attn",
    )(lam_vecs, dq, dk, dv, dgs, norm_w)


def _mem_kv_kernel(mem_ref, nw_ref, w_ref, k_ref, v_ref):
    d = mem_ref.shape[1]
    mn = _rms_scale(mem_ref[...], nw_ref[...]).astype(BF16)
    k_ref[...] = jnp.dot(mn, w_ref[:, :d], preferred_element_type=F32).astype(BF16)
    v_ref[...] = jnp.dot(mn, w_ref[:, d:], preferred_element_type=F32).astype(BF16)


def _mem_kv(mem2d, norm_w, w_ckv, batch, mem_len):
    D = mem2d.shape[1]
    row = lambda b: (b, 0)
    fixed = lambda b: (0, 0)
    shp = jax.ShapeDtypeStruct((batch * mem_len, D), BF16)
    return pl.pallas_call(
        _mem_kv_kernel,
        grid=(batch,),
        in_specs=[pl.BlockSpec((mem_len, D), row), pl.BlockSpec((1, D), fixed),
                  pl.BlockSpec(w_ckv.shape, fixed)],
        out_specs=(pl.BlockSpec((mem_len, D), row), pl.BlockSpec((mem_len, D), row)),
        out_shape=(shp, shp),
        compiler_params=pltpu.CompilerParams(dimension_semantics=("parallel",),
                                             vmem_limit_bytes=VMEM_LIMIT),
        name="mem_kv",
    )(mem2d, norm_w, w_ckv)


def _out_cross_kernel(x_ref, og_ref, od_ref, wo_ref, ncw_ref, wq_ref, k_ref, v_ref, wco_ref,
                      fnw_ref, o_ref, ctx_s, *, final_norm):
    d = x_ref.shape[1]
    dh = d // CROSS_HEADS
    x1 = (x_ref[...]
          + jnp.dot(og_ref[...], wo_ref[0:GLA_WIDTH, :], preferred_element_type=F32)
          + jnp.dot(od_ref[...], wo_ref[GLA_WIDTH:, :], preferred_element_type=F32))
    hc = _rms_scale(x1, ncw_ref[...]).astype(BF16)
    q = (jnp.dot(hc, wq_ref[...], preferred_element_type=F32) * (dh ** -0.5)).astype(BF16)
    for hd in range(CROSS_HEADS):
        sl = slice(hd * dh, (hd + 1) * dh)
        s = lax.dot_general(q[:, sl], k_ref[:, sl], NT_DIMS, preferred_element_type=F32)
        p = jnp.exp(s - jnp.max(s, axis=-1, keepdims=True))
        p = p / jnp.sum(p, axis=-1, keepdims=True)
        ctx_s[:, sl] = jnp.dot(p.astype(BF16), v_ref[:, sl], preferred_element_type=F32).astype(BF16)
    x2 = x1 + jnp.dot(ctx_s[...], wco_ref[...], preferred_element_type=F32)
    if final_norm:
        x2 = _rms_scale(x2, fnw_ref[...])
    o_ref[...] = x2


def _out_cross(x2d, og, od, w_out, ncw, w_cq, kc, vc, w_co, fnw, seq, mem_len, final_norm):
    T, D = x2d.shape
    tm = TM_OUT
    per_b = seq // tm
    row = lambda i: (i, 0)
    fixed = lambda i: (0, 0)
    bmap = lambda i: (i // per_b, 0)
    return pl.pallas_call(
        functools.partial(_out_cross_kernel, final_norm=final_norm),
        grid=(T // tm,),
        in_specs=[
            pl.BlockSpec((tm, D), row),
            pl.BlockSpec((tm, GLA_WIDTH), row),
            pl.BlockSpec((tm, DIFF_WIDTH), row),
            pl.BlockSpec(w_out.shape, fixed),
            pl.BlockSpec((1, D), fixed),
            pl.BlockSpec(w_cq.shape, fixed),
            pl.BlockSpec((mem_len, D), bmap),
            pl.BlockSpec((mem_len, D), bmap),
            pl.BlockSpec(w_co.shape, fixed),
            pl.BlockSpec((1, D), fixed),
        ],
        out_specs=pl.BlockSpec((tm, D), row),
        out_shape=jax.ShapeDtypeStruct((T, D), F32),
        scratch_shapes=[pltpu.VMEM((tm, D), BF16)],
        compiler_params=pltpu.CompilerParams(dimension_semantics=("parallel",),
                                             vmem_limit_bytes=VMEM_LIMIT),
        name="out_cross",
    )(x2d, og, od, w_out, ncw, w_cq, kc, vc, w_co, fnw)


def _rope_lane_tables(positions):
    pos = positions.astype(F32)
    inv_freq = ROPE_THETA ** (-(jnp.arange(0, ROPE_DIM, 2, dtype=F32) / ROPE_DIM))
    ang = pos[..., None] * inv_freq
    cos, sin = jnp.cos(ang), jnp.sin(ang)
    ones = jnp.ones(ang.shape[:-1] + (DIFF_DQK - ROPE_DIM,), F32)
    cos_map = jnp.concatenate([cos, cos, ones], axis=-1)
    sin_map = jnp.concatenate([-sin, sin, 0.0 * ones], axis=-1)
    cos_t = jnp.concatenate([cos_map, cos_map], axis=-1).reshape(-1, LANES)
    sin_t = jnp.concatenate([sin_map, sin_map], axis=-1).reshape(-1, LANES)
    return cos_t, sin_t


def kernel(x, mem, positions, norm_mix_w, w_in, w_alpha2, b_alpha2, gla_norm_w,
           lam_q1, lam_k1, lam_q2, lam_k2, diff_norm_w, w_out,
           norm_cross_w, norm_mem_w, w_cq, w_ckv, w_co, final_norm_w):
    B, S, D = x.shape
    M = mem.shape[1]
    depth = w_in.shape[0]
    T = B * S
    cos_t, sin_t = _rope_lane_tables(positions)
    x2d = x.reshape(T, D)
    mem2d = mem.reshape(B * M, D)

    c_gq, c_gk, c_gv, c_gg = 0, GLA_QK, 2 * GLA_QK, 2 * GLA_QK + GLA_WIDTH
    c_ga = c_gg + GLA_WIDTH
    c_dq = c_ga + GLA_LOWRANK

    for l in range(depth):
        wl = w_in[l]
        wg = wl[:, c_gq:c_ga].astype(BF16)
        wga = jnp.pad(wl[:, c_ga:c_dq], ((0, 0), (0, LANES - GLA_LOWRANK))).astype(BF16)
        wd = wl[:, c_dq:].astype(BF16)
        wa2 = jnp.pad(w_alpha2[l], ((0, LANES - GLA_LOWRANK), (0, 0))).astype(BF16)
        ba2 = b_alpha2[l].reshape(1, GLA_QK).astype(F32)

        gq, gk, gv, ggs, g, dq, dk, dv, dgs = _in_proj(
            x2d, norm_mix_w[l].reshape(1, D), wg, wga, wa2, ba2, wd, cos_t, sin_t)

        o_gla = _gla(gq, gk, g, gv, ggs, gla_norm_w[l].reshape(1, GLA_DV), B, S)

        lam_init = 0.8 - 0.6 * math.exp(-0.3 * l)
        lam_vecs = jnp.stack([lam_q1[l], lam_k1[l], lam_q2[l], lam_k2[l]]).astype(F32)
        o_diff = _diff_attn(lam_vecs, dq, dk, dv, dgs, diff_norm_w[l].reshape(1, DIFF_DV),
                            B, S, lam_init)

        kc, vc = _mem_kv(mem2d, norm_mem_w[l].reshape(1, D), w_ckv[l].astype(BF16), B, M)

        x2d = _out_cross(x2d, o_gla, o_diff, w_out[l].astype(BF16),
                         norm_cross_w[l].reshape(1, D), w_cq[l].astype(BF16), kc, vc,
                         w_co[l].astype(BF16), final_norm_w.reshape(1, D), S, M,
                         final_norm=(l == depth - 1))
    return x2d.reshape(B, S, D)
```

```python
import functools
import math

import jax
import jax.numpy as jnp
import numpy as np
from jax import lax
from jax.experimental import pallas as pl
from jax.experimental.pallas import tpu as pltpu

F32 = jnp.float32
BF16 = jnp.bfloat16

CHUNK = 64
GLA_HEADS = 4
GLA_DK = 64
GLA_DV = 128
GLA_QK = GLA_HEADS * GLA_DK
GLA_WIDTH = GLA_HEADS * GLA_DV
GLA_LOWRANK = 16
GLA_TAU = 16.0
DIFF_HEADS = 4
DIFF_DQK = 64
DIFF_DV = 128
DIFF_WIDTH = DIFF_HEADS * DIFF_DV
ROPE_DIM = 16
ROPE_THETA = 500000.0
CROSS_HEADS = 4
EPS = 1e-6
LOG2E = math.log2(math.e)

LANES = 128

TM_PROJ = 512
TM_GLA = 512
TQ_ATT = 1024
TK_ATT = 256
TM_OUT = 512
VMEM_LIMIT = 56 * 1024 * 1024

NT_DIMS = (((1,), (1,)), ((), ()))
TN_DIMS = (((0,), (0,)), ((), ()))


def _rms_scale(xf, w):
    return xf * lax.rsqrt(jnp.mean(xf * xf, axis=-1, keepdims=True) + EPS) * w


def _silu(t):
    return t * (1.0 / (1.0 + jnp.exp(-t)))


def _in_proj_kernel(x_ref, nw_ref, wg_ref, wga_ref, wa2_ref, ba2_ref, wd_ref, cs_ref, sel_ref,
                    gq_ref, gk_ref, gv_ref, ggs_ref, g_ref, dq_ref, dk_ref, dv_ref, dgs_ref):
    h = _rms_scale(x_ref[...], nw_ref[...]).astype(BF16)

    gq_ref[...] = jnp.dot(h, wg_ref[:, 0:GLA_QK], preferred_element_type=F32) * (GLA_DK ** -0.5)
    gk_ref[...] = jnp.dot(h, wg_ref[:, GLA_QK:2 * GLA_QK], preferred_element_type=F32)
    gv_ref[...] = jnp.dot(h, wg_ref[:, 2 * GLA_QK:2 * GLA_QK + GLA_WIDTH],
                          preferred_element_type=F32).astype(BF16)
    gate = jnp.dot(h, wg_ref[:, 2 * GLA_QK + GLA_WIDTH:], preferred_element_type=F32)
    ggs_ref[...] = _silu(gate).astype(BF16)

    ga = jnp.dot(h, wga_ref[...], preferred_element_type=F32).astype(BF16)
    logit = jnp.dot(ga, wa2_ref[...], preferred_element_type=F32) + ba2_ref[...]
    log_sig = jnp.minimum(logit, 0.0) - jnp.log1p(jnp.exp(-jnp.abs(logit)))
    g_ref[...] = log_sig * (1.0 / GLA_TAU)

    cs = cs_ref[...]
    c1 = cs.astype(BF16)
    r1 = cs - c1.astype(F32)
    c2 = r1.astype(BF16)
    c3 = (r1 - c2.astype(F32)).astype(BF16)
    sel = sel_ref[...]
    tabs = (jnp.dot(c1, sel, preferred_element_type=F32)
            + jnp.dot(c2, sel, preferred_element_type=F32)
            + jnp.dot(c3, sel, preferred_element_type=F32))
    lane = lax.broadcasted_iota(jnp.int32, (cs.shape[0], LANES), 1)
    cos = tabs[:, :LANES] + jnp.where((lane % DIFF_DQK) < ROPE_DIM, 0.0, 1.0)
    sin = tabs[:, LANES:]
    first_half = (lane % DIFF_DQK) < (ROPE_DIM // 2)

    def rope(t):
        partner = jnp.where(first_half, pltpu.roll(t, LANES - ROPE_DIM // 2, 1),
                            pltpu.roll(t, ROPE_DIM // 2, 1))
        return t * cos + partner * sin

    for hd in range(DIFF_HEADS):
        sl = slice(hd * DIFF_DV, (hd + 1) * DIFF_DV)
        q = jnp.dot(h, wd_ref[:, sl], preferred_element_type=F32)
        dq_ref[:, sl] = (rope(q) * (DIFF_DQK ** -0.5 * LOG2E)).astype(BF16)
        k = jnp.dot(h, wd_ref[:, DIFF_WIDTH + hd * DIFF_DV:DIFF_WIDTH + (hd + 1) * DIFF_DV],
                    preferred_element_type=F32)
        dk_ref[:, sl] = rope(k).astype(BF16)
    dv_ref[...] = jnp.dot(h, wd_ref[:, 2 * DIFF_WIDTH:3 * DIFF_WIDTH],
                          preferred_element_type=F32).astype(BF16)
    dgate = jnp.dot(h, wd_ref[:, 3 * DIFF_WIDTH:], preferred_element_type=F32)
    dgs_ref[...] = _silu(dgate).astype(BF16)


def _in_proj(x2d, norm_w, wg, wga, wa2, ba2, wd, cs_t, sel):
    T, D = x2d.shape
    tm = TM_PROJ
    row = lambda i: (i, 0)
    fixed = lambda i: (0, 0)
    out_shapes = (
        jax.ShapeDtypeStruct((T, GLA_QK), F32),
        jax.ShapeDtypeStruct((T, GLA_QK), F32),
        jax.ShapeDtypeStruct((T, GLA_WIDTH), BF16),
        jax.ShapeDtypeStruct((T, GLA_WIDTH), BF16),
        jax.ShapeDtypeStruct((T, GLA_QK), F32),
        jax.ShapeDtypeStruct((T, DIFF_WIDTH), BF16),
        jax.ShapeDtypeStruct((T, DIFF_WIDTH), BF16),
        jax.ShapeDtypeStruct((T, DIFF_WIDTH), BF16),
        jax.ShapeDtypeStruct((T, DIFF_WIDTH), BF16),
    )
    return pl.pallas_call(
        _in_proj_kernel,
        grid=(T // tm,),
        in_specs=[
            pl.BlockSpec((tm, D), row),
            pl.BlockSpec((1, D), fixed),
            pl.BlockSpec(wg.shape, fixed),
            pl.BlockSpec(wga.shape, fixed),
            pl.BlockSpec(wa2.shape, fixed),
            pl.BlockSpec(ba2.shape, fixed),
            pl.BlockSpec(wd.shape, fixed),
            pl.BlockSpec((tm, ROPE_DIM), row),
            pl.BlockSpec(sel.shape, fixed),
        ],
        out_specs=tuple(pl.BlockSpec((tm, s.shape[1]), row) for s in out_shapes),
        out_shape=out_shapes,
        compiler_params=pltpu.CompilerParams(dimension_semantics=("parallel",),
                                             vmem_limit_bytes=VMEM_LIMIT),
        name="in_proj",
    )(x2d, norm_w, wg, wga, wa2, ba2, wd, cs_t, sel)


def _gla_kernel(gq_ref, gk_ref, g_ref, gv_ref, ggs_ref, nw_ref, o_ref,
                qd_s, ki_s, kl_s, dec_s, state_s):
    tm = gq_ref.shape[0]
    n_chunks = tm // CHUNK

    @pl.when(pl.program_id(1) == 0)
    def _():
        state_s[...] = jnp.zeros_like(state_s)

    g = g_ref[...]
    row_in_chunk = lax.broadcasted_iota(jnp.int32, g.shape, 0) % CHUNK
    G = g
    shift = 1
    while shift < CHUNK:
        G = G + jnp.where(row_in_chunk >= shift, pltpu.roll(G, shift, 0), 0.0)
        shift *= 2
    q = gq_ref[...]
    k = gk_ref[...]
    qd_s[...] = (q * jnp.exp(G)).astype(BF16)
    ki_s[...] = (k * jnp.exp(-G)).astype(BF16)
    for c in range(n_chunks):
        rows = slice(c * CHUNK, (c + 1) * CHUNK)
        g_last = G[(c + 1) * CHUNK - 1:(c + 1) * CHUNK, :]
        kl_s[rows, :] = (k[rows, :] * jnp.exp(g_last - G[rows, :])).astype(BF16)
        dec_s[c:c + 1, :] = jnp.exp(g_last)

    causal = (lax.broadcasted_iota(jnp.int32, (CHUNK, CHUNK), 1)
              <= lax.broadcasted_iota(jnp.int32, (CHUNK, CHUNK), 0))
    nw = nw_ref[...]

    for c in range(n_chunks):
        rows = slice(c * CHUNK, (c + 1) * CHUNK)
        for hd in range(GLA_HEADS):
            kl = slice(hd * GLA_DK, (hd + 1) * GLA_DK)
            vl = slice(hd * GLA_DV, (hd + 1) * GLA_DV)
            qd = qd_s[rows, kl]
            ki = ki_s[rows, kl]
            kk = kl_s[rows, kl]
            v = gv_ref[rows, vl]
            st = state_s[hd]
            scores = lax.dot_general(qd, ki, NT_DIMS, preferred_element_type=F32)
            scores = jnp.where(causal, scores, 0.0).astype(BF16)
            o = (jnp.dot(scores, v, preferred_element_type=F32)
                 + lax.dot_general(qd, st.astype(BF16), NT_DIMS, preferred_element_type=F32))
            kv_t = lax.dot_general(v, kk, TN_DIMS, preferred_element_type=F32)
            state_s[hd] = dec_s[c:c + 1, kl] * st + kv_t
            o = _rms_scale(o, nw) * ggs_ref[rows, vl].astype(F32)
            o_ref[rows, vl] = o.astype(BF16)


def _gla(gq, gk, g, gv, ggs, norm_w, batch, seq):
    tm = TM_GLA
    nt = seq // tm
    row = lambda b, i: (b * nt + i, 0)
    fixed = lambda b, i: (0, 0)
    T = gq.shape[0]
    return pl.pallas_call(
        _gla_kernel,
        grid=(batch, nt),
        in_specs=[
            pl.BlockSpec((tm, GLA_QK), row),
            pl.BlockSpec((tm, GLA_QK), row),
            pl.BlockSpec((tm, GLA_QK), row),
            pl.BlockSpec((tm, GLA_WIDTH), row),
            pl.BlockSpec((tm, GLA_WIDTH), row),
            pl.BlockSpec((1, GLA_DV), fixed),
        ],
        out_specs=pl.BlockSpec((tm, GLA_WIDTH), row),
        out_shape=jax.ShapeDtypeStruct((T, GLA_WIDTH), BF16),
        scratch_shapes=[
            pltpu.VMEM((tm, GLA_QK), BF16),
            pltpu.VMEM((tm, GLA_QK), BF16),
            pltpu.VMEM((tm, GLA_QK), BF16),
            pltpu.VMEM((tm // CHUNK, GLA_QK), F32),
            pltpu.VMEM((GLA_HEADS, GLA_DV, GLA_DK), F32),
        ],
        compiler_params=pltpu.CompilerParams(dimension_semantics=("parallel", "arbitrary"),
                                             vmem_limit_bytes=VMEM_LIMIT),
        name="gla",
    )(gq, gk, g, gv, ggs, norm_w)


def _diff_attn_kernel(lam_ref, q_ref, k_ref, v_ref, gs_ref, nw_ref, o_ref,
                      qq_s, m_s, acc_s, *, lam_init):
    seq = q_ref.shape[0]
    tq = TQ_ATT
    tk = TK_ATT
    n_sub = tq // tk
    rep = tk // LANES

    lv = lam_ref[...]
    lam = (jnp.exp(jnp.sum(lv[0:1] * lv[1:2], axis=-1, keepdims=True))
           - jnp.exp(jnp.sum(lv[2:3] * lv[3:4], axis=-1, keepdims=True)) + lam_init)
    nw = nw_ref[...]
    ones = jnp.ones((tk, LANES), BF16)

    def step(key_start, row_lo, staircase):
        k = k_ref[pl.ds(key_start, tk), :]
        v1 = jnp.concatenate([v_ref[pl.ds(key_start, tk), :], ones], axis=1)
        s = lax.dot_general(qq_s[row_lo:, :], k, NT_DIMS, preferred_element_type=F32)
        if staircase:
            qc = (lax.broadcasted_iota(jnp.int32, (2 * tk, tk), 0) % tk) // CHUNK
            kc = lax.broadcasted_iota(jnp.int32, (2 * tk, tk), 1) // CHUNK
            head = jnp.where(kc <= qc, s[0:2 * tk, :], -jnp.inf)
            s = head if s.shape[0] == 2 * tk else jnp.concatenate([head, s[2 * tk:, :]], axis=0)
        m_old = m_s[row_lo:, :]
        m_new = jnp.maximum(m_old, jnp.max(s, axis=-1, keepdims=True))
        p = jnp.exp2(s - jnp.concatenate([m_new] * rep, axis=1))
        alpha = jnp.exp2(m_old - m_new)
        acc_s[row_lo:, :] = (jnp.concatenate([alpha, alpha], axis=1) * acc_s[row_lo:, :]
                             + jnp.dot(p.astype(BF16), v1, preferred_element_type=F32))
        m_s[row_lo:, :] = m_new

    def q_tile(i, carry):
        q_start = pl.multiple_of(i * tq, tq)
        lane = lax.broadcasted_iota(jnp.int32, (tk, LANES), 1)
        for r in range(n_sub):
            q = q_ref[pl.ds(q_start + r * tk, tk), :]
            zero = jnp.zeros_like(q)
            qq_s[2 * r * tk:(2 * r + 1) * tk, :] = jnp.where(lane < DIFF_DQK, q, zero)
            qq_s[(2 * r + 1) * tk:(2 * r + 2) * tk, :] = jnp.where(lane >= DIFF_DQK, q, zero)
        m_s[...] = jnp.full_like(m_s, -jnp.inf)
        acc_s[...] = jnp.zeros_like(acc_s)

        def past_keys(jj, c):
            for u in range(n_sub):
                step(pl.multiple_of((jj * n_sub + u) * tk, tk), 0, False)
            return c

        lax.fori_loop(0, i, past_keys, 0)
        for d in range(n_sub):
            step(pl.multiple_of(q_start + d * tk, tk), 2 * d * tk, True)

        for r in range(n_sub):
            lo = 2 * r * tk
            o0 = acc_s[lo:lo + tk, 0:DIFF_DV] / acc_s[lo:lo + tk, DIFF_DV:]
            o1 = acc_s[lo + tk:lo + 2 * tk, 0:DIFF_DV] / acc_s[lo + tk:lo + 2 * tk, DIFF_DV:]
            o = o0 - lam * o1
            rows = pl.ds(q_start + r * tk, tk)
            o = _rms_scale(o, nw) * (1.0 - lam_init) * gs_ref[rows, :].astype(F32)
            o_ref[rows, :] = o.astype(BF16)
        return carry

    lax.fori_loop(0, seq // tq, q_tile, 0)


def _diff_attn(lam_vecs, dq, dk, dv, dgs, norm_w, batch, seq, lam_init):
    tq = TQ_ATT
    assert tq % TK_ATT == 0 and TK_ATT % CHUNK == 0 and seq % tq == 0
    T = dq.shape[0]
    bh = lambda b, h: (b, h)
    fixed = lambda b, h: (0, 0)
    blk = pl.BlockSpec((seq, DIFF_DV), bh)
    return pl.pallas_call(
        functools.partial(_diff_attn_kernel, lam_init=lam_init),
        grid=(batch, DIFF_HEADS),
        in_specs=[pl.BlockSpec(lam_vecs.shape, fixed), blk, blk, blk, blk,
                  pl.BlockSpec((1, DIFF_DV), fixed)],
        out_specs=blk,
        out_shape=jax.ShapeDtypeStruct((T, DIFF_WIDTH), BF16),
        scratch_shapes=[
            pltpu.VMEM((2 * tq, DIFF_DV), BF16),
            pltpu.VMEM((2 * tq, LANES), F32),
            pltpu.VMEM((2 * tq, 2 * DIFF_DV), F32),
        ],
        compiler_params=pltpu.CompilerParams(dimension_semantics=("parallel", "parallel"),
                                             vmem_limit_bytes=VMEM_LIMIT),
        name="diff_attn",
    )(lam_vecs, dq, dk, dv, dgs, norm_w)


def _mem_kv_kernel(mem_ref, nw_ref, w_ref, k_ref, v_ref):
    d = mem_ref.shape[1]
    mn = _rms_scale(mem_ref[...], nw_ref[...]).astype(BF16)
    k_ref[...] = jnp.dot(mn, w_ref[:, :d], preferred_element_type=F32).astype(BF16)
    v_ref[...] = jnp.dot(mn, w_ref[:, d:], preferred_element_type=F32).astype(BF16)


def _mem_kv(mem2d, norm_w, w_ckv, batch, mem_len):
    D = mem2d.shape[1]
    row = lambda b: (b, 0)
    fixed = lambda b: (0, 0)
    shp = jax.ShapeDtypeStruct((batch * mem_len, D), BF16)
    return pl.pallas_call(
        _mem_kv_kernel,
        grid=(batch,),
        in_specs=[pl.BlockSpec((mem_len, D), row), pl.BlockSpec((1, D), fixed),
                  pl.BlockSpec(w_ckv.shape, fixed)],
        out_specs=(pl.BlockSpec((mem_len, D), row), pl.BlockSpec((mem_len, D), row)),
        out_shape=(shp, shp),
        compiler_params=pltpu.CompilerParams(dimension_semantics=("parallel",),
                                             vmem_limit_bytes=VMEM_LIMIT),
        name="mem_kv",
    )(mem2d, norm_w, w_ckv)


def _out_cross_kernel(x_ref, og_ref, od_ref, wo_ref, ncw_ref, wq_ref, k_ref, v_ref, wco_ref,
                      fnw_ref, o_ref, ctx_s, *, final_norm):
    d = x_ref.shape[1]
    dh = d // CROSS_HEADS
    x1 = (x_ref[...]
          + jnp.dot(og_ref[...], wo_ref[0:GLA_WIDTH, :], preferred_element_type=F32)
          + jnp.dot(od_ref[...], wo_ref[GLA_WIDTH:, :], preferred_element_type=F32))
    hc = _rms_scale(x1, ncw_ref[...]).astype(BF16)
    q = (jnp.dot(hc, wq_ref[...], preferred_element_type=F32) * (dh ** -0.5)).astype(BF16)
    for hd in range(CROSS_HEADS):
        sl = slice(hd * dh, (hd + 1) * dh)
        s = lax.dot_general(q[:, sl], k_ref[:, sl], NT_DIMS, preferred_element_type=F32)
        p = jnp.exp(s - jnp.max(s, axis=-1, keepdims=True))
        p = p / jnp.sum(p, axis=-1, keepdims=True)
        ctx_s[:, sl] = jnp.dot(p.astype(BF16), v_ref[:, sl], preferred_element_type=F32).astype(BF16)
    x2 = x1 + jnp.dot(ctx_s[...], wco_ref[...], preferred_element_type=F32)
    if final_norm:
        x2 = _rms_scale(x2, fnw_ref[...])
    o_ref[...] = x2


def _out_cross(x2d, og, od, w_out, ncw, w_cq, kc, vc, w_co, fnw, seq, mem_len, final_norm):
    T, D = x2d.shape
    tm = TM_OUT
    per_b = seq // tm
    row = lambda i: (i, 0)
    fixed = lambda i: (0, 0)
    bmap = lambda i: (i // per_b, 0)
    return pl.pallas_call(
        functools.partial(_out_cross_kernel, final_norm=final_norm),
        grid=(T // tm,),
        in_specs=[
            pl.BlockSpec((tm, D), row),
            pl.BlockSpec((tm, GLA_WIDTH), row),
            pl.BlockSpec((tm, DIFF_WIDTH), row),
            pl.BlockSpec(w_out.shape, fixed),
            pl.BlockSpec((1, D), fixed),
            pl.BlockSpec(w_cq.shape, fixed),
            pl.BlockSpec((mem_len, D), bmap),
            pl.BlockSpec((mem_len, D), bmap),
            pl.BlockSpec(w_co.shape, fixed),
            pl.BlockSpec((1, D), fixed),
        ],
        out_specs=pl.BlockSpec((tm, D), row),
        out_shape=jax.ShapeDtypeStruct((T, D), F32),
        scratch_shapes=[pltpu.VMEM((tm, D), BF16)],
        compiler_params=pltpu.CompilerParams(dimension_semantics=("parallel",),
                                             vmem_limit_bytes=VMEM_LIMIT),
        name="out_cross",
    )(x2d, og, od, w_out, ncw, w_cq, kc, vc, w_co, fnw)


def _rope_tables(positions):
    half = ROPE_DIM // 2
    T = positions.size
    per_row = LANES // half
    inv_freq = ROPE_THETA ** (-(jnp.arange(0, ROPE_DIM, 2, dtype=F32) / ROPE_DIM))
    pos = positions.reshape(T // per_row, per_row).astype(F32)
    ang = jnp.repeat(pos, half, axis=1) * jnp.tile(inv_freq, per_row)
    return jnp.concatenate([jnp.cos(ang).reshape(T, half), jnp.sin(ang).reshape(T, half)], axis=1)


def _rope_select_matrix():
    half = ROPE_DIM // 2
    sel = np.zeros((ROPE_DIM, 2 * LANES), np.float32)
    for j in range(LANES):
        jm = j % DIFF_DQK
        if jm < ROPE_DIM:
            sel[jm % half, j] = 1.0
            sel[half + jm % half, LANES + j] = -1.0 if jm < half else 1.0
    return jnp.asarray(sel, BF16)


def kernel(x, mem, positions, norm_mix_w, w_in, w_alpha2, b_alpha2, gla_norm_w,
           lam_q1, lam_k1, lam_q2, lam_k2, diff_norm_w, w_out,
           norm_cross_w, norm_mem_w, w_cq, w_ckv, w_co, final_norm_w):
    B, S, D = x.shape
    M = mem.shape[1]
    depth = w_in.shape[0]
    T = B * S
    cs_t = _rope_tables(positions)
    sel = _rope_select_matrix()
    x2d = x.reshape(T, D)
    mem2d = mem.reshape(B * M, D)

    c_gq, c_gk, c_gv, c_gg = 0, GLA_QK, 2 * GLA_QK, 2 * GLA_QK + GLA_WIDTH
    c_ga = c_gg + GLA_WIDTH
    c_dq = c_ga + GLA_LOWRANK

    for l in range(depth):
        wl = w_in[l]
        wg = wl[:, c_gq:c_ga].astype(BF16)
        wga = jnp.pad(wl[:, c_ga:c_dq], ((0, 0), (0, LANES - GLA_LOWRANK))).astype(BF16)
        wd = wl[:, c_dq:].astype(BF16)
        wa2 = jnp.pad(w_alpha2[l], ((0, LANES - GLA_LOWRANK), (0, 0))).astype(BF16)
        ba2 = b_alpha2[l].reshape(1, GLA_QK).astype(F32)

        gq, gk, gv, ggs, g, dq, dk, dv, dgs = _in_proj(
            x2d, norm_mix_w[l].reshape(1, D), wg, wga, wa2, ba2, wd, cs_t, sel)

        o_gla = _gla(gq, gk, g, gv, ggs, gla_norm_w[l].reshape(1, GLA_DV), B, S)

        lam_init = 0.8 - 0.6 * math.exp(-0.3 * l)
        lam_vecs = jnp.stack([lam_q1[l], lam_k1[l], lam_q2[l], lam_k2[l]]).astype(F32)
        o_diff = _diff_attn(lam_vecs, dq, dk, dv, dgs, diff_norm_w[l].reshape(1, DIFF_DV),
                            B, S, lam_init)

        kc, vc = _mem_kv(mem2d, norm_mem_w[l].reshape(1, D), w_ckv[l].astype(BF16), B, M)

        x2d = _out_cross(x2d, o_gla, o_diff, w_out[l].astype(BF16),
                         norm_cross_w[l].reshape(1, D), w_cq[l].astype(BF16), kc, vc,
                         w_co[l].astype(BF16), final_norm_w.reshape(1, D), S, M,
                         final_norm=(l == depth - 1))
    return x2d.reshape(B, S, D)
```

```python
import functools
import math

import jax
import jax.numpy as jnp
import numpy as np
from jax import lax
from jax.experimental import pallas as pl
from jax.experimental.pallas import tpu as pltpu

F32 = jnp.float32
BF16 = jnp.bfloat16

CHUNK = 64
GLA_HEADS = 4
GLA_DK = 64
GLA_DV = 128
GLA_QK = GLA_HEADS * GLA_DK
GLA_WIDTH = GLA_HEADS * GLA_DV
GLA_LOWRANK = 16
GLA_TAU = 16.0
DIFF_HEADS = 4
DIFF_DQK = 64
DIFF_DV = 128
DIFF_WIDTH = DIFF_HEADS * DIFF_DV
ROPE_DIM = 16
ROPE_THETA = 500000.0
CROSS_HEADS = 4
EPS = 1e-6
LOG2E = math.log2(math.e)
ROPE_SPLIT = 3

LANES = 128

TM_PROJ = 1024
TM_GLA = 512
TQ_ATT = 1024
TK_ATT = 256
TM_OUT = 1024
VMEM_LIMIT = 56 * 1024 * 1024

NT_DIMS = (((1,), (1,)), ((), ()))
TN_DIMS = (((0,), (0,)), ((), ()))


def _rms_scale(xf, w):
    return xf * lax.rsqrt(jnp.mean(xf * xf, axis=-1, keepdims=True) + EPS) * w


def _silu(t):
    return t * (1.0 / (1.0 + jnp.exp(-t)))


def _in_proj_kernel(x_ref, nw_ref, wa_ref, wa2_ref, ba2_ref, wb_ref, cs_ref, sel_ref,
                    gq_ref, gk_ref, gv_ref, ggs_ref, g_ref, dq_ref, dk_ref, dv_ref, dgs_ref):
    h = _rms_scale(x_ref[...], nw_ref[...]).astype(BF16)

    def proj_b(group):
        return jnp.dot(h, wb_ref[:, group * GLA_WIDTH:(group + 1) * GLA_WIDTH],
                       preferred_element_type=F32)

    qkg = jnp.dot(h, wa_ref[...], preferred_element_type=F32)
    gq_ref[...] = qkg[:, 0:GLA_QK] * (GLA_DK ** -0.5)
    gk_ref[...] = qkg[:, GLA_QK:2 * GLA_QK]
    ga = qkg[:, 2 * GLA_QK:].astype(BF16)
    logit = jnp.dot(ga, wa2_ref[...], preferred_element_type=F32) + ba2_ref[...]
    log_sig = jnp.minimum(logit, 0.0) - jnp.log1p(jnp.exp(-jnp.abs(logit)))
    g_ref[...] = log_sig * (1.0 / GLA_TAU)

    gv_ref[...] = proj_b(0).astype(BF16)
    ggs_ref[...] = _silu(proj_b(1)).astype(BF16)

    tabs = jnp.dot(cs_ref[...], sel_ref[...], preferred_element_type=F32)
    lane = lax.broadcasted_iota(jnp.int32, (tabs.shape[0], LANES), 1)
    cos = tabs[:, :LANES] + jnp.where((lane % DIFF_DQK) < ROPE_DIM, 0.0, 1.0)
    sin = tabs[:, LANES:]
    first_half = (lane % DIFF_DQK) < (ROPE_DIM // 2)

    def rope(t):
        partner = jnp.where(first_half, pltpu.roll(t, LANES - ROPE_DIM // 2, 1),
                            pltpu.roll(t, ROPE_DIM // 2, 1))
        return t * cos + partner * sin

    q = proj_b(2)
    k = proj_b(3)
    for hd in range(DIFF_HEADS):
        sl = slice(hd * DIFF_DV, (hd + 1) * DIFF_DV)
        dq_ref[:, sl] = (rope(q[:, sl]) * (DIFF_DQK ** -0.5 * LOG2E)).astype(BF16)
        dk_ref[:, sl] = rope(k[:, sl]).astype(BF16)
    dv_ref[...] = proj_b(4).astype(BF16)
    dgs_ref[...] = _silu(proj_b(5)).astype(BF16)


def _in_proj(x2d, norm_w, wa, wa2, ba2, wb, cs_t, sel):
    T, D = x2d.shape
    tm = TM_PROJ
    row = lambda i: (i, 0)
    fixed = lambda i: (0, 0)
    out_shapes = (
        jax.ShapeDtypeStruct((T, GLA_QK), F32),
        jax.ShapeDtypeStruct((T, GLA_QK), F32),
        jax.ShapeDtypeStruct((T, GLA_WIDTH), BF16),
        jax.ShapeDtypeStruct((T, GLA_WIDTH), BF16),
        jax.ShapeDtypeStruct((T, GLA_QK), F32),
        jax.ShapeDtypeStruct((T, DIFF_WIDTH), BF16),
        jax.ShapeDtypeStruct((T, DIFF_WIDTH), BF16),
        jax.ShapeDtypeStruct((T, DIFF_WIDTH), BF16),
        jax.ShapeDtypeStruct((T, DIFF_WIDTH), BF16),
    )
    return pl.pallas_call(
        _in_proj_kernel,
        grid=(T // tm,),
        in_specs=[
            pl.BlockSpec((tm, D), row),
            pl.BlockSpec((1, D), fixed),
            pl.BlockSpec(wa.shape, fixed),
            pl.BlockSpec(wa2.shape, fixed),
            pl.BlockSpec(ba2.shape, fixed),
            pl.BlockSpec(wb.shape, fixed),
            pl.BlockSpec((tm, cs_t.shape[1]), row),
            pl.BlockSpec(sel.shape, fixed),
        ],
        out_specs=tuple(pl.BlockSpec((tm, s.shape[1]), row) for s in out_shapes),
        out_shape=out_shapes,
        compiler_params=pltpu.CompilerParams(dimension_semantics=("parallel",),
                                             vmem_limit_bytes=VMEM_LIMIT),
        name="in_proj",
    )(x2d, norm_w, wa, wa2, ba2, wb, cs_t, sel)


def _gla_kernel(gq_ref, gk_ref, g_ref, gv_ref, ggs_ref, nw_ref, o_ref,
                qd4_s, ki_s, kl4_s, dec_s, state_s):
    tm = gq_ref.shape[0]
    n_chunks = tm // CHUNK

    @pl.when(pl.program_id(1) == 0)
    def _():
        state_s[...] = jnp.zeros_like(state_s)

    g = g_ref[...]
    row_in_chunk = lax.broadcasted_iota(jnp.int32, g.shape, 0) % CHUNK
    G = g
    shift = 1
    while shift < CHUNK:
        G = G + jnp.where(row_in_chunk >= shift, pltpu.roll(G, shift, 0), 0.0)
        shift *= 2
    q = gq_ref[...]
    k = gk_ref[...]
    qd = (q * jnp.exp(G)).astype(BF16)
    ki_s[...] = (k * jnp.exp(-G)).astype(BF16)

    lane_head = lax.broadcasted_iota(jnp.int32, (CHUNK, GLA_QK), 1) // GLA_DK
    zero = jnp.zeros((CHUNK, GLA_QK), BF16)
    for c in range(n_chunks):
        rows = slice(c * CHUNK, (c + 1) * CHUNK)
        g_last = G[(c + 1) * CHUNK - 1:(c + 1) * CHUNK, :]
        kl = (k[rows, :] * jnp.exp(g_last - G[rows, :])).astype(BF16)
        dec_s[c:c + 1, :] = jnp.exp(g_last)
        for hd in range(GLA_HEADS):
            dst = slice((c * GLA_HEADS + hd) * CHUNK, (c * GLA_HEADS + hd + 1) * CHUNK)
            qd4_s[dst, :] = jnp.where(lane_head == hd, qd[rows, :], zero)
            kl4_s[dst, :] = jnp.where(lane_head == hd, kl, zero)

    r_i = lax.broadcasted_iota(jnp.int32, (GLA_HEADS * CHUNK, GLA_HEADS * CHUNK), 0)
    c_i = lax.broadcasted_iota(jnp.int32, (GLA_HEADS * CHUNK, GLA_HEADS * CHUNK), 1)
    keep = (r_i // CHUNK == c_i // CHUNK) & (c_i % CHUNK <= r_i % CHUNK)
    nw = nw_ref[...]

    for c in range(n_chunks):
        rows = slice(c * CHUNK, (c + 1) * CHUNK)
        rows4 = slice(c * GLA_HEADS * CHUNK, (c + 1) * GLA_HEADS * CHUNK)
        q4 = qd4_s[rows4, :]
        ki = ki_s[rows, :]
        k4 = jnp.concatenate([ki] * GLA_HEADS, axis=0)
        v4 = jnp.concatenate([gv_ref[rows, hd * GLA_DV:(hd + 1) * GLA_DV]
                              for hd in range(GLA_HEADS)], axis=0)
        st = state_s[...]
        scores = lax.dot_general(q4, k4, NT_DIMS, preferred_element_type=F32)
        scores = jnp.where(keep, scores, 0.0).astype(BF16)
        o = (jnp.dot(scores, v4, preferred_element_type=F32)
             + lax.dot_general(q4, st.astype(BF16), NT_DIMS, preferred_element_type=F32))
        kv_t = lax.dot_general(v4, kl4_s[rows4, :], TN_DIMS, preferred_element_type=F32)
        state_s[...] = dec_s[c:c + 1, :] * st + kv_t
        gate = jnp.concatenate([ggs_ref[rows, hd * GLA_DV:(hd + 1) * GLA_DV]
                                for hd in range(GLA_HEADS)], axis=0).astype(F32)
        o = (_rms_scale(o, nw) * gate).astype(BF16)
        for hd in range(GLA_HEADS):
            o_ref[rows, hd * GLA_DV:(hd + 1) * GLA_DV] = o[hd * CHUNK:(hd + 1) * CHUNK, :]


def _gla(gq, gk, g, gv, ggs, norm_w, batch, seq):
    tm = TM_GLA
    nt = seq // tm
    row = lambda b, i: (b * nt + i, 0)
    fixed = lambda b, i: (0, 0)
    T = gq.shape[0]
    return pl.pallas_call(
        _gla_kernel,
        grid=(batch, nt),
        in_specs=[
            pl.BlockSpec((tm, GLA_QK), row),
            pl.BlockSpec((tm, GLA_QK), row),
            pl.BlockSpec((tm, GLA_QK), row),
            pl.BlockSpec((tm, GLA_WIDTH), row),
            pl.BlockSpec((tm, GLA_WIDTH), row),
            pl.BlockSpec((1, GLA_DV), fixed),
        ],
        out_specs=pl.BlockSpec((tm, GLA_WIDTH), row),
        out_shape=jax.ShapeDtypeStruct((T, GLA_WIDTH), BF16),
        scratch_shapes=[
            pltpu.VMEM((GLA_HEADS * tm, GLA_QK), BF16),
            pltpu.VMEM((tm, GLA_QK), BF16),
            pltpu.VMEM((GLA_HEADS * tm, GLA_QK), BF16),
            pltpu.VMEM((tm // CHUNK, GLA_QK), F32),
            pltpu.VMEM((GLA_DV, GLA_QK), F32),
        ],
        compiler_params=pltpu.CompilerParams(dimension_semantics=("parallel", "arbitrary"),
                                             vmem_limit_bytes=VMEM_LIMIT),
        name="gla",
    )(gq, gk, g, gv, ggs, norm_w)


def _diff_attn_kernel(lam_ref, q_ref, k_ref, v_ref, gs_ref, nw_ref, o_ref,
                      qq_s, m_s, acc_s, *, lam_init):
    seq = q_ref.shape[0]
    tq = TQ_ATT
    tk = TK_ATT
    n_sub = tq // tk
    rep = tk // LANES

    lv = lam_ref[...]
    lam = (jnp.exp(jnp.sum(lv[0:1] * lv[1:2], axis=-1, keepdims=True))
           - jnp.exp(jnp.sum(lv[2:3] * lv[3:4], axis=-1, keepdims=True)) + lam_init)
    nw = nw_ref[...]
    ones = jnp.ones((tk, LANES), BF16)

    qc = (lax.broadcasted_iota(jnp.int32, (2 * tk, tk), 0) % tk) // CHUNK
    kc = lax.broadcasted_iota(jnp.int32, (2 * tk, tk), 1) // CHUNK
    stair_bias = jnp.where(kc <= qc, 0.0, -jnp.inf).astype(F32)

    def step(key_start, row_lo, staircase, first):
        k = k_ref[pl.ds(key_start, tk), :]
        v1 = jnp.concatenate([v_ref[pl.ds(key_start, tk), :], ones], axis=1)
        s = lax.dot_general(qq_s[row_lo:, :], k, NT_DIMS, preferred_element_type=F32)
        if staircase:
            head = s[0:2 * tk, :] + stair_bias
            s = head if s.shape[0] == 2 * tk else jnp.concatenate([head, s[2 * tk:, :]], axis=0)
        m_cur = jnp.max(s, axis=-1, keepdims=True)
        if first:
            m_new = jnp.broadcast_to(m_cur, (s.shape[0], LANES))
        else:
            m_old = m_s[row_lo:, :]
            m_new = jnp.maximum(m_old, m_cur)
        p = jnp.exp2(s - jnp.concatenate([m_new] * rep, axis=1))
        pv = jnp.dot(p.astype(BF16), v1, preferred_element_type=F32)
        if first:
            acc_s[row_lo:, :] = pv
        else:
            alpha = jnp.exp2(m_old - m_new)
            acc_s[row_lo:, :] = jnp.concatenate([alpha, alpha], axis=1) * acc_s[row_lo:, :] + pv
        m_s[row_lo:, :] = m_new

    def q_tile(i, carry):
        q_start = pl.multiple_of(i * tq, tq)
        lane = lax.broadcasted_iota(jnp.int32, (tk, LANES), 1)
        for r in range(n_sub):
            q = q_ref[pl.ds(q_start + r * tk, tk), :]
            zero = jnp.zeros_like(q)
            qq_s[2 * r * tk:(2 * r + 1) * tk, :] = jnp.where(lane < DIFF_DQK, q, zero)
            qq_s[(2 * r + 1) * tk:(2 * r + 2) * tk, :] = jnp.where(lane >= DIFF_DQK, q, zero)

        for d in range(n_sub):
            step(pl.multiple_of(q_start + d * tk, tk), 2 * d * tk, True, d == 0)

        def past_keys(jj, c):
            for u in range(n_sub):
                step(pl.multiple_of((jj * n_sub + u) * tk, tk), 0, False, False)
            return c

        lax.fori_loop(0, i, past_keys, 0)

        for r in range(n_sub):
            lo = 2 * r * tk
            o0 = acc_s[lo:lo + tk, 0:DIFF_DV] / acc_s[lo:lo + tk, DIFF_DV:]
            o1 = acc_s[lo + tk:lo + 2 * tk, 0:DIFF_DV] / acc_s[lo + tk:lo + 2 * tk, DIFF_DV:]
            o = o0 - lam * o1
            rows = pl.ds(q_start + r * tk, tk)
            o = _rms_scale(o, nw) * (1.0 - lam_init) * gs_ref[rows, :].astype(F32)
            o_ref[rows, :] = o.astype(BF16)
        return carry

    lax.fori_loop(0, seq // tq, q_tile, 0)


def _diff_attn(lam_vecs, dq, dk, dv, dgs, norm_w, batch, seq, lam_init):
    tq = TQ_ATT
    assert tq % TK_ATT == 0 and TK_ATT % CHUNK == 0 and seq % tq == 0
    T = dq.shape[0]
    bh = lambda b, h: (b, h)
    fixed = lambda b, h: (0, 0)
    blk = pl.BlockSpec((seq, DIFF_DV), bh)
    return pl.pallas_call(
        functools.partial(_diff_attn_kernel, lam_init=lam_init),
        grid=(batch, DIFF_HEADS),
        in_specs=[pl.BlockSpec(lam_vecs.shape, fixed), blk, blk, blk, blk,
                  pl.BlockSpec((1, DIFF_DV), fixed)],
        out_specs=blk,
        out_shape=jax.ShapeDtypeStruct((T, DIFF_WIDTH), BF16),
        scratch_shapes=[
            pltpu.VMEM((2 * tq, DIFF_DV), BF16),
            pltpu.VMEM((2 * tq, LANES), F32),
            pltpu.VMEM((2 * tq, 2 * DIFF_DV), F32),
        ],
        compiler_params=pltpu.CompilerParams(dimension_semantics=("parallel", "parallel"),
                                             vmem_limit_bytes=VMEM_LIMIT),
        name="diff_attn",
    )(lam_vecs, dq, dk, dv, dgs, norm_w)


def _mem_kv_kernel(mem_ref, nw_ref, w_ref, k_ref, v_ref):
    d = mem_ref.shape[1]
    mn = _rms_scale(mem_ref[...], nw_ref[...]).astype(BF16)
    k_ref[...] = jnp.dot(mn, w_ref[:, :d], preferred_element_type=F32).astype(BF16)
    v_ref[...] = jnp.dot(mn, w_ref[:, d:], preferred_element_type=F32).astype(BF16)


def _mem_kv(mem2d, norm_w, w_ckv, batch, mem_len):
    D = mem2d.shape[1]
    row = lambda b: (b, 0)
    fixed = lambda b: (0, 0)
    shp = jax.ShapeDtypeStruct((batch * mem_len, D), BF16)
    return pl.pallas_call(
        _mem_kv_kernel,
        grid=(batch,),
        in_specs=[pl.BlockSpec((mem_len, D), row), pl.BlockSpec((1, D), fixed),
                  pl.BlockSpec(w_ckv.shape, fixed)],
        out_specs=(pl.BlockSpec((mem_len, D), row), pl.BlockSpec((mem_len, D), row)),
        out_shape=(shp, shp),
        compiler_params=pltpu.CompilerParams(dimension_semantics=("parallel",),
                                             vmem_limit_bytes=VMEM_LIMIT),
        name="mem_kv",
    )(mem2d, norm_w, w_ckv)


def _out_cross_kernel(x_ref, og_ref, od_ref, wo_ref, ncw_ref, wq_ref, k_ref, v_ref, wco_ref,
                      fnw_ref, o_ref, ctx_s, *, final_norm):
    d = x_ref.shape[1]
    dh = d // CROSS_HEADS
    x1 = (x_ref[...]
          + jnp.dot(og_ref[...], wo_ref[0:GLA_WIDTH, :], preferred_element_type=F32)
          + jnp.dot(od_ref[...], wo_ref[GLA_WIDTH:, :], preferred_element_type=F32))
    hc = _rms_scale(x1, ncw_ref[...]).astype(BF16)
    q = (jnp.dot(hc, wq_ref[...], preferred_element_type=F32) * (dh ** -0.5)).astype(BF16)
    for hd in range(CROSS_HEADS):
        sl = slice(hd * dh, (hd + 1) * dh)
        s = lax.dot_general(q[:, sl], k_ref[:, sl], NT_DIMS, preferred_element_type=F32)
        p = jnp.exp(s - jnp.max(s, axis=-1, keepdims=True))
        p = p / jnp.sum(p, axis=-1, keepdims=True)
        ctx_s[:, sl] = jnp.dot(p.astype(BF16), v_ref[:, sl], preferred_element_type=F32).astype(BF16)
    x2 = x1 + jnp.dot(ctx_s[...], wco_ref[...], preferred_element_type=F32)
    if final_norm:
        x2 = _rms_scale(x2, fnw_ref[...])
    o_ref[...] = x2


def _out_cross(x2d, og, od, w_out, ncw, w_cq, kc, vc, w_co, fnw, seq, mem_len, final_norm):
    T, D = x2d.shape
    tm = TM_OUT
    per_b = seq // tm
    row = lambda i: (i, 0)
    fixed = lambda i: (0, 0)
    bmap = lambda i: (i // per_b, 0)
    return pl.pallas_call(
        functools.partial(_out_cross_kernel, final_norm=final_norm),
        grid=(T // tm,),
        in_specs=[
            pl.BlockSpec((tm, D), row),
            pl.BlockSpec((tm, GLA_WIDTH), row),
            pl.BlockSpec((tm, DIFF_WIDTH), row),
            pl.BlockSpec(w_out.shape, fixed),
            pl.BlockSpec((1, D), fixed),
            pl.BlockSpec(w_cq.shape, fixed),
            pl.BlockSpec((mem_len, D), bmap),
            pl.BlockSpec((mem_len, D), bmap),
            pl.BlockSpec(w_co.shape, fixed),
            pl.BlockSpec((1, D), fixed),
        ],
        out_specs=pl.BlockSpec((tm, D), row),
        out_shape=jax.ShapeDtypeStruct((T, D), F32),
        scratch_shapes=[pltpu.VMEM((tm, D), BF16)],
        compiler_params=pltpu.CompilerParams(dimension_semantics=("parallel",),
                                             vmem_limit_bytes=VMEM_LIMIT),
        name="out_cross",
    )(x2d, og, od, w_out, ncw, w_cq, kc, vc, w_co, fnw)


def _rope_tables(positions):
    half = ROPE_DIM // 2
    T = positions.size
    per_row = LANES // half
    inv_freq = ROPE_THETA ** (-(jnp.arange(0, ROPE_DIM, 2, dtype=F32) / ROPE_DIM))
    pos = positions.reshape(T // per_row, per_row).astype(F32)
    ang = jnp.repeat(pos, half, axis=1) * jnp.tile(inv_freq, per_row)
    pieces = []
    rest_c, rest_s = jnp.cos(ang), jnp.sin(ang)
    for _ in range(ROPE_SPLIT):
        pc, ps = rest_c.astype(BF16), rest_s.astype(BF16)
        pieces += [pc.reshape(T, half), ps.reshape(T, half)]
        rest_c, rest_s = rest_c - pc.astype(F32), rest_s - ps.astype(F32)
    return jnp.concatenate(pieces, axis=1)


def _rope_select_matrix():
    half = ROPE_DIM // 2
    sel = np.zeros((ROPE_DIM, 2 * LANES), np.float32)
    for j in range(LANES):
        jm = j % DIFF_DQK
        if jm < ROPE_DIM:
            sel[jm % half, j] = 1.0
            sel[half + jm % half, LANES + j] = -1.0 if jm < half else 1.0
    return jnp.asarray(np.tile(sel, (ROPE_SPLIT, 1)), BF16)


def kernel(x, mem, positions, norm_mix_w, w_in, w_alpha2, b_alpha2, gla_norm_w,
           lam_q1, lam_k1, lam_q2, lam_k2, diff_norm_w, w_out,
           norm_cross_w, norm_mem_w, w_cq, w_ckv, w_co, final_norm_w):
    B, S, D = x.shape
    M = mem.shape[1]
    depth = w_in.shape[0]
    T = B * S
    cs_t = _rope_tables(positions)
    sel = _rope_select_matrix()
    x2d = x.reshape(T, D)
    mem2d = mem.reshape(B * M, D)

    c_gv = 2 * GLA_QK
    c_ga = c_gv + 2 * GLA_WIDTH
    c_dq = c_ga + GLA_LOWRANK

    for l in range(depth):
        wl = w_in[l]
        ga_pad = jnp.pad(wl[:, c_ga:c_dq], ((0, 0), (0, LANES - GLA_LOWRANK)))
        wa = jnp.concatenate([wl[:, :c_gv], ga_pad], axis=1).astype(BF16)
        wb = jnp.concatenate([wl[:, c_gv:c_ga], wl[:, c_dq:]], axis=1).astype(BF16)
        wa2 = jnp.pad(w_alpha2[l], ((0, LANES - GLA_LOWRANK), (0, 0))).astype(BF16)
        ba2 = b_alpha2[l].reshape(1, GLA_QK).astype(F32)

        gq, gk, gv, ggs, g, dq, dk, dv, dgs = _in_proj(
            x2d, norm_mix_w[l].reshape(1, D), wa, wa2, ba2, wb, cs_t, sel)

        o_gla = _gla(gq, gk, g, gv, ggs, gla_norm_w[l].reshape(1, GLA_DV), B, S)

        lam_init = 0.8 - 0.6 * math.exp(-0.3 * l)
        lam_vecs = jnp.stack([lam_q1[l], lam_k1[l], lam_q2[l], lam_k2[l]]).astype(F32)
        o_diff = _diff_attn(lam_vecs, dq, dk, dv, dgs, diff_norm_w[l].reshape(1, DIFF_DV),
                            B, S, lam_init)

        kc, vc = _mem_kv(mem2d, norm_mem_w[l].reshape(1, D), w_ckv[l].astype(BF16), B, M)

        x2d = _out_cross(x2d, o_gla, o_diff, w_out[l].astype(BF16),
                         norm_cross_w[l].reshape(1, D), w_cq[l].astype(BF16), kc, vc,
                         w_co[l].astype(BF16), final_norm_w.reshape(1, D), S, M,
                         final_norm=(l == depth - 1))
    return x2d.reshape(B, S, D)
```

```python
import functools
import math

import jax
import jax.numpy as jnp
import numpy as np
from jax import lax
from jax.experimental import pallas as pl
from jax.experimental.pallas import tpu as pltpu

F32 = jnp.float32
BF16 = jnp.bfloat16

CHUNK = 64
GLA_HEADS = 4
GLA_DK = 64
GLA_DV = 128
GLA_QK = GLA_HEADS * GLA_DK
GLA_WIDTH = GLA_HEADS * GLA_DV
GLA_LOWRANK = 16
GLA_TAU = 16.0
DIFF_HEADS = 4
DIFF_DQK = 64
DIFF_DV = 128
DIFF_WIDTH = DIFF_HEADS * DIFF_DV
ROPE_DIM = 16
ROPE_THETA = 500000.0
CROSS_HEADS = 4
EPS = 1e-6
LOG2E = math.log2(math.e)
ROPE_SPLIT = 3

LANES = 128

TM_PROJ = 1024
TM_GLA = 512
TQ_ATT = 2048
TK_ATT = 256
TM_OUT = 1024
ROW_PARTS = 2
VMEM_LIMIT = 56 * 1024 * 1024

NT_DIMS = (((1,), (1,)), ((), ()))
TN_DIMS = (((0,), (0,)), ((), ()))


def _rms_scale(xf, w):
    return xf * lax.rsqrt(jnp.mean(xf * xf, axis=-1, keepdims=True) + EPS) * w


def _silu(t):
    return t * (1.0 / (1.0 + jnp.exp(-t)))


def _in_proj_kernel(x_ref, nw_ref, wa_ref, wa2_ref, ba2_ref, wb_ref, cs_ref, sel_ref,
                    gq_ref, gk_ref, gv_ref, ggs_ref, g_ref, dq_ref, dk_ref, dv_ref, dgs_ref):
    pm = x_ref.shape[0] // ROW_PARTS
    for part in range(ROW_PARTS):
        _in_proj_rows(slice(part * pm, (part + 1) * pm),
                      x_ref, nw_ref, wa_ref, wa2_ref, ba2_ref, wb_ref, cs_ref, sel_ref,
                      gq_ref, gk_ref, gv_ref, ggs_ref, g_ref, dq_ref, dk_ref, dv_ref, dgs_ref)


def _in_proj_rows(rows, x_ref, nw_ref, wa_ref, wa2_ref, ba2_ref, wb_ref, cs_ref, sel_ref,
                  gq_ref, gk_ref, gv_ref, ggs_ref, g_ref, dq_ref, dk_ref, dv_ref, dgs_ref):
    h = _rms_scale(x_ref[rows, :], nw_ref[...]).astype(BF16)

    def proj_b(group):
        return jnp.dot(h, wb_ref[:, group * GLA_WIDTH:(group + 1) * GLA_WIDTH],
                       preferred_element_type=F32)

    qkg = jnp.dot(h, wa_ref[...], preferred_element_type=F32)
    gq_ref[rows, :] = qkg[:, 0:GLA_QK] * (GLA_DK ** -0.5)
    gk_ref[rows, :] = qkg[:, GLA_QK:2 * GLA_QK]
    ga = qkg[:, 2 * GLA_QK:].astype(BF16)

    gv_ref[rows, :] = proj_b(0).astype(BF16)
    ggs_ref[rows, :] = _silu(proj_b(1)).astype(BF16)

    tabs = jnp.dot(cs_ref[rows, :], sel_ref[...], preferred_element_type=F32)
    lane = lax.broadcasted_iota(jnp.int32, (tabs.shape[0], LANES), 1)
    cos = tabs[:, :LANES] + jnp.where((lane % DIFF_DQK) < ROPE_DIM, 0.0, 1.0)
    sin = tabs[:, LANES:]
    first_half = (lane % DIFF_DQK) < (ROPE_DIM // 2)

    def rope(t):
        partner = jnp.where(first_half, pltpu.roll(t, LANES - ROPE_DIM // 2, 1),
                            pltpu.roll(t, ROPE_DIM // 2, 1))
        return t * cos + partner * sin

    q = proj_b(2)
    k = proj_b(3)
    for hd in range(DIFF_HEADS):
        sl = slice(hd * DIFF_DV, (hd + 1) * DIFF_DV)
        dq_ref[rows, sl] = (rope(q[:, sl]) * (DIFF_DQK ** -0.5 * LOG2E)).astype(BF16)
        dk_ref[rows, sl] = rope(k[:, sl]).astype(BF16)
    dv_ref[rows, :] = proj_b(4).astype(BF16)

    logit = jnp.dot(ga, wa2_ref[...], preferred_element_type=F32) + ba2_ref[...]
    log_sig = jnp.minimum(logit, 0.0) - jnp.log1p(jnp.exp(-jnp.abs(logit)))
    g_ref[rows, :] = log_sig * (1.0 / GLA_TAU)

    dgs_ref[rows, :] = _silu(proj_b(5)).astype(BF16)


def _in_proj(x2d, norm_w, wa, wa2, ba2, wb, cs_t, sel):
    T, D = x2d.shape
    tm = TM_PROJ
    row = lambda i: (i, 0)
    fixed = lambda i: (0, 0)
    out_shapes = (
        jax.ShapeDtypeStruct((T, GLA_QK), F32),
        jax.ShapeDtypeStruct((T, GLA_QK), F32),
        jax.ShapeDtypeStruct((T, GLA_WIDTH), BF16),
        jax.ShapeDtypeStruct((T, GLA_WIDTH), BF16),
        jax.ShapeDtypeStruct((T, GLA_QK), F32),
        jax.ShapeDtypeStruct((T, DIFF_WIDTH), BF16),
        jax.ShapeDtypeStruct((T, DIFF_WIDTH), BF16),
        jax.ShapeDtypeStruct((T, DIFF_WIDTH), BF16),
        jax.ShapeDtypeStruct((T, DIFF_WIDTH), BF16),
    )
    return pl.pallas_call(
        _in_proj_kernel,
        grid=(T // tm,),
        in_specs=[
            pl.BlockSpec((tm, D), row),
            pl.BlockSpec((1, D), fixed),
            pl.BlockSpec(wa.shape, fixed),
            pl.BlockSpec(wa2.shape, fixed),
            pl.BlockSpec(ba2.shape, fixed),
            pl.BlockSpec(wb.shape, fixed),
            pl.BlockSpec((tm, cs_t.shape[1]), row),
            pl.BlockSpec(sel.shape, fixed),
        ],
        out_specs=tuple(pl.BlockSpec((tm, s.shape[1]), row) for s in out_shapes),
        out_shape=out_shapes,
        compiler_params=pltpu.CompilerParams(dimension_semantics=("parallel",),
                                             vmem_limit_bytes=VMEM_LIMIT),
        name="in_proj",
    )(x2d, norm_w, wa, wa2, ba2, wb, cs_t, sel)


def _gla_kernel(gq_ref, gk_ref, g_ref, gv_ref, ggs_ref, nw_ref, o_ref,
                qd4_s, ki_s, kl4_s, dec_s, state_s):
    tm = gq_ref.shape[0]
    n_chunks = tm // CHUNK

    @pl.when(pl.program_id(1) == 0)
    def _():
        state_s[...] = jnp.zeros_like(state_s)

    g = g_ref[...]
    row_in_chunk = lax.broadcasted_iota(jnp.int32, g.shape, 0) % CHUNK
    G = g
    shift = 1
    while shift < CHUNK:
        G = G + jnp.where(row_in_chunk >= shift, pltpu.roll(G, shift, 0), 0.0)
        shift *= 2
    q = gq_ref[...]
    k = gk_ref[...]
    qd = (q * jnp.exp(G)).astype(BF16)
    ki_s[...] = (k * jnp.exp(-G)).astype(BF16)

    lane_head = lax.broadcasted_iota(jnp.int32, (CHUNK, GLA_QK), 1) // GLA_DK
    zero = jnp.zeros((CHUNK, GLA_QK), BF16)
    for c in range(n_chunks):
        rows = slice(c * CHUNK, (c + 1) * CHUNK)
        g_last = G[(c + 1) * CHUNK - 1:(c + 1) * CHUNK, :]
        kl = (k[rows, :] * jnp.exp(g_last - G[rows, :])).astype(BF16)
        dec_s[c:c + 1, :] = jnp.exp(g_last)
        for hd in range(GLA_HEADS):
            dst = slice((c * GLA_HEADS + hd) * CHUNK, (c * GLA_HEADS + hd + 1) * CHUNK)
            qd4_s[dst, :] = jnp.where(lane_head == hd, qd[rows, :], zero)
            kl4_s[dst, :] = jnp.where(lane_head == hd, kl, zero)

    r_i = lax.broadcasted_iota(jnp.int32, (GLA_HEADS * CHUNK, GLA_HEADS * CHUNK), 0)
    c_i = lax.broadcasted_iota(jnp.int32, (GLA_HEADS * CHUNK, GLA_HEADS * CHUNK), 1)
    keep = (r_i // CHUNK == c_i // CHUNK) & (c_i % CHUNK <= r_i % CHUNK)
    nw = nw_ref[...]

    for c in range(n_chunks):
        rows = slice(c * CHUNK, (c + 1) * CHUNK)
        rows4 = slice(c * GLA_HEADS * CHUNK, (c + 1) * GLA_HEADS * CHUNK)
        q4 = qd4_s[rows4, :]
        ki = ki_s[rows, :]
        k4 = jnp.concatenate([ki] * GLA_HEADS, axis=0)
        v4 = jnp.concatenate([gv_ref[rows, hd * GLA_DV:(hd + 1) * GLA_DV]
                              for hd in range(GLA_HEADS)], axis=0)
        st = state_s[...]
        scores = lax.dot_general(q4, k4, NT_DIMS, preferred_element_type=F32)
        scores = jnp.where(keep, scores, 0.0).astype(BF16)
        o = (jnp.dot(scores, v4, preferred_element_type=F32)
             + lax.dot_general(q4, st.astype(BF16), NT_DIMS, preferred_element_type=F32))
        kv_t = lax.dot_general(v4, kl4_s[rows4, :], TN_DIMS, preferred_element_type=F32)
        state_s[...] = dec_s[c:c + 1, :] * st + kv_t
        gate = jnp.concatenate([ggs_ref[rows, hd * GLA_DV:(hd + 1) * GLA_DV]
                                for hd in range(GLA_HEADS)], axis=0).astype(F32)
        o = (_rms_scale(o, nw) * gate).astype(BF16)
        for hd in range(GLA_HEADS):
            o_ref[rows, hd * GLA_DV:(hd + 1) * GLA_DV] = o[hd * CHUNK:(hd + 1) * CHUNK, :]


def _gla(gq, gk, g, gv, ggs, norm_w, batch, seq):
    tm = TM_GLA
    nt = seq // tm
    row = lambda b, i: (b * nt + i, 0)
    fixed = lambda b, i: (0, 0)
    T = gq.shape[0]
    return pl.pallas_call(
        _gla_kernel,
        grid=(batch, nt),
        in_specs=[
            pl.BlockSpec((tm, GLA_QK), row),
            pl.BlockSpec((tm, GLA_QK), row),
            pl.BlockSpec((tm, GLA_QK), row),
            pl.BlockSpec((tm, GLA_WIDTH), row),
            pl.BlockSpec((tm, GLA_WIDTH), row),
            pl.BlockSpec((1, GLA_DV), fixed),
        ],
        out_specs=pl.BlockSpec((tm, GLA_WIDTH), row),
        out_shape=jax.ShapeDtypeStruct((T, GLA_WIDTH), BF16),
        scratch_shapes=[
            pltpu.VMEM((GLA_HEADS * tm, GLA_QK), BF16),
            pltpu.VMEM((tm, GLA_QK), BF16),
            pltpu.VMEM((GLA_HEADS * tm, GLA_QK), BF16),
            pltpu.VMEM((tm // CHUNK, GLA_QK), F32),
            pltpu.VMEM((GLA_DV, GLA_QK), F32),
        ],
        compiler_params=pltpu.CompilerParams(dimension_semantics=("parallel", "arbitrary"),
                                             vmem_limit_bytes=VMEM_LIMIT),
        name="gla",
    )(gq, gk, g, gv, ggs, norm_w)


def _diff_attn_kernel(lam_ref, q_ref, k_ref, v_ref, gs_ref, nw_ref, o_ref,
                      qq_s, m_s, acc_s, *, lam_init):
    seq = q_ref.shape[0]
    tq = TQ_ATT
    tk = TK_ATT
    n_sub = tq // tk
    rep = tk // LANES

    lv = lam_ref[...]
    lam = (jnp.exp(jnp.sum(lv[0:1] * lv[1:2], axis=-1, keepdims=True))
           - jnp.exp(jnp.sum(lv[2:3] * lv[3:4], axis=-1, keepdims=True)) + lam_init)
    nw = nw_ref[...]
    ones = jnp.ones((tk, LANES), BF16)

    qc = (lax.broadcasted_iota(jnp.int32, (2 * tk, tk), 0) % tk) // CHUNK
    kc = lax.broadcasted_iota(jnp.int32, (2 * tk, tk), 1) // CHUNK
    stair_bias = jnp.where(kc <= qc, 0.0, -jnp.inf).astype(F32)

    def step(key_start, row_lo, staircase, first):
        k = k_ref[pl.ds(key_start, tk), :]
        v1 = jnp.concatenate([v_ref[pl.ds(key_start, tk), :], ones], axis=1)
        s = lax.dot_general(qq_s[row_lo:, :], k, NT_DIMS, preferred_element_type=F32)
        if staircase:
            head = s[0:2 * tk, :] + stair_bias
            s = head if s.shape[0] == 2 * tk else jnp.concatenate([head, s[2 * tk:, :]], axis=0)
        m_cur = jnp.max(s, axis=-1, keepdims=True)
        if first:
            m_new = jnp.broadcast_to(m_cur, (s.shape[0], LANES))
        else:
            m_old = m_s[row_lo:, :]
            m_new = jnp.maximum(m_old, m_cur)
        p = jnp.exp2(s - jnp.concatenate([m_new] * rep, axis=1))
        pv = jnp.dot(p.astype(BF16), v1, preferred_element_type=F32)
        if first:
            acc_s[row_lo:, :] = pv
        else:
            alpha = jnp.exp2(m_old - m_new)
            acc_s[row_lo:, :] = jnp.concatenate([alpha, alpha], axis=1) * acc_s[row_lo:, :] + pv
        m_s[row_lo:, :] = m_new

    def q_tile(i, carry):
        q_start = pl.multiple_of(i * tq, tq)
        lane = lax.broadcasted_iota(jnp.int32, (tk, LANES), 1)
        for r in range(n_sub):
            q = q_ref[pl.ds(q_start + r * tk, tk), :]
            zero = jnp.zeros_like(q)
            qq_s[2 * r * tk:(2 * r + 1) * tk, :] = jnp.where(lane < DIFF_DQK, q, zero)
            qq_s[(2 * r + 1) * tk:(2 * r + 2) * tk, :] = jnp.where(lane >= DIFF_DQK, q, zero)

        for d in range(n_sub):
            step(pl.multiple_of(q_start + d * tk, tk), 2 * d * tk, True, d == 0)

        def past_keys(jj, c):
            for u in range(n_sub):
                step(pl.multiple_of((jj * n_sub + u) * tk, tk), 0, False, False)
            return c

        lax.fori_loop(0, i, past_keys, 0)

        for r in range(n_sub):
            lo = 2 * r * tk
            o0 = acc_s[lo:lo + tk, 0:DIFF_DV] / acc_s[lo:lo + tk, DIFF_DV:]
            o1 = acc_s[lo + tk:lo + 2 * tk, 0:DIFF_DV] / acc_s[lo + tk:lo + 2 * tk, DIFF_DV:]
            o = o0 - lam * o1
            rows = pl.ds(q_start + r * tk, tk)
            o = _rms_scale(o, nw) * (1.0 - lam_init) * gs_ref[rows, :].astype(F32)
            o_ref[rows, :] = o.astype(BF16)
        return carry

    lax.fori_loop(0, seq // tq, q_tile, 0)


def _diff_attn(lam_vecs, dq, dk, dv, dgs, norm_w, batch, seq, lam_init):
    tq = TQ_ATT
    assert tq % TK_ATT == 0 and TK_ATT % CHUNK == 0 and seq % tq == 0
    T = dq.shape[0]
    bh = lambda b, h: (b, h)
    fixed = lambda b, h: (0, 0)
    blk = pl.BlockSpec((seq, DIFF_DV), bh)
    return pl.pallas_call(
        functools.partial(_diff_attn_kernel, lam_init=lam_init),
        grid=(batch, DIFF_HEADS),
        in_specs=[pl.BlockSpec(lam_vecs.shape, fixed), blk, blk, blk, blk,
                  pl.BlockSpec((1, DIFF_DV), fixed)],
        out_specs=blk,
        out_shape=jax.ShapeDtypeStruct((T, DIFF_WIDTH), BF16),
        scratch_shapes=[
            pltpu.VMEM((2 * tq, DIFF_DV), BF16),
            pltpu.VMEM((2 * tq, LANES), F32),
            pltpu.VMEM((2 * tq, 2 * DIFF_DV), F32),
        ],
        compiler_params=pltpu.CompilerParams(dimension_semantics=("parallel", "parallel"),
                                             vmem_limit_bytes=VMEM_LIMIT),
        name="diff_attn",
    )(lam_vecs, dq, dk, dv, dgs, norm_w)


def _mem_kv_kernel(mem_ref, nw_ref, w_ref, k_ref, v_ref):
    d = mem_ref.shape[1]
    mn = _rms_scale(mem_ref[...], nw_ref[...]).astype(BF16)
    k_ref[...] = jnp.dot(mn, w_ref[:, :d], preferred_element_type=F32).astype(BF16)
    v_ref[...] = jnp.dot(mn, w_ref[:, d:], preferred_element_type=F32).astype(BF16)


def _mem_kv(mem2d, norm_w, w_ckv, batch, mem_len):
    D = mem2d.shape[1]
    row = lambda b: (b, 0)
    fixed = lambda b: (0, 0)
    shp = jax.ShapeDtypeStruct((batch * mem_len, D), BF16)
    return pl.pallas_call(
        _mem_kv_kernel,
        grid=(batch,),
        in_specs=[pl.BlockSpec((mem_len, D), row), pl.BlockSpec((1, D), fixed),
                  pl.BlockSpec(w_ckv.shape, fixed)],
        out_specs=(pl.BlockSpec((mem_len, D), row), pl.BlockSpec((mem_len, D), row)),
        out_shape=(shp, shp),
        compiler_params=pltpu.CompilerParams(dimension_semantics=("parallel",),
                                             vmem_limit_bytes=VMEM_LIMIT),
        name="mem_kv",
    )(mem2d, norm_w, w_ckv)


def _out_cross_kernel(x_ref, og_ref, od_ref, wo_ref, ncw_ref, wq_ref, k_ref, v_ref, wco_ref,
                      fnw_ref, o_ref, ctx_s, *, final_norm):
    d = x_ref.shape[1]
    dh = d // CROSS_HEADS
    pm = x_ref.shape[0] // ROW_PARTS
    parts = [slice(p * pm, (p + 1) * pm) for p in range(ROW_PARTS)]
    x1 = [x_ref[r, :]
          + jnp.dot(og_ref[r, :], wo_ref[0:GLA_WIDTH, :], preferred_element_type=F32)
          + jnp.dot(od_ref[r, :], wo_ref[GLA_WIDTH:, :], preferred_element_type=F32)
          for r in parts]
    q = [(jnp.dot(_rms_scale(x, ncw_ref[...]).astype(BF16), wq_ref[...],
                  preferred_element_type=F32) * (dh ** -0.5)).astype(BF16) for x in x1]
    for hd in range(CROSS_HEADS):
        sl = slice(hd * dh, (hd + 1) * dh)
        for r, qp in zip(parts, q):
            s = lax.dot_general(qp[:, sl], k_ref[:, sl], NT_DIMS, preferred_element_type=F32)
            p = jnp.exp(s - jnp.max(s, axis=-1, keepdims=True))
            p = p / jnp.sum(p, axis=-1, keepdims=True)
            ctx_s[r, sl] = jnp.dot(p.astype(BF16), v_ref[:, sl],
                                   preferred_element_type=F32).astype(BF16)
    for r, x in zip(parts, x1):
        x2 = x + jnp.dot(ctx_s[r, :], wco_ref[...], preferred_element_type=F32)
        if final_norm:
            x2 = _rms_scale(x2, fnw_ref[...])
        o_ref[r, :] = x2


def _out_cross(x2d, og, od, w_out, ncw, w_cq, kc, vc, w_co, fnw, seq, mem_len, final_norm):
    T, D = x2d.shape
    tm = TM_OUT
    per_b = seq // tm
    row = lambda i: (i, 0)
    fixed = lambda i: (0, 0)
    bmap = lambda i: (i // per_b, 0)
    return pl.pallas_call(
        functools.partial(_out_cross_kernel, final_norm=final_norm),
        grid=(T // tm,),
        in_specs=[
            pl.BlockSpec((tm, D), row),
            pl.BlockSpec((tm, GLA_WIDTH), row),
            pl.BlockSpec((tm, DIFF_WIDTH), row),
            pl.BlockSpec(w_out.shape, fixed),
            pl.BlockSpec((1, D), fixed),
            pl.BlockSpec(w_cq.shape, fixed),
            pl.BlockSpec((mem_len, D), bmap),
            pl.BlockSpec((mem_len, D), bmap),
            pl.BlockSpec(w_co.shape, fixed),
            pl.BlockSpec((1, D), fixed),
        ],
        out_specs=pl.BlockSpec((tm, D), row),
        out_shape=jax.ShapeDtypeStruct((T, D), F32),
        scratch_shapes=[pltpu.VMEM((tm, D), BF16)],
        compiler_params=pltpu.CompilerParams(dimension_semantics=("parallel",),
                                             vmem_limit_bytes=VMEM_LIMIT),
        name="out_cross",
    )(x2d, og, od, w_out, ncw, w_cq, kc, vc, w_co, fnw)


def _rope_tables(positions):
    half = ROPE_DIM // 2
    T = positions.size
    per_row = LANES // half
    inv_freq = ROPE_THETA ** (-(jnp.arange(0, ROPE_DIM, 2, dtype=F32) / ROPE_DIM))
    pos = positions.reshape(T // per_row, per_row).astype(F32)
    ang = jnp.repeat(pos, half, axis=1) * jnp.tile(inv_freq, per_row)
    pieces = []
    rest_c, rest_s = jnp.cos(ang), jnp.sin(ang)
    for _ in range(ROPE_SPLIT):
        pc, ps = rest_c.astype(BF16), rest_s.astype(BF16)
        pieces += [pc.reshape(T, half), ps.reshape(T, half)]
        rest_c, rest_s = rest_c - pc.astype(F32), rest_s - ps.astype(F32)
    return jnp.concatenate(pieces, axis=1)


def _rope_select_matrix():
    half = ROPE_DIM // 2
    sel = np.zeros((ROPE_DIM, 2 * LANES), np.float32)
    for j in range(LANES):
        jm = j % DIFF_DQK
        if jm < ROPE_DIM:
            sel[jm % half, j] = 1.0
            sel[half + jm % half, LANES + j] = -1.0 if jm < half else 1.0
    return jnp.asarray(np.tile(sel, (ROPE_SPLIT, 1)), BF16)


def kernel(x, mem, positions, norm_mix_w, w_in, w_alpha2, b_alpha2, gla_norm_w,
           lam_q1, lam_k1, lam_q2, lam_k2, diff_norm_w, w_out,
           norm_cross_w, norm_mem_w, w_cq, w_ckv, w_co, final_norm_w):
    B, S, D = x.shape
    M = mem.shape[1]
    depth = w_in.shape[0]
    T = B * S
    cs_t = _rope_tables(positions)
    sel = _rope_select_matrix()
    x2d = x.reshape(T, D)
    mem2d = mem.reshape(B * M, D)

    c_gv = 2 * GLA_QK
    c_ga = c_gv + 2 * GLA_WIDTH
    c_dq = c_ga + GLA_LOWRANK

    for l in range(depth):
        wl = w_in[l]
        ga_pad = jnp.pad(wl[:, c_ga:c_dq], ((0, 0), (0, LANES - GLA_LOWRANK)))
        wa = jnp.concatenate([wl[:, :c_gv], ga_pad], axis=1).astype(BF16)
        wb = jnp.concatenate([wl[:, c_gv:c_ga], wl[:, c_dq:]], axis=1).astype(BF16)
        wa2 = jnp.pad(w_alpha2[l], ((0, LANES - GLA_LOWRANK), (0, 0))).astype(BF16)
        ba2 = b_alpha2[l].reshape(1, GLA_QK).astype(F32)

        gq, gk, gv, ggs, g, dq, dk, dv, dgs = _in_proj(
            x2d, norm_mix_w[l].reshape(1, D), wa, wa2, ba2, wb, cs_t, sel)

        o_gla = _gla(gq, gk, g, gv, ggs, gla_norm_w[l].reshape(1, GLA_DV), B, S)

        lam_init = 0.8 - 0.6 * math.exp(-0.3 * l)
        lam_vecs = jnp.stack([lam_q1[l], lam_k1[l], lam_q2[l], lam_k2[l]]).astype(F32)
        o_diff = _diff_attn(lam_vecs, dq, dk, dv, dgs, diff_norm_w[l].reshape(1, DIFF_DV),
                            B, S, lam_init)

        kc, vc = _mem_kv(mem2d, norm_mem_w[l].reshape(1, D), w_ckv[l].astype(BF16), B, M)

        x2d = _out_cross(x2d, o_gla, o_diff, w_out[l].astype(BF16),
                         norm_cross_w[l].reshape(1, D), w_cq[l].astype(BF16), kc, vc,
                         w_co[l].astype(BF16), final_norm_w.reshape(1, D), S, M,
                         final_norm=(l == depth - 1))
    return x2d.reshape(B, S, D)
```

```python
import functools
import math

import jax
import jax.numpy as jnp
import numpy as np
from jax import lax
from jax.experimental import pallas as pl
from jax.experimental.pallas import tpu as pltpu

F32 = jnp.float32
BF16 = jnp.bfloat16

CHUNK = 64
GLA_HEADS = 4
GLA_DK = 64
GLA_DV = 128
GLA_QK = GLA_HEADS * GLA_DK
GLA_WIDTH = GLA_HEADS * GLA_DV
GLA_LOWRANK = 16
GLA_TAU = 16.0
DIFF_HEADS = 4
DIFF_DQK = 64
DIFF_DV = 128
DIFF_WIDTH = DIFF_HEADS * DIFF_DV
ROPE_DIM = 16
ROPE_THETA = 500000.0
CROSS_HEADS = 4
EPS = 1e-6
LOG2E = math.log2(math.e)
ROPE_SPLIT = 3

LANES = 128

TM_PROJ = 1024
TM_GLA = 512
TK_ATT = 256
SUM_ROWS = 16
SCORES_AHEAD = 2
TM_OUT = 1024
ROW_PARTS = 2
VMEM_LIMIT = 56 * 1024 * 1024

NT_DIMS = (((1,), (1,)), ((), ()))
TN_DIMS = (((0,), (0,)), ((), ()))


def _rms_scale(xf, w):
    return xf * lax.rsqrt(jnp.mean(xf * xf, axis=-1, keepdims=True) + EPS) * w


def _silu(t):
    return t * (1.0 / (1.0 + jnp.exp(-t)))


def _in_proj_kernel(x_ref, nw_ref, wa_ref, wa2_ref, ba2_ref, wb_ref, wvt_ref, cs_ref, sel_ref,
                    gq_ref, gk_ref, gv_ref, ggs_ref, g_ref, dq_ref, dk_ref, dvt_ref, dgs_ref):
    pm = x_ref.shape[0] // ROW_PARTS
    for part in range(ROW_PARTS):
        _in_proj_rows(slice(part * pm, (part + 1) * pm),
                      x_ref, nw_ref, wa_ref, wa2_ref, ba2_ref, wb_ref, wvt_ref, cs_ref, sel_ref,
                      gq_ref, gk_ref, gv_ref, ggs_ref, g_ref, dq_ref, dk_ref, dvt_ref, dgs_ref)


def _in_proj_rows(rows, x_ref, nw_ref, wa_ref, wa2_ref, ba2_ref, wb_ref, wvt_ref, cs_ref, sel_ref,
                  gq_ref, gk_ref, gv_ref, ggs_ref, g_ref, dq_ref, dk_ref, dvt_ref, dgs_ref):
    h = _rms_scale(x_ref[rows, :], nw_ref[...]).astype(BF16)

    def proj_b(group):
        return jnp.dot(h, wb_ref[:, group * GLA_WIDTH:(group + 1) * GLA_WIDTH],
                       preferred_element_type=F32)

    qkg = jnp.dot(h, wa_ref[...], preferred_element_type=F32)
    gq_ref[rows, :] = qkg[:, 0:GLA_QK] * (GLA_DK ** -0.5)
    gk_ref[rows, :] = qkg[:, GLA_QK:2 * GLA_QK]
    ga = qkg[:, 2 * GLA_QK:].astype(BF16)

    gv_ref[rows, :] = proj_b(0).astype(BF16)
    ggs_ref[rows, :] = _silu(proj_b(1)).astype(BF16)

    tabs = jnp.dot(cs_ref[rows, :], sel_ref[...], preferred_element_type=F32)
    lane = lax.broadcasted_iota(jnp.int32, (tabs.shape[0], LANES), 1)
    cos = tabs[:, :LANES] + jnp.where((lane % DIFF_DQK) < ROPE_DIM, 0.0, 1.0)
    sin = tabs[:, LANES:]
    first_half = (lane % DIFF_DQK) < (ROPE_DIM // 2)

    def rope(t):
        partner = jnp.where(first_half, pltpu.roll(t, LANES - ROPE_DIM // 2, 1),
                            pltpu.roll(t, ROPE_DIM // 2, 1))
        return t * cos + partner * sin

    q = proj_b(2)
    k = proj_b(3)
    for hd in range(DIFF_HEADS):
        sl = slice(hd * DIFF_DV, (hd + 1) * DIFF_DV)
        dq_ref[rows, sl] = (rope(q[:, sl]) * (DIFF_DQK ** -0.5 * LOG2E)).astype(BF16)
        dk_ref[rows, sl] = rope(k[:, sl]).astype(BF16)
    dvt_ref[:, rows] = lax.dot_general(wvt_ref[...], h, NT_DIMS,
                                       preferred_element_type=F32).astype(BF16)

    logit = jnp.dot(ga, wa2_ref[...], preferred_element_type=F32) + ba2_ref[...]
    log_sig = jnp.minimum(logit, 0.0) - jnp.log1p(jnp.exp(-jnp.abs(logit)))
    g_ref[rows, :] = log_sig * (1.0 / GLA_TAU)

    dgs_ref[rows, :] = _silu(proj_b(4)).astype(BF16)


def _in_proj(x2d, norm_w, wa, wa2, ba2, wb, wvt, cs_t, sel):
    T, D = x2d.shape
    tm = TM_PROJ
    row = lambda i: (i, 0)
    col = lambda i: (0, i)
    fixed = lambda i: (0, 0)
    out_shapes = (
        jax.ShapeDtypeStruct((T, GLA_QK), F32),
        jax.ShapeDtypeStruct((T, GLA_QK), F32),
        jax.ShapeDtypeStruct((T, GLA_WIDTH), BF16),
        jax.ShapeDtypeStruct((T, GLA_WIDTH), BF16),
        jax.ShapeDtypeStruct((T, GLA_QK), F32),
        jax.ShapeDtypeStruct((T, DIFF_WIDTH), BF16),
        jax.ShapeDtypeStruct((T, DIFF_WIDTH), BF16),
        jax.ShapeDtypeStruct((DIFF_WIDTH, T), BF16),
        jax.ShapeDtypeStruct((T, DIFF_WIDTH), BF16),
    )
    out_specs = [pl.BlockSpec((tm, s.shape[1]), row) for s in out_shapes]
    out_specs[7] = pl.BlockSpec((DIFF_WIDTH, tm), col)
    return pl.pallas_call(
        _in_proj_kernel,
        grid=(T // tm,),
        in_specs=[
            pl.BlockSpec((tm, D), row),
            pl.BlockSpec((1, D), fixed),
            pl.BlockSpec(wa.shape, fixed),
            pl.BlockSpec(wa2.shape, fixed),
            pl.BlockSpec(ba2.shape, fixed),
            pl.BlockSpec(wb.shape, fixed),
            pl.BlockSpec(wvt.shape, fixed),
            pl.BlockSpec((tm, cs_t.shape[1]), row),
            pl.BlockSpec(sel.shape, fixed),
        ],
        out_specs=tuple(out_specs),
        out_shape=out_shapes,
        compiler_params=pltpu.CompilerParams(dimension_semantics=("parallel",),
                                             vmem_limit_bytes=VMEM_LIMIT),
        name="in_proj",
    )(x2d, norm_w, wa, wa2, ba2, wb, wvt, cs_t, sel)


def _gla_kernel(gq_ref, gk_ref, g_ref, gv_ref, ggs_ref, nw_ref, o_ref,
                qd4_s, ki_s, kl4_s, dec_s, state_s):
    tm = gq_ref.shape[0]
    n_chunks = tm // CHUNK

    @pl.when(pl.program_id(1) == 0)
    def _():
        state_s[...] = jnp.zeros_like(state_s)

    g = g_ref[...]
    row_in_chunk = lax.broadcasted_iota(jnp.int32, g.shape, 0) % CHUNK
    G = g
    shift = 1
    while shift < CHUNK:
        G = G + jnp.where(row_in_chunk >= shift, pltpu.roll(G, shift, 0), 0.0)
        shift *= 2
    q = gq_ref[...]
    k = gk_ref[...]
    qd = (q * jnp.exp(G)).astype(BF16)
    ki_s[...] = (k * jnp.exp(-G)).astype(BF16)

    lane_head = lax.broadcasted_iota(jnp.int32, (CHUNK, GLA_QK), 1) // GLA_DK
    zero = jnp.zeros((CHUNK, GLA_QK), BF16)
    for c in range(n_chunks):
        rows = slice(c * CHUNK, (c + 1) * CHUNK)
        g_last = G[(c + 1) * CHUNK - 1:(c + 1) * CHUNK, :]
        kl = (k[rows, :] * jnp.exp(g_last - G[rows, :])).astype(BF16)
        dec_s[c:c + 1, :] = jnp.exp(g_last)
        for hd in range(GLA_HEADS):
            dst = slice((c * GLA_HEADS + hd) * CHUNK, (c * GLA_HEADS + hd + 1) * CHUNK)
            qd4_s[dst, :] = jnp.where(lane_head == hd, qd[rows, :], zero)
            kl4_s[dst, :] = jnp.where(lane_head == hd, kl, zero)

    r_i = lax.broadcasted_iota(jnp.int32, (GLA_HEADS * CHUNK, GLA_HEADS * CHUNK), 0)
    c_i = lax.broadcasted_iota(jnp.int32, (GLA_HEADS * CHUNK, GLA_HEADS * CHUNK), 1)
    keep = (r_i // CHUNK == c_i // CHUNK) & (c_i % CHUNK <= r_i % CHUNK)
    nw = nw_ref[...]

    for c in range(n_chunks):
        rows = slice(c * CHUNK, (c + 1) * CHUNK)
        rows4 = slice(c * GLA_HEADS * CHUNK, (c + 1) * GLA_HEADS * CHUNK)
        q4 = qd4_s[rows4, :]
        ki = ki_s[rows, :]
        k4 = jnp.concatenate([ki] * GLA_HEADS, axis=0)
        v4 = jnp.concatenate([gv_ref[rows, hd * GLA_DV:(hd + 1) * GLA_DV]
                              for hd in range(GLA_HEADS)], axis=0)
        st = state_s[...]
        scores = lax.dot_general(q4, k4, NT_DIMS, preferred_element_type=F32)
        scores = jnp.where(keep, scores, 0.0).astype(BF16)
        o = (jnp.dot(scores, v4, preferred_element_type=F32)
             + lax.dot_general(q4, st.astype(BF16), NT_DIMS, preferred_element_type=F32))
        kv_t = lax.dot_general(v4, kl4_s[rows4, :], TN_DIMS, preferred_element_type=F32)
        state_s[...] = dec_s[c:c + 1, :] * st + kv_t
        gate = jnp.concatenate([ggs_ref[rows, hd * GLA_DV:(hd + 1) * GLA_DV]
                                for hd in range(GLA_HEADS)], axis=0).astype(F32)
        o = (_rms_scale(o, nw) * gate).astype(BF16)
        for hd in range(GLA_HEADS):
            o_ref[rows, hd * GLA_DV:(hd + 1) * GLA_DV] = o[hd * CHUNK:(hd + 1) * CHUNK, :]


def _gla(gq, gk, g, gv, ggs, norm_w, batch, seq):
    tm = TM_GLA
    nt = seq // tm
    row = lambda b, i: (b * nt + i, 0)
    fixed = lambda b, i: (0, 0)
    T = gq.shape[0]
    return pl.pallas_call(
        _gla_kernel,
        grid=(batch, nt),
        in_specs=[
            pl.BlockSpec((tm, GLA_QK), row),
            pl.BlockSpec((tm, GLA_QK), row),
            pl.BlockSpec((tm, GLA_QK), row),
            pl.BlockSpec((tm, GLA_WIDTH), row),
            pl.BlockSpec((tm, GLA_WIDTH), row),
            pl.BlockSpec((1, GLA_DV), fixed),
        ],
        out_specs=pl.BlockSpec((tm, GLA_WIDTH), row),
        out_shape=jax.ShapeDtypeStruct((T, GLA_WIDTH), BF16),
        scratch_shapes=[
            pltpu.VMEM((GLA_HEADS * tm, GLA_QK), BF16),
            pltpu.VMEM((tm, GLA_QK), BF16),
            pltpu.VMEM((GLA_HEADS * tm, GLA_QK), BF16),
            pltpu.VMEM((tm // CHUNK, GLA_QK), F32),
            pltpu.VMEM((GLA_DV, GLA_QK), F32),
        ],
        compiler_params=pltpu.CompilerParams(dimension_semantics=("parallel", "arbitrary"),
                                             vmem_limit_bytes=VMEM_LIMIT),
        name="gla",
    )(gq, gk, g, gv, ggs, norm_w)


def _diff_attn_kernel(lam_ref, q_ref, k_ref, vt_ref, gs_ref, nw_ref, o_ref,
                      qq_s, m_s, acc_s, *, lam_init):
    seq = q_ref.shape[0]
    tk = TK_ATT
    n_blk = seq // tk
    cw = 2 * tk

    lv = lam_ref[...]
    lam = (jnp.exp(jnp.sum(lv[0:1] * lv[1:2], axis=-1, keepdims=True))
           - jnp.exp(jnp.sum(lv[2:3] * lv[3:4], axis=-1, keepdims=True)) + lam_init)
    nw = nw_ref[...]
    ones = jnp.ones((SUM_ROWS, tk), BF16)

    kc = lax.broadcasted_iota(jnp.int32, (tk, cw), 0) // CHUNK
    qc = (lax.broadcasted_iota(jnp.int32, (tk, cw), 1) % tk) // CHUNK
    stair_bias = jnp.where(kc <= qc, 0.0, -jnp.inf).astype(F32)

    lane = lax.broadcasted_iota(jnp.int32, (tk, LANES), 1)
    for r in range(n_blk):
        q = q_ref[r * tk:(r + 1) * tk, :]
        zero = jnp.zeros_like(q)
        qq_s[r * cw:r * cw + tk, :] = jnp.where(lane < DIFF_DQK, q, zero)
        qq_s[r * cw + tk:(r + 1) * cw, :] = jnp.where(lane >= DIFF_DQK, q, zero)

    def scores(kb, cb):
        k = k_ref[kb * tk:(kb + 1) * tk, :]
        st = lax.dot_general(k, qq_s[cb * cw:(cb + 1) * cw, :], NT_DIMS,
                             preferred_element_type=F32)
        return st + stair_bias if cb == kb else st

    def update(kb, cb, st):
        cs = slice(cb * cw, (cb + 1) * cw)
        m_cur = jnp.max(st, axis=0, keepdims=True)
        if kb == 0:
            m_new = m_cur
        else:
            m_old = m_s[0:1, cs]
            m_new = jnp.maximum(m_old, m_cur)
        pt = jnp.exp2(st - m_new).astype(BF16)
        v1t = jnp.concatenate([vt_ref[:, kb * tk:(kb + 1) * tk], ones], axis=0)
        pv = jnp.dot(v1t, pt, preferred_element_type=F32)
        if kb == 0:
            acc_s[:, cs] = pv
        else:
            acc_s[:, cs] = jnp.exp2(m_old - m_new) * acc_s[:, cs] + pv
        m_s[:, cs] = jnp.broadcast_to(m_new, (m_s.shape[0], cw))

    def finish(r):
        c0 = r * cw
        o0t = acc_s[0:DIFF_DV, c0:c0 + tk] / acc_s[DIFF_DV:DIFF_DV + 1, c0:c0 + tk]
        o1t = acc_s[0:DIFF_DV, c0 + tk:c0 + cw] / acc_s[DIFF_DV:DIFF_DV + 1, c0 + tk:c0 + cw]
        o = (o0t - lam * o1t).T
        rows = slice(r * tk, (r + 1) * tk)
        o = _rms_scale(o, nw) * (1.0 - lam_init) * gs_ref[rows, :].astype(F32)
        o_ref[rows, :] = o.astype(BF16)

    items = [(kb, cb) for kb in range(n_blk) for cb in range(kb, n_blk)]
    pending = [scores(*it) for it in items[:SCORES_AHEAD]]
    for n, (kb, cb) in enumerate(items):
        if n + SCORES_AHEAD < len(items):
            pending.append(scores(*items[n + SCORES_AHEAD]))
        update(kb, cb, pending.pop(0))
        if cb == kb:
            finish(kb)


def _diff_attn(lam_vecs, dq, dk, dvt, dgs, norm_w, batch, seq, lam_init):
    assert TK_ATT % CHUNK == 0 and seq % TK_ATT == 0
    T = dq.shape[0]
    bh = lambda b, h: (b, h)
    hb = lambda b, h: (h, b)
    fixed = lambda b, h: (0, 0)
    blk = pl.BlockSpec((seq, DIFF_DV), bh)
    return pl.pallas_call(
        functools.partial(_diff_attn_kernel, lam_init=lam_init),
        grid=(batch, DIFF_HEADS),
        in_specs=[pl.BlockSpec(lam_vecs.shape, fixed), blk, blk,
                  pl.BlockSpec((DIFF_DV, seq), hb), blk,
                  pl.BlockSpec((1, DIFF_DV), fixed)],
        out_specs=blk,
        out_shape=jax.ShapeDtypeStruct((T, DIFF_WIDTH), BF16),
        scratch_shapes=[
            pltpu.VMEM((2 * seq, DIFF_DV), BF16),
            pltpu.VMEM((8, 2 * seq), F32),
            pltpu.VMEM((DIFF_DV + SUM_ROWS, 2 * seq), F32),
        ],
        compiler_params=pltpu.CompilerParams(dimension_semantics=("parallel", "parallel"),
                                             vmem_limit_bytes=VMEM_LIMIT),
        name="diff_attn",
    )(lam_vecs, dq, dk, dvt, dgs, norm_w)


def _mem_kv_kernel(mem_ref, nw_ref, w_ref, k_ref, v_ref):
    d = mem_ref.shape[1]
    mn = _rms_scale(mem_ref[...], nw_ref[...]).astype(BF16)
    k_ref[...] = jnp.dot(mn, w_ref[:, :d], preferred_element_type=F32).astype(BF16)
    v_ref[...] = jnp.dot(mn, w_ref[:, d:], preferred_element_type=F32).astype(BF16)


def _mem_kv(mem2d, norm_w, w_ckv, batch, mem_len):
    D = mem2d.shape[1]
    row = lambda b: (b, 0)
    fixed = lambda b: (0, 0)
    shp = jax.ShapeDtypeStruct((batch * mem_len, D), BF16)
    return pl.pallas_call(
        _mem_kv_kernel,
        grid=(batch,),
        in_specs=[pl.BlockSpec((mem_len, D), row), pl.BlockSpec((1, D), fixed),
                  pl.BlockSpec(w_ckv.shape, fixed)],
        out_specs=(pl.BlockSpec((mem_len, D), row), pl.BlockSpec((mem_len, D), row)),
        out_shape=(shp, shp),
        compiler_params=pltpu.CompilerParams(dimension_semantics=("parallel",),
                                             vmem_limit_bytes=VMEM_LIMIT),
        name="mem_kv",
    )(mem2d, norm_w, w_ckv)


def _out_cross_kernel(x_ref, og_ref, od_ref, wo_ref, ncw_ref, wq_ref, k_ref, v_ref, wco_ref,
                      fnw_ref, o_ref, ctx_s, *, final_norm):
    d = x_ref.shape[1]
    dh = d // CROSS_HEADS
    pm = x_ref.shape[0] // ROW_PARTS
    parts = [slice(p * pm, (p + 1) * pm) for p in range(ROW_PARTS)]
    x1 = [x_ref[r, :]
          + jnp.dot(og_ref[r, :], wo_ref[0:GLA_WIDTH, :], preferred_element_type=F32)
          + jnp.dot(od_ref[r, :], wo_ref[GLA_WIDTH:, :], preferred_element_type=F32)
          for r in parts]
    q = [(jnp.dot(_rms_scale(x, ncw_ref[...]).astype(BF16), wq_ref[...],
                  preferred_element_type=F32) * (dh ** -0.5)).astype(BF16) for x in x1]
    for hd in range(CROSS_HEADS):
        sl = slice(hd * dh, (hd + 1) * dh)
        for r, qp in zip(parts, q):
            s = lax.dot_general(qp[:, sl], k_ref[:, sl], NT_DIMS, preferred_element_type=F32)
            p = jnp.exp(s - jnp.max(s, axis=-1, keepdims=True))
            p = p / jnp.sum(p, axis=-1, keepdims=True)
            ctx_s[r, sl] = jnp.dot(p.astype(BF16), v_ref[:, sl],
                                   preferred_element_type=F32).astype(BF16)
    for r, x in zip(parts, x1):
        x2 = x + jnp.dot(ctx_s[r, :], wco_ref[...], preferred_element_type=F32)
        if final_norm:
            x2 = _rms_scale(x2, fnw_ref[...])
        o_ref[r, :] = x2


def _out_cross(x2d, og, od, w_out, ncw, w_cq, kc, vc, w_co, fnw, seq, mem_len, final_norm):
    T, D = x2d.shape
    tm = TM_OUT
    per_b = seq // tm
    row = lambda i: (i, 0)
    fixed = lambda i: (0, 0)
    bmap = lambda i: (i // per_b, 0)
    return pl.pallas_call(
        functools.partial(_out_cross_kernel, final_norm=final_norm),
        grid=(T // tm,),
        in_specs=[
            pl.BlockSpec((tm, D), row),
            pl.BlockSpec((tm, GLA_WIDTH), row),
            pl.BlockSpec((tm, DIFF_WIDTH), row),
            pl.BlockSpec(w_out.shape, fixed),
            pl.BlockSpec((1, D), fixed),
            pl.BlockSpec(w_cq.shape, fixed),
            pl.BlockSpec((mem_len, D), bmap),
            pl.BlockSpec((mem_len, D), bmap),
            pl.BlockSpec(w_co.shape, fixed),
            pl.BlockSpec((1, D), fixed),
        ],
        out_specs=pl.BlockSpec((tm, D), row),
        out_shape=jax.ShapeDtypeStruct((T, D), F32),
        scratch_shapes=[pltpu.VMEM((tm, D), BF16)],
        compiler_params=pltpu.CompilerParams(dimension_semantics=("parallel",),
                                             vmem_limit_bytes=VMEM_LIMIT),
        name="out_cross",
    )(x2d, og, od, w_out, ncw, w_cq, kc, vc, w_co, fnw)


def _rope_tables(positions):
    half = ROPE_DIM // 2
    T = positions.size
    per_row = LANES // half
    inv_freq = ROPE_THETA ** (-(jnp.arange(0, ROPE_DIM, 2, dtype=F32) / ROPE_DIM))
    pos = positions.reshape(T // per_row, per_row).astype(F32)
    ang = jnp.repeat(pos, half, axis=1) * jnp.tile(inv_freq, per_row)
    pieces = []
    rest_c, rest_s = jnp.cos(ang), jnp.sin(ang)
    for _ in range(ROPE_SPLIT):
        pc, ps = rest_c.astype(BF16), rest_s.astype(BF16)
        pieces += [pc.reshape(T, half), ps.reshape(T, half)]
        rest_c, rest_s = rest_c - pc.astype(F32), rest_s - ps.astype(F32)
    return jnp.concatenate(pieces, axis=1)


def _rope_select_matrix():
    half = ROPE_DIM // 2
    sel = np.zeros((ROPE_DIM, 2 * LANES), np.float32)
    for j in range(LANES):
        jm = j % DIFF_DQK
        if jm < ROPE_DIM:
            sel[jm % half, j] = 1.0
            sel[half + jm % half, LANES + j] = -1.0 if jm < half else 1.0
    return jnp.asarray(np.tile(sel, (ROPE_SPLIT, 1)), BF16)


def kernel(x, mem, positions, norm_mix_w, w_in, w_alpha2, b_alpha2, gla_norm_w,
           lam_q1, lam_k1, lam_q2, lam_k2, diff_norm_w, w_out,
           norm_cross_w, norm_mem_w, w_cq, w_ckv, w_co, final_norm_w):
    B, S, D = x.shape
    M = mem.shape[1]
    depth = w_in.shape[0]
    T = B * S
    cs_t = _rope_tables(positions)
    sel = _rope_select_matrix()
    x2d = x.reshape(T, D)
    mem2d = mem.reshape(B * M, D)

    c_gv = 2 * GLA_QK
    c_ga = c_gv + 2 * GLA_WIDTH
    c_dq = c_ga + GLA_LOWRANK

    for l in range(depth):
        wl = w_in[l]
        ga_pad = jnp.pad(wl[:, c_ga:c_dq], ((0, 0), (0, LANES - GLA_LOWRANK)))
        wa = jnp.concatenate([wl[:, :c_gv], ga_pad], axis=1).astype(BF16)
        c_dv = c_dq + 2 * DIFF_WIDTH
        c_dg = c_dv + DIFF_WIDTH
        wb = jnp.concatenate([wl[:, c_gv:c_ga], wl[:, c_dq:c_dv], wl[:, c_dg:]],
                             axis=1).astype(BF16)
        wvt = wl[:, c_dv:c_dg].T.astype(BF16)
        wa2 = jnp.pad(w_alpha2[l], ((0, LANES - GLA_LOWRANK), (0, 0))).astype(BF16)
        ba2 = b_alpha2[l].reshape(1, GLA_QK).astype(F32)

        gq, gk, gv, ggs, g, dq, dk, dvt, dgs = _in_proj(
            x2d, norm_mix_w[l].reshape(1, D), wa, wa2, ba2, wb, wvt, cs_t, sel)

        o_gla = _gla(gq, gk, g, gv, ggs, gla_norm_w[l].reshape(1, GLA_DV), B, S)

        lam_init = 0.8 - 0.6 * math.exp(-0.3 * l)
        lam_vecs = jnp.stack([lam_q1[l], lam_k1[l], lam_q2[l], lam_k2[l]]).astype(F32)
        o_diff = _diff_attn(lam_vecs, dq, dk, dvt, dgs, diff_norm_w[l].reshape(1, DIFF_DV),
                            B, S, lam_init)

        kc, vc = _mem_kv(mem2d, norm_mem_w[l].reshape(1, D), w_ckv[l].astype(BF16), B, M)

        x2d = _out_cross(x2d, o_gla, o_diff, w_out[l].astype(BF16),
                         norm_cross_w[l].reshape(1, D), w_cq[l].astype(BF16), kc, vc,
                         w_co[l].astype(BF16), final_norm_w.reshape(1, D), S, M,
                         final_norm=(l == depth - 1))
    return x2d.reshape(B, S, D)
```

```python
import functools
import math

import jax
import jax.numpy as jnp
import numpy as np
from jax import lax
from jax.experimental import pallas as pl
from jax.experimental.pallas import tpu as pltpu

F32 = jnp.float32
BF16 = jnp.bfloat16

CHUNK = 64
GLA_HEADS = 4
GLA_DK = 64
GLA_DV = 128
GLA_QK = GLA_HEADS * GLA_DK
GLA_WIDTH = GLA_HEADS * GLA_DV
GLA_LOWRANK = 16
GLA_TAU = 16.0
DIFF_HEADS = 4
DIFF_DQK = 64
DIFF_DV = 128
DIFF_WIDTH = DIFF_HEADS * DIFF_DV
ROPE_DIM = 16
ROPE_THETA = 500000.0
CROSS_HEADS = 4
EPS = 1e-6
LOG2E = math.log2(math.e)
ROPE_SPLIT = 3

LANES = 128

TM_PROJ = 1024
TM_GLA = 512
TK_ATT = 256
SUM_ROWS = 16
VT_ROWS = DIFF_DV + SUM_ROWS
SCORES_AHEAD = 2
TM_OUT = 1024
ROW_PARTS = 2
VMEM_LIMIT = 56 * 1024 * 1024

NT_DIMS = (((1,), (1,)), ((), ()))
TN_DIMS = (((0,), (0,)), ((), ()))


def _rms_scale(xf, w):
    return xf * lax.rsqrt(jnp.mean(xf * xf, axis=-1, keepdims=True) + EPS) * w


def _silu(t):
    return t * (1.0 / (1.0 + jnp.exp(-t)))


def _in_proj_kernel(x_ref, nw_ref, wa_ref, wa2_ref, ba2_ref, wb_ref, wvt_ref, cs_ref, sel_ref,
                    gq_ref, gk_ref, gv_ref, ggs_ref, g_ref, dq_ref, dk_ref, dvt_ref, dgs_ref):
    pm = x_ref.shape[0] // ROW_PARTS
    for part in range(ROW_PARTS):
        _in_proj_rows(slice(part * pm, (part + 1) * pm),
                      x_ref, nw_ref, wa_ref, wa2_ref, ba2_ref, wb_ref, wvt_ref, cs_ref, sel_ref,
                      gq_ref, gk_ref, gv_ref, ggs_ref, g_ref, dq_ref, dk_ref, dvt_ref, dgs_ref)


def _in_proj_rows(rows, x_ref, nw_ref, wa_ref, wa2_ref, ba2_ref, wb_ref, wvt_ref, cs_ref, sel_ref,
                  gq_ref, gk_ref, gv_ref, ggs_ref, g_ref, dq_ref, dk_ref, dvt_ref, dgs_ref):
    h = _rms_scale(x_ref[rows, :], nw_ref[...]).astype(BF16)

    def proj_b(group):
        return jnp.dot(h, wb_ref[:, group * GLA_WIDTH:(group + 1) * GLA_WIDTH],
                       preferred_element_type=F32)

    qkg = jnp.dot(h, wa_ref[...], preferred_element_type=F32)
    gq_ref[rows, :] = qkg[:, 0:GLA_QK] * (GLA_DK ** -0.5)
    gk_ref[rows, :] = qkg[:, GLA_QK:2 * GLA_QK]
    ga = qkg[:, 2 * GLA_QK:].astype(BF16)

    gv_ref[rows, :] = proj_b(0).astype(BF16)
    ggs_ref[rows, :] = _silu(proj_b(1)).astype(BF16)

    tabs = jnp.dot(cs_ref[rows, :], sel_ref[...], preferred_element_type=F32)
    lane = lax.broadcasted_iota(jnp.int32, (tabs.shape[0], LANES), 1)
    cos = tabs[:, :LANES] + jnp.where((lane % DIFF_DQK) < ROPE_DIM, 0.0, 1.0)
    sin = tabs[:, LANES:]
    first_half = (lane % DIFF_DQK) < (ROPE_DIM // 2)

    def rope(t):
        partner = jnp.where(first_half, pltpu.roll(t, LANES - ROPE_DIM // 2, 1),
                            pltpu.roll(t, ROPE_DIM // 2, 1))
        return t * cos + partner * sin

    q = proj_b(2)
    k = proj_b(3)
    for hd in range(DIFF_HEADS):
        sl = slice(hd * DIFF_DV, (hd + 1) * DIFF_DV)
        dq_ref[rows, sl] = (rope(q[:, sl]) * (DIFF_DQK ** -0.5 * LOG2E)).astype(BF16)
        dk_ref[rows, sl] = rope(k[:, sl]).astype(BF16)
    dvt = lax.dot_general(wvt_ref[...], h, NT_DIMS, preferred_element_type=F32).astype(BF16)
    for hd in range(DIFF_HEADS):
        base = hd * VT_ROWS
        dvt_ref[base:base + DIFF_DV, rows] = dvt[hd * DIFF_DV:(hd + 1) * DIFF_DV, :]
        dvt_ref[base + DIFF_DV:base + VT_ROWS, rows] = jnp.ones((SUM_ROWS, dvt.shape[1]), BF16)

    logit = jnp.dot(ga, wa2_ref[...], preferred_element_type=F32) + ba2_ref[...]
    log_sig = jnp.minimum(logit, 0.0) - jnp.log1p(jnp.exp(-jnp.abs(logit)))
    g_ref[rows, :] = log_sig * (1.0 / GLA_TAU)

    dgs_ref[rows, :] = _silu(proj_b(4)).astype(BF16)


def _in_proj(x2d, norm_w, wa, wa2, ba2, wb, wvt, cs_t, sel):
    T, D = x2d.shape
    tm = TM_PROJ
    row = lambda i: (i, 0)
    col = lambda i: (0, i)
    fixed = lambda i: (0, 0)
    out_shapes = (
        jax.ShapeDtypeStruct((T, GLA_QK), F32),
        jax.ShapeDtypeStruct((T, GLA_QK), F32),
        jax.ShapeDtypeStruct((T, GLA_WIDTH), BF16),
        jax.ShapeDtypeStruct((T, GLA_WIDTH), BF16),
        jax.ShapeDtypeStruct((T, GLA_QK), F32),
        jax.ShapeDtypeStruct((T, DIFF_WIDTH), BF16),
        jax.ShapeDtypeStruct((T, DIFF_WIDTH), BF16),
        jax.ShapeDtypeStruct((DIFF_HEADS * VT_ROWS, T), BF16),
        jax.ShapeDtypeStruct((T, DIFF_WIDTH), BF16),
    )
    out_specs = [pl.BlockSpec((tm, s.shape[1]), row) for s in out_shapes]
    out_specs[7] = pl.BlockSpec((DIFF_HEADS * VT_ROWS, tm), col)
    return pl.pallas_call(
        _in_proj_kernel,
        grid=(T // tm,),
        in_specs=[
            pl.BlockSpec((tm, D), row),
            pl.BlockSpec((1, D), fixed),
            pl.BlockSpec(wa.shape, fixed),
            pl.BlockSpec(wa2.shape, fixed),
            pl.BlockSpec(ba2.shape, fixed),
            pl.BlockSpec(wb.shape, fixed),
            pl.BlockSpec(wvt.shape, fixed),
            pl.BlockSpec((tm, cs_t.shape[1]), row),
            pl.BlockSpec(sel.shape, fixed),
        ],
        out_specs=tuple(out_specs),
        out_shape=out_shapes,
        compiler_params=pltpu.CompilerParams(dimension_semantics=("parallel",),
                                             vmem_limit_bytes=VMEM_LIMIT),
        name="in_proj",
    )(x2d, norm_w, wa, wa2, ba2, wb, wvt, cs_t, sel)


def _gla_kernel(gq_ref, gk_ref, g_ref, gv_ref, ggs_ref, nw_ref, o_ref,
                qd4_s, ki_s, kl4_s, dec_s, state_s):
    tm = gq_ref.shape[0]
    n_chunks = tm // CHUNK

    @pl.when(pl.program_id(1) == 0)
    def _():
        state_s[...] = jnp.zeros_like(state_s)

    g = g_ref[...]
    row_in_chunk = lax.broadcasted_iota(jnp.int32, g.shape, 0) % CHUNK
    G = g
    shift = 1
    while shift < CHUNK:
        G = G + jnp.where(row_in_chunk >= shift, pltpu.roll(G, shift, 0), 0.0)
        shift *= 2
    q = gq_ref[...]
    k = gk_ref[...]
    qd = (q * jnp.exp(G)).astype(BF16)
    ki_s[...] = (k * jnp.exp(-G)).astype(BF16)

    lane_head = lax.broadcasted_iota(jnp.int32, (CHUNK, GLA_QK), 1) // GLA_DK
    zero = jnp.zeros((CHUNK, GLA_QK), BF16)
    for c in range(n_chunks):
        rows = slice(c * CHUNK, (c + 1) * CHUNK)
        g_last = G[(c + 1) * CHUNK - 1:(c + 1) * CHUNK, :]
        kl = (k[rows, :] * jnp.exp(g_last - G[rows, :])).astype(BF16)
        dec_s[c:c + 1, :] = jnp.exp(g_last)
        for hd in range(GLA_HEADS):
            dst = slice((c * GLA_HEADS + hd) * CHUNK, (c * GLA_HEADS + hd + 1) * CHUNK)
            qd4_s[dst, :] = jnp.where(lane_head == hd, qd[rows, :], zero)
            kl4_s[dst, :] = jnp.where(lane_head == hd, kl, zero)

    r_i = lax.broadcasted_iota(jnp.int32, (GLA_HEADS * CHUNK, GLA_HEADS * CHUNK), 0)
    c_i = lax.broadcasted_iota(jnp.int32, (GLA_HEADS * CHUNK, GLA_HEADS * CHUNK), 1)
    keep = (r_i // CHUNK == c_i // CHUNK) & (c_i % CHUNK <= r_i % CHUNK)
    nw = nw_ref[...]

    for c in range(n_chunks):
        rows = slice(c * CHUNK, (c + 1) * CHUNK)
        rows4 = slice(c * GLA_HEADS * CHUNK, (c + 1) * GLA_HEADS * CHUNK)
        q4 = qd4_s[rows4, :]
        ki = ki_s[rows, :]
        k4 = jnp.concatenate([ki] * GLA_HEADS, axis=0)
        v4 = jnp.concatenate([gv_ref[rows, hd * GLA_DV:(hd + 1) * GLA_DV]
                              for hd in range(GLA_HEADS)], axis=0)
        st = state_s[...]
        scores = lax.dot_general(q4, k4, NT_DIMS, preferred_element_type=F32)
        scores = jnp.where(keep, scores, 0.0).astype(BF16)
        o = (jnp.dot(scores, v4, preferred_element_type=F32)
             + lax.dot_general(q4, st.astype(BF16), NT_DIMS, preferred_element_type=F32))
        kv_t = lax.dot_general(v4, kl4_s[rows4, :], TN_DIMS, preferred_element_type=F32)
        state_s[...] = dec_s[c:c + 1, :] * st + kv_t
        gate = jnp.concatenate([ggs_ref[rows, hd * GLA_DV:(hd + 1) * GLA_DV]
                                for hd in range(GLA_HEADS)], axis=0).astype(F32)
        o = (_rms_scale(o, nw) * gate).astype(BF16)
        for hd in range(GLA_HEADS):
            o_ref[rows, hd * GLA_DV:(hd + 1) * GLA_DV] = o[hd * CHUNK:(hd + 1) * CHUNK, :]


def _gla(gq, gk, g, gv, ggs, norm_w, batch, seq):
    tm = TM_GLA
    nt = seq // tm
    row = lambda b, i: (b * nt + i, 0)
    fixed = lambda b, i: (0, 0)
    T = gq.shape[0]
    return pl.pallas_call(
        _gla_kernel,
        grid=(batch, nt),
        in_specs=[
            pl.BlockSpec((tm, GLA_QK), row),
            pl.BlockSpec((tm, GLA_QK), row),
            pl.BlockSpec((tm, GLA_QK), row),
            pl.BlockSpec((tm, GLA_WIDTH), row),
            pl.BlockSpec((tm, GLA_WIDTH), row),
            pl.BlockSpec((1, GLA_DV), fixed),
        ],
        out_specs=pl.BlockSpec((tm, GLA_WIDTH), row),
        out_shape=jax.ShapeDtypeStruct((T, GLA_WIDTH), BF16),
        scratch_shapes=[
            pltpu.VMEM((GLA_HEADS * tm, GLA_QK), BF16),
            pltpu.VMEM((tm, GLA_QK), BF16),
            pltpu.VMEM((GLA_HEADS * tm, GLA_QK), BF16),
            pltpu.VMEM((tm // CHUNK, GLA_QK), F32),
            pltpu.VMEM((GLA_DV, GLA_QK), F32),
        ],
        compiler_params=pltpu.CompilerParams(dimension_semantics=("parallel", "arbitrary"),
                                             vmem_limit_bytes=VMEM_LIMIT),
        name="gla",
    )(gq, gk, g, gv, ggs, norm_w)


def _diff_attn_kernel(lam_ref, q_ref, k_ref, vt_ref, gs_ref, nw_ref, o_ref,
                      qq_s, *, lam_init):
    seq = q_ref.shape[0]
    tk = TK_ATT
    n_blk = seq // tk
    cw = 2 * tk

    lv = lam_ref[...]
    lam = (jnp.exp(jnp.sum(lv[0:1] * lv[1:2], axis=-1, keepdims=True))
           - jnp.exp(jnp.sum(lv[2:3] * lv[3:4], axis=-1, keepdims=True)) + lam_init)
    nw = nw_ref[...]

    kc = lax.broadcasted_iota(jnp.int32, (tk, cw), 0) // CHUNK
    qc = (lax.broadcasted_iota(jnp.int32, (tk, cw), 1) % tk) // CHUNK
    stair_bias = jnp.where(kc <= qc, 0.0, -jnp.inf).astype(F32)

    lane = lax.broadcasted_iota(jnp.int32, (tk, LANES), 1)
    for r in range(n_blk):
        q = q_ref[r * tk:(r + 1) * tk, :]
        zero = jnp.zeros_like(q)
        qq_s[r * cw:r * cw + tk, :] = jnp.where(lane < DIFF_DQK, q, zero)
        qq_s[r * cw + tk:(r + 1) * cw, :] = jnp.where(lane >= DIFF_DQK, q, zero)

    def scores(cb):
        nk = (cb + 1) * tk
        st = lax.dot_general(k_ref[0:nk, :], qq_s[cb * cw:(cb + 1) * cw, :], NT_DIMS,
                             preferred_element_type=F32)
        parts = [st[j * tk:(j + 1) * tk, :] for j in range(cb)] + [st[cb * tk:, :] + stair_bias]
        m = jnp.max(parts[0], axis=0, keepdims=True)
        for part in parts[1:]:
            m = jnp.maximum(m, jnp.max(part, axis=0, keepdims=True))
        return parts, m

    def finish(cb, parts, m):
        nk = (cb + 1) * tk
        pt = jnp.concatenate([jnp.exp2(part - m).astype(BF16) for part in parts], axis=0)
        pv = jnp.dot(vt_ref[:, 0:nk], pt, preferred_element_type=F32)
        o0t = pv[0:DIFF_DV, 0:tk] / pv[DIFF_DV:DIFF_DV + 1, 0:tk]
        o1t = pv[0:DIFF_DV, tk:] / pv[DIFF_DV:DIFF_DV + 1, tk:]
        o = (o0t - lam * o1t).T
        rows = slice(cb * tk, (cb + 1) * tk)
        o = _rms_scale(o, nw) * (1.0 - lam_init) * gs_ref[rows, :].astype(F32)
        o_ref[rows, :] = o.astype(BF16)

    pending = [scores(cb) for cb in range(SCORES_AHEAD)]
    for cb in range(n_blk):
        if cb + SCORES_AHEAD < n_blk:
            pending.append(scores(cb + SCORES_AHEAD))
        finish(cb, *pending.pop(0))


def _diff_attn(lam_vecs, dq, dk, dvt, dgs, norm_w, batch, seq, lam_init):
    assert TK_ATT % CHUNK == 0 and seq % TK_ATT == 0
    T = dq.shape[0]
    bh = lambda b, h: (b, h)
    hb = lambda b, h: (h, b)
    fixed = lambda b, h: (0, 0)
    blk = pl.BlockSpec((seq, DIFF_DV), bh)
    return pl.pallas_call(
        functools.partial(_diff_attn_kernel, lam_init=lam_init),
        grid=(batch, DIFF_HEADS),
        in_specs=[pl.BlockSpec(lam_vecs.shape, fixed), blk, blk,
                  pl.BlockSpec((VT_ROWS, seq), hb), blk,
                  pl.BlockSpec((1, DIFF_DV), fixed)],
        out_specs=blk,
        out_shape=jax.ShapeDtypeStruct((T, DIFF_WIDTH), BF16),
        scratch_shapes=[
            pltpu.VMEM((2 * seq, DIFF_DV), BF16),
        ],
        compiler_params=pltpu.CompilerParams(dimension_semantics=("parallel", "parallel"),
                                             vmem_limit_bytes=VMEM_LIMIT),
        name="diff_attn",
    )(lam_vecs, dq, dk, dvt, dgs, norm_w)


def _mem_kv_kernel(mem_ref, nw_ref, w_ref, k_ref, v_ref):
    d = mem_ref.shape[1]
    mn = _rms_scale(mem_ref[...], nw_ref[...]).astype(BF16)
    k_ref[...] = jnp.dot(mn, w_ref[:, :d], preferred_element_type=F32).astype(BF16)
    v_ref[...] = jnp.dot(mn, w_ref[:, d:], preferred_element_type=F32).astype(BF16)


def _mem_kv(mem2d, norm_w, w_ckv, batch, mem_len):
    D = mem2d.shape[1]
    row = lambda b: (b, 0)
    fixed = lambda b: (0, 0)
    shp = jax.ShapeDtypeStruct((batch * mem_len, D), BF16)
    return pl.pallas_call(
        _mem_kv_kernel,
        grid=(batch,),
        in_specs=[pl.BlockSpec((mem_len, D), row), pl.BlockSpec((1, D), fixed),
                  pl.BlockSpec(w_ckv.shape, fixed)],
        out_specs=(pl.BlockSpec((mem_len, D), row), pl.BlockSpec((mem_len, D), row)),
        out_shape=(shp, shp),
        compiler_params=pltpu.CompilerParams(dimension_semantics=("parallel",),
                                             vmem_limit_bytes=VMEM_LIMIT),
        name="mem_kv",
    )(mem2d, norm_w, w_ckv)


def _out_cross_kernel(x_ref, og_ref, od_ref, wo_ref, ncw_ref, wq_ref, k_ref, v_ref, wco_ref,
                      fnw_ref, o_ref, ctx_s, *, final_norm):
    d = x_ref.shape[1]
    dh = d // CROSS_HEADS
    pm = x_ref.shape[0] // ROW_PARTS
    parts = [slice(p * pm, (p + 1) * pm) for p in range(ROW_PARTS)]
    x1 = [x_ref[r, :]
          + jnp.dot(og_ref[r, :], wo_ref[0:GLA_WIDTH, :], preferred_element_type=F32)
          + jnp.dot(od_ref[r, :], wo_ref[GLA_WIDTH:, :], preferred_element_type=F32)
          for r in parts]
    q = [(jnp.dot(_rms_scale(x, ncw_ref[...]).astype(BF16), wq_ref[...],
                  preferred_element_type=F32) * (dh ** -0.5)).astype(BF16) for x in x1]
    for hd in range(CROSS_HEADS):
        sl = slice(hd * dh, (hd + 1) * dh)
        for r, qp in zip(parts, q):
            s = lax.dot_general(qp[:, sl], k_ref[:, sl], NT_DIMS, preferred_element_type=F32)
            p = jnp.exp(s - jnp.max(s, axis=-1, keepdims=True))
            p = p / jnp.sum(p, axis=-1, keepdims=True)
            ctx_s[r, sl] = jnp.dot(p.astype(BF16), v_ref[:, sl],
                                   preferred_element_type=F32).astype(BF16)
    for r, x in zip(parts, x1):
        x2 = x + jnp.dot(ctx_s[r, :], wco_ref[...], preferred_element_type=F32)
        if final_norm:
            x2 = _rms_scale(x2, fnw_ref[...])
        o_ref[r, :] = x2


def _out_cross(x2d, og, od, w_out, ncw, w_cq, kc, vc, w_co, fnw, seq, mem_len, final_norm):
    T, D = x2d.shape
    tm = TM_OUT
    per_b = seq // tm
    row = lambda i: (i, 0)
    fixed = lambda i: (0, 0)
    bmap = lambda i: (i // per_b, 0)
    return pl.pallas_call(
        functools.partial(_out_cross_kernel, final_norm=final_norm),
        grid=(T // tm,),
        in_specs=[
            pl.BlockSpec((tm, D), row),
            pl.BlockSpec((tm, GLA_WIDTH), row),
            pl.BlockSpec((tm, DIFF_WIDTH), row),
            pl.BlockSpec(w_out.shape, fixed),
            pl.BlockSpec((1, D), fixed),
            pl.BlockSpec(w_cq.shape, fixed),
            pl.BlockSpec((mem_len, D), bmap),
            pl.BlockSpec((mem_len, D), bmap),
            pl.BlockSpec(w_co.shape, fixed),
            pl.BlockSpec((1, D), fixed),
        ],
        out_specs=pl.BlockSpec((tm, D), row),
        out_shape=jax.ShapeDtypeStruct((T, D), F32),
        scratch_shapes=[pltpu.VMEM((tm, D), BF16)],
        compiler_params=pltpu.CompilerParams(dimension_semantics=("parallel",),
                                             vmem_limit_bytes=VMEM_LIMIT),
        name="out_cross",
    )(x2d, og, od, w_out, ncw, w_cq, kc, vc, w_co, fnw)


def _rope_tables(positions):
    half = ROPE_DIM // 2
    T = positions.size
    per_row = LANES // half
    inv_freq = ROPE_THETA ** (-(jnp.arange(0, ROPE_DIM, 2, dtype=F32) / ROPE_DIM))
    pos = positions.reshape(T // per_row, per_row).astype(F32)
    ang = jnp.repeat(pos, half, axis=1) * jnp.tile(inv_freq, per_row)
    pieces = []
    rest_c, rest_s = jnp.cos(ang), jnp.sin(ang)
    for _ in range(ROPE_SPLIT):
        pc, ps = rest_c.astype(BF16), rest_s.astype(BF16)
        pieces += [pc.reshape(T, half), ps.reshape(T, half)]
        rest_c, rest_s = rest_c - pc.astype(F32), rest_s - ps.astype(F32)
    return jnp.concatenate(pieces, axis=1)


def _rope_select_matrix():
    half = ROPE_DIM // 2
    sel = np.zeros((ROPE_DIM, 2 * LANES), np.float32)
    for j in range(LANES):
        jm = j % DIFF_DQK
        if jm < ROPE_DIM:
            sel[jm % half, j] = 1.0
            sel[half + jm % half, LANES + j] = -1.0 if jm < half else 1.0
    return jnp.asarray(np.tile(sel, (ROPE_SPLIT, 1)), BF16)


def kernel(x, mem, positions, norm_mix_w, w_in, w_alpha2, b_alpha2, gla_norm_w,
           lam_q1, lam_k1, lam_q2, lam_k2, diff_norm_w, w_out,
           norm_cross_w, norm_mem_w, w_cq, w_ckv, w_co, final_norm_w):
    B, S, D = x.shape
    M = mem.shape[1]
    depth = w_in.shape[0]
    T = B * S
    cs_t = _rope_tables(positions)
    sel = _rope_select_matrix()
    x2d = x.reshape(T, D)
    mem2d = mem.reshape(B * M, D)

    c_gv = 2 * GLA_QK
    c_ga = c_gv + 2 * GLA_WIDTH
    c_dq = c_ga + GLA_LOWRANK

    for l in range(depth):
        wl = w_in[l]
        ga_pad = jnp.pad(wl[:, c_ga:c_dq], ((0, 0), (0, LANES - GLA_LOWRANK)))
        wa = jnp.concatenate([wl[:, :c_gv], ga_pad], axis=1).astype(BF16)
        c_dv = c_dq + 2 * DIFF_WIDTH
        c_dg = c_dv + DIFF_WIDTH
        wb = jnp.concatenate([wl[:, c_gv:c_ga], wl[:, c_dq:c_dv], wl[:, c_dg:]],
                             axis=1).astype(BF16)
        wvt = wl[:, c_dv:c_dg].T.astype(BF16)
        wa2 = jnp.pad(w_alpha2[l], ((0, LANES - GLA_LOWRANK), (0, 0))).astype(BF16)
        ba2 = b_alpha2[l].reshape(1, GLA_QK).astype(F32)

        gq, gk, gv, ggs, g, dq, dk, dvt, dgs = _in_proj(
            x2d, norm_mix_w[l].reshape(1, D), wa, wa2, ba2, wb, wvt, cs_t, sel)

        o_gla = _gla(gq, gk, g, gv, ggs, gla_norm_w[l].reshape(1, GLA_DV), B, S)

        lam_init = 0.8 - 0.6 * math.exp(-0.3 * l)
        lam_vecs = jnp.stack([lam_q1[l], lam_k1[l], lam_q2[l], lam_k2[l]]).astype(F32)
        o_diff = _diff_attn(lam_vecs, dq, dk, dvt, dgs, diff_norm_w[l].reshape(1, DIFF_DV),
                            B, S, lam_init)

        kc, vc = _mem_kv(mem2d, norm_mem_w[l].reshape(1, D), w_ckv[l].astype(BF16), B, M)

        x2d = _out_cross(x2d, o_gla, o_diff, w_out[l].astype(BF16),
                         norm_cross_w[l].reshape(1, D), w_cq[l].astype(BF16), kc, vc,
                         w_co[l].astype(BF16), final_norm_w.reshape(1, D), S, M,
                         final_norm=(l == depth - 1))
    return x2d.reshape(B, S, D)
```

```python
import functools
import math

import jax
import jax.numpy as jnp
import numpy as np
from jax import lax
from jax.experimental import pallas as pl
from jax.experimental.pallas import tpu as pltpu

F32 = jnp.float32
BF16 = jnp.bfloat16

CHUNK = 64
GLA_HEADS = 4
GLA_DK = 64
GLA_DV = 128
GLA_QK = GLA_HEADS * GLA_DK
GLA_WIDTH = GLA_HEADS * GLA_DV
GLA_LOWRANK = 16
GLA_TAU = 16.0
DIFF_HEADS = 4
DIFF_DQK = 64
DIFF_DV = 128
DIFF_WIDTH = DIFF_HEADS * DIFF_DV
ROPE_DIM = 16
ROPE_THETA = 500000.0
CROSS_HEADS = 4
EPS = 1e-6
LOG2E = math.log2(math.e)
ROPE_SPLIT = 3

LANES = 128

TM_PROJ = 1024
TM_GLA = 512
TK_ATT = 256
SUM_ROWS = 16
VT_ROWS = DIFF_DV + SUM_ROWS
SCORES_AHEAD = 2
TM_OUT = 1024
ROW_PARTS = 2
VMEM_LIMIT = 56 * 1024 * 1024

NT_DIMS = (((1,), (1,)), ((), ()))
TN_DIMS = (((0,), (0,)), ((), ()))


def _rms_scale(xf, w):
    return xf * lax.rsqrt(jnp.mean(xf * xf, axis=-1, keepdims=True) + EPS) * w


def _silu(t):
    return t * (1.0 / (1.0 + jnp.exp(-t)))


def _in_proj_kernel(x_ref, nw_ref, wa_ref, wa2_ref, ba2_ref, wb_ref, wvt_ref, cs_ref, sel_ref,
                    gq_ref, gk_ref, gv_ref, ggs_ref, g_ref, dq_ref, dk_ref, dvt_ref, dgs_ref):
    pm = x_ref.shape[0] // ROW_PARTS
    for part in range(ROW_PARTS):
        _in_proj_rows(slice(part * pm, (part + 1) * pm),
                      x_ref, nw_ref, wa_ref, wa2_ref, ba2_ref, wb_ref, wvt_ref, cs_ref, sel_ref,
                      gq_ref, gk_ref, gv_ref, ggs_ref, g_ref, dq_ref, dk_ref, dvt_ref, dgs_ref)


def _in_proj_rows(rows, x_ref, nw_ref, wa_ref, wa2_ref, ba2_ref, wb_ref, wvt_ref, cs_ref, sel_ref,
                  gq_ref, gk_ref, gv_ref, ggs_ref, g_ref, dq_ref, dk_ref, dvt_ref, dgs_ref):
    h = _rms_scale(x_ref[rows, :], nw_ref[...]).astype(BF16)

    def proj_b(group):
        return jnp.dot(h, wb_ref[:, group * GLA_WIDTH:(group + 1) * GLA_WIDTH],
                       preferred_element_type=F32)

    qkg = jnp.dot(h, wa_ref[...], preferred_element_type=F32)
    gq_ref[rows, :] = qkg[:, 0:GLA_QK] * (GLA_DK ** -0.5)
    gk_ref[rows, :] = qkg[:, GLA_QK:2 * GLA_QK]
    ga = qkg[:, 2 * GLA_QK:].astype(BF16)

    gv_ref[rows, :] = proj_b(0).astype(BF16)
    ggs_ref[rows, :] = _silu(proj_b(1)).astype(BF16)

    tabs = jnp.dot(cs_ref[rows, :], sel_ref[...], preferred_element_type=F32)
    lane = lax.broadcasted_iota(jnp.int32, (tabs.shape[0], LANES), 1)
    cos = tabs[:, :LANES] + jnp.where((lane % DIFF_DQK) < ROPE_DIM, 0.0, 1.0)
    sin = tabs[:, LANES:]
    first_half = (lane % DIFF_DQK) < (ROPE_DIM // 2)

    def rope(t):
        partner = jnp.where(first_half, pltpu.roll(t, LANES - ROPE_DIM // 2, 1),
                            pltpu.roll(t, ROPE_DIM // 2, 1))
        return t * cos + partner * sin

    q = proj_b(2)
    k = proj_b(3)
    for hd in range(DIFF_HEADS):
        sl = slice(hd * DIFF_DV, (hd + 1) * DIFF_DV)
        dq_ref[rows, sl] = (rope(q[:, sl]) * (DIFF_DQK ** -0.5 * LOG2E)).astype(BF16)
        dk_ref[rows, sl] = rope(k[:, sl]).astype(BF16)
    dvt = lax.dot_general(wvt_ref[...], h, NT_DIMS, preferred_element_type=F32).astype(BF16)
    for hd in range(DIFF_HEADS):
        base = hd * VT_ROWS
        dvt_ref[base:base + DIFF_DV, rows] = dvt[hd * DIFF_DV:(hd + 1) * DIFF_DV, :]
        dvt_ref[base + DIFF_DV:base + VT_ROWS, rows] = jnp.ones((SUM_ROWS, dvt.shape[1]), BF16)

    logit = jnp.dot(ga, wa2_ref[...], preferred_element_type=F32) + ba2_ref[...]
    log_sig = jnp.minimum(logit, 0.0) - jnp.log1p(jnp.exp(-jnp.abs(logit)))
    g_ref[rows, :] = log_sig * (1.0 / GLA_TAU)

    dgs_ref[rows, :] = _silu(proj_b(4)).astype(BF16)


def _in_proj(x2d, norm_w, wa, wa2, ba2, wb, wvt, cs_t, sel):
    T, D = x2d.shape
    tm = TM_PROJ
    row = lambda i: (i, 0)
    col = lambda i: (0, i)
    fixed = lambda i: (0, 0)
    out_shapes = (
        jax.ShapeDtypeStruct((T, GLA_QK), F32),
        jax.ShapeDtypeStruct((T, GLA_QK), F32),
        jax.ShapeDtypeStruct((T, GLA_WIDTH), BF16),
        jax.ShapeDtypeStruct((T, GLA_WIDTH), BF16),
        jax.ShapeDtypeStruct((T, GLA_QK), F32),
        jax.ShapeDtypeStruct((T, DIFF_WIDTH), BF16),
        jax.ShapeDtypeStruct((T, DIFF_WIDTH), BF16),
        jax.ShapeDtypeStruct((DIFF_HEADS * VT_ROWS, T), BF16),
        jax.ShapeDtypeStruct((T, DIFF_WIDTH), BF16),
    )
    out_specs = [pl.BlockSpec((tm, s.shape[1]), row) for s in out_shapes]
    out_specs[7] = pl.BlockSpec((DIFF_HEADS * VT_ROWS, tm), col)
    return pl.pallas_call(
        _in_proj_kernel,
        grid=(T // tm,),
        in_specs=[
            pl.BlockSpec((tm, D), row),
            pl.BlockSpec((1, D), fixed),
            pl.BlockSpec(wa.shape, fixed),
            pl.BlockSpec(wa2.shape, fixed),
            pl.BlockSpec(ba2.shape, fixed),
            pl.BlockSpec(wb.shape, fixed),
            pl.BlockSpec(wvt.shape, fixed),
            pl.BlockSpec((tm, cs_t.shape[1]), row),
            pl.BlockSpec(sel.shape, fixed),
        ],
        out_specs=tuple(out_specs),
        out_shape=out_shapes,
        compiler_params=pltpu.CompilerParams(dimension_semantics=("parallel",),
                                             vmem_limit_bytes=VMEM_LIMIT),
        name="in_proj",
    )(x2d, norm_w, wa, wa2, ba2, wb, wvt, cs_t, sel)


def _gla_kernel(gq_ref, gk_ref, g_ref, gv_ref, ggs_ref, nw_ref, o_ref,
                qd4_s, ki_s, kl4_s, dec_s, state_s):
    tm = gq_ref.shape[0]
    n_chunks = tm // CHUNK

    @pl.when(pl.program_id(1) == 0)
    def _():
        state_s[...] = jnp.zeros_like(state_s)

    g = g_ref[...]
    row_in_chunk = lax.broadcasted_iota(jnp.int32, g.shape, 0) % CHUNK
    G = g
    shift = 1
    while shift < CHUNK:
        G = G + jnp.where(row_in_chunk >= shift, pltpu.roll(G, shift, 0), 0.0)
        shift *= 2
    q = gq_ref[...]
    k = gk_ref[...]
    qd = (q * jnp.exp(G)).astype(BF16)
    ki_s[...] = (k * jnp.exp(-G)).astype(BF16)

    lane_head = lax.broadcasted_iota(jnp.int32, (CHUNK, GLA_QK), 1) // GLA_DK
    zero = jnp.zeros((CHUNK, GLA_QK), BF16)
    for c in range(n_chunks):
        rows = slice(c * CHUNK, (c + 1) * CHUNK)
        g_last = G[(c + 1) * CHUNK - 1:(c + 1) * CHUNK, :]
        kl = (k[rows, :] * jnp.exp(g_last - G[rows, :])).astype(BF16)
        dec_s[c:c + 1, :] = jnp.exp(g_last)
        for hd in range(GLA_HEADS):
            dst = slice((c * GLA_HEADS + hd) * CHUNK, (c * GLA_HEADS + hd + 1) * CHUNK)
            qd4_s[dst, :] = jnp.where(lane_head == hd, qd[rows, :], zero)
            kl4_s[dst, :] = jnp.where(lane_head == hd, kl, zero)

    r_i = lax.broadcasted_iota(jnp.int32, (GLA_HEADS * CHUNK, GLA_HEADS * CHUNK), 0)
    c_i = lax.broadcasted_iota(jnp.int32, (GLA_HEADS * CHUNK, GLA_HEADS * CHUNK), 1)
    keep = (r_i // CHUNK == c_i // CHUNK) & (c_i % CHUNK <= r_i % CHUNK)
    nw = nw_ref[...]

    def chunk_rows4(c):
        return slice(c * GLA_HEADS * CHUNK, (c + 1) * GLA_HEADS * CHUNK)

    def chunk_values(c):
        rows = slice(c * CHUNK, (c + 1) * CHUNK)
        return jnp.concatenate([gv_ref[rows, hd * GLA_DV:(hd + 1) * GLA_DV]
                                for hd in range(GLA_HEADS)], axis=0)

    def state_free_products(c):
        ki = ki_s[c * CHUNK:(c + 1) * CHUNK, :]
        k4 = jnp.concatenate([ki] * GLA_HEADS, axis=0)
        scores = lax.dot_general(qd4_s[chunk_rows4(c), :], k4, NT_DIMS, preferred_element_type=F32)
        kv_t = lax.dot_general(chunk_values(c), kl4_s[chunk_rows4(c), :], TN_DIMS,
                               preferred_element_type=F32)
        return scores, kv_t

    ahead = state_free_products(0)
    for c in range(n_chunks):
        rows = slice(c * CHUNK, (c + 1) * CHUNK)
        scores, kv_t = ahead
        if c + 1 < n_chunks:
            ahead = state_free_products(c + 1)
        q4 = qd4_s[chunk_rows4(c), :]
        st = state_s[...]
        scores = jnp.where(keep, scores, 0.0).astype(BF16)
        o = (jnp.dot(scores, chunk_values(c), preferred_element_type=F32)
             + lax.dot_general(q4, st.astype(BF16), NT_DIMS, preferred_element_type=F32))
        state_s[...] = dec_s[c:c + 1, :] * st + kv_t
        gate = jnp.concatenate([ggs_ref[rows, hd * GLA_DV:(hd + 1) * GLA_DV]
                                for hd in range(GLA_HEADS)], axis=0).astype(F32)
        o = (_rms_scale(o, nw) * gate).astype(BF16)
        for hd in range(GLA_HEADS):
            o_ref[rows, hd * GLA_DV:(hd + 1) * GLA_DV] = o[hd * CHUNK:(hd + 1) * CHUNK, :]


def _gla(gq, gk, g, gv, ggs, norm_w, batch, seq):
    tm = TM_GLA
    nt = seq // tm
    row = lambda b, i: (b * nt + i, 0)
    fixed = lambda b, i: (0, 0)
    T = gq.shape[0]
    return pl.pallas_call(
        _gla_kernel,
        grid=(batch, nt),
        in_specs=[
            pl.BlockSpec((tm, GLA_QK), row),
            pl.BlockSpec((tm, GLA_QK), row),
            pl.BlockSpec((tm, GLA_QK), row),
            pl.BlockSpec((tm, GLA_WIDTH), row),
            pl.BlockSpec((tm, GLA_WIDTH), row),
            pl.BlockSpec((1, GLA_DV), fixed),
        ],
        out_specs=pl.BlockSpec((tm, GLA_WIDTH), row),
        out_shape=jax.ShapeDtypeStruct((T, GLA_WIDTH), BF16),
        scratch_shapes=[
            pltpu.VMEM((GLA_HEADS * tm, GLA_QK), BF16),
            pltpu.VMEM((tm, GLA_QK), BF16),
            pltpu.VMEM((GLA_HEADS * tm, GLA_QK), BF16),
            pltpu.VMEM((tm // CHUNK, GLA_QK), F32),
            pltpu.VMEM((GLA_DV, GLA_QK), F32),
        ],
        compiler_params=pltpu.CompilerParams(dimension_semantics=("parallel", "arbitrary"),
                                             vmem_limit_bytes=VMEM_LIMIT),
        name="gla",
    )(gq, gk, g, gv, ggs, norm_w)


def _diff_attn_kernel(lam_ref, q_ref, k_ref, vt_ref, gs_ref, nw_ref, o_ref,
                      qq_s, *, lam_init):
    seq = q_ref.shape[0]
    tk = TK_ATT
    n_blk = seq // tk
    cw = 2 * tk

    lv = lam_ref[...]
    lam = (jnp.exp(jnp.sum(lv[0:1] * lv[1:2], axis=-1, keepdims=True))
           - jnp.exp(jnp.sum(lv[2:3] * lv[3:4], axis=-1, keepdims=True)) + lam_init)
    nw = nw_ref[...]

    kc = lax.broadcasted_iota(jnp.int32, (tk, cw), 0) // CHUNK
    qc = (lax.broadcasted_iota(jnp.int32, (tk, cw), 1) % tk) // CHUNK
    stair_bias = jnp.where(kc <= qc, 0.0, -jnp.inf).astype(F32)

    lane = lax.broadcasted_iota(jnp.int32, (tk, LANES), 1)
    for r in range(n_blk):
        q = q_ref[r * tk:(r + 1) * tk, :]
        zero = jnp.zeros_like(q)
        qq_s[r * cw:r * cw + tk, :] = jnp.where(lane < DIFF_DQK, q, zero)
        qq_s[r * cw + tk:(r + 1) * cw, :] = jnp.where(lane >= DIFF_DQK, q, zero)

    def scores(cb):
        nk = (cb + 1) * tk
        st = lax.dot_general(k_ref[0:nk, :], qq_s[cb * cw:(cb + 1) * cw, :], NT_DIMS,
                             preferred_element_type=F32)
        parts = [st[j * tk:(j + 1) * tk, :] for j in range(cb)] + [st[cb * tk:, :] + stair_bias]
        m = jnp.max(parts[0], axis=0, keepdims=True)
        for part in parts[1:]:
            m = jnp.maximum(m, jnp.max(part, axis=0, keepdims=True))
        return parts, m

    def finish(cb, parts, m):
        nk = (cb + 1) * tk
        pt = jnp.concatenate([jnp.exp2(part - m).astype(BF16) for part in parts], axis=0)
        pv = jnp.dot(vt_ref[:, 0:nk], pt, preferred_element_type=F32)
        o0t = pv[0:DIFF_DV, 0:tk] / pv[DIFF_DV:DIFF_DV + 1, 0:tk]
        o1t = pv[0:DIFF_DV, tk:] / pv[DIFF_DV:DIFF_DV + 1, tk:]
        o = (o0t - lam * o1t).T
        rows = slice(cb * tk, (cb + 1) * tk)
        o = _rms_scale(o, nw) * (1.0 - lam_init) * gs_ref[rows, :].astype(F32)
        o_ref[rows, :] = o.astype(BF16)

    pending = [scores(cb) for cb in range(SCORES_AHEAD)]
    for cb in range(n_blk):
        if cb + SCORES_AHEAD < n_blk:
            pending.append(scores(cb + SCORES_AHEAD))
        finish(cb, *pending.pop(0))


def _diff_attn(lam_vecs, dq, dk, dvt, dgs, norm_w, batch, seq, lam_init):
    assert TK_ATT % CHUNK == 0 and seq % TK_ATT == 0
    T = dq.shape[0]
    bh = lambda b, h: (b, h)
    hb = lambda b, h: (h, b)
    fixed = lambda b, h: (0, 0)
    blk = pl.BlockSpec((seq, DIFF_DV), bh)
    return pl.pallas_call(
        functools.partial(_diff_attn_kernel, lam_init=lam_init),
        grid=(batch, DIFF_HEADS),
        in_specs=[pl.BlockSpec(lam_vecs.shape, fixed), blk, blk,
                  pl.BlockSpec((VT_ROWS, seq), hb), blk,
                  pl.BlockSpec((1, DIFF_DV), fixed)],
        out_specs=blk,
        out_shape=jax.ShapeDtypeStruct((T, DIFF_WIDTH), BF16),
        scratch_shapes=[
            pltpu.VMEM((2 * seq, DIFF_DV), BF16),
        ],
        compiler_params=pltpu.CompilerParams(dimension_semantics=("parallel", "parallel"),
                                             vmem_limit_bytes=VMEM_LIMIT),
        name="diff_attn",
    )(lam_vecs, dq, dk, dvt, dgs, norm_w)


def _mem_kv_kernel(mem_ref, nw_ref, w_ref, k_ref, v_ref):
    d = mem_ref.shape[1]
    mn = _rms_scale(mem_ref[...], nw_ref[...]).astype(BF16)
    k_ref[...] = jnp.dot(mn, w_ref[:, :d], preferred_element_type=F32).astype(BF16)
    v_ref[...] = jnp.dot(mn, w_ref[:, d:], preferred_element_type=F32).astype(BF16)


def _mem_kv(mem2d, norm_w, w_ckv, batch, mem_len):
    D = mem2d.shape[1]
    row = lambda b: (b, 0)
    fixed = lambda b: (0, 0)
    shp = jax.ShapeDtypeStruct((batch * mem_len, D), BF16)
    return pl.pallas_call(
        _mem_kv_kernel,
        grid=(batch,),
        in_specs=[pl.BlockSpec((mem_len, D), row), pl.BlockSpec((1, D), fixed),
                  pl.BlockSpec(w_ckv.shape, fixed)],
        out_specs=(pl.BlockSpec((mem_len, D), row), pl.BlockSpec((mem_len, D), row)),
        out_shape=(shp, shp),
        compiler_params=pltpu.CompilerParams(dimension_semantics=("parallel",),
                                             vmem_limit_bytes=VMEM_LIMIT),
        name="mem_kv",
    )(mem2d, norm_w, w_ckv)


def _out_cross_kernel(x_ref, og_ref, od_ref, wo_ref, ncw_ref, wq_ref, k_ref, v_ref, wco_ref,
                      fnw_ref, o_ref, ctx_s, *, final_norm):
    d = x_ref.shape[1]
    dh = d // CROSS_HEADS
    pm = x_ref.shape[0] // ROW_PARTS
    parts = [slice(p * pm, (p + 1) * pm) for p in range(ROW_PARTS)]
    x1 = [x_ref[r, :]
          + jnp.dot(og_ref[r, :], wo_ref[0:GLA_WIDTH, :], preferred_element_type=F32)
          + jnp.dot(od_ref[r, :], wo_ref[GLA_WIDTH:, :], preferred_element_type=F32)
          for r in parts]
    q = [(jnp.dot(_rms_scale(x, ncw_ref[...]).astype(BF16), wq_ref[...],
                  preferred_element_type=F32) * (dh ** -0.5)).astype(BF16) for x in x1]
    def head_scores(hd, part):
        sl = slice(hd * dh, (hd + 1) * dh)
        return lax.dot_general(q[part][:, sl], k_ref[:, sl], NT_DIMS,
                               preferred_element_type=F32)

    items = [(hd, part) for hd in range(CROSS_HEADS) for part in range(ROW_PARTS)]
    s_next = head_scores(*items[0])
    for n, (hd, part) in enumerate(items):
        s = s_next
        if n + 1 < len(items):
            s_next = head_scores(*items[n + 1])
        sl = slice(hd * dh, (hd + 1) * dh)
        p = jnp.exp(s - jnp.max(s, axis=-1, keepdims=True))
        p = p / jnp.sum(p, axis=-1, keepdims=True)
        ctx_s[parts[part], sl] = jnp.dot(p.astype(BF16), v_ref[:, sl],
                                         preferred_element_type=F32).astype(BF16)
    for r, x in zip(parts, x1):
        x2 = x + jnp.dot(ctx_s[r, :], wco_ref[...], preferred_element_type=F32)
        if final_norm:
            x2 = _rms_scale(x2, fnw_ref[...])
        o_ref[r, :] = x2


def _out_cross(x2d, og, od, w_out, ncw, w_cq, kc, vc, w_co, fnw, seq, mem_len, final_norm):
    T, D = x2d.shape
    tm = TM_OUT
    per_b = seq // tm
    row = lambda i: (i, 0)
    fixed = lambda i: (0, 0)
    bmap = lambda i: (i // per_b, 0)
    return pl.pallas_call(
        functools.partial(_out_cross_kernel, final_norm=final_norm),
        grid=(T // tm,),
        in_specs=[
            pl.BlockSpec((tm, D), row),
            pl.BlockSpec((tm, GLA_WIDTH), row),
            pl.BlockSpec((tm, DIFF_WIDTH), row),
            pl.BlockSpec(w_out.shape, fixed),
            pl.BlockSpec((1, D), fixed),
            pl.BlockSpec(w_cq.shape, fixed),
            pl.BlockSpec((mem_len, D), bmap),
            pl.BlockSpec((mem_len, D), bmap),
            pl.BlockSpec(w_co.shape, fixed),
            pl.BlockSpec((1, D), fixed),
        ],
        out_specs=pl.BlockSpec((tm, D), row),
        out_shape=jax.ShapeDtypeStruct((T, D), F32),
        scratch_shapes=[pltpu.VMEM((tm, D), BF16)],
        compiler_params=pltpu.CompilerParams(dimension_semantics=("parallel",),
                                             vmem_limit_bytes=VMEM_LIMIT),
        name="out_cross",
    )(x2d, og, od, w_out, ncw, w_cq, kc, vc, w_co, fnw)


def _rope_tables(positions):
    half = ROPE_DIM // 2
    T = positions.size
    per_row = LANES // half
    inv_freq = ROPE_THETA ** (-(jnp.arange(0, ROPE_DIM, 2, dtype=F32) / ROPE_DIM))
    pos = positions.reshape(T // per_row, per_row).astype(F32)
    ang = jnp.repeat(pos, half, axis=1) * jnp.tile(inv_freq, per_row)
    pieces = []
    rest_c, rest_s = jnp.cos(ang), jnp.sin(ang)
    for _ in range(ROPE_SPLIT):
        pc, ps = rest_c.astype(BF16), rest_s.astype(BF16)
        pieces += [pc, ps]
        rest_c, rest_s = rest_c - pc.astype(F32), rest_s - ps.astype(F32)
    stacked = jnp.stack(pieces).reshape(2 * ROPE_SPLIT, T // per_row, per_row, half)
    return stacked.transpose(1, 2, 0, 3).reshape(T, 2 * ROPE_SPLIT * half)


def _rope_select_matrix():
    half = ROPE_DIM // 2
    sel = np.zeros((ROPE_DIM, 2 * LANES), np.float32)
    for j in range(LANES):
        jm = j % DIFF_DQK
        if jm < ROPE_DIM:
            sel[jm % half, j] = 1.0
            sel[half + jm % half, LANES + j] = -1.0 if jm < half else 1.0
    return jnp.asarray(np.tile(sel, (ROPE_SPLIT, 1)), BF16)


def kernel(x, mem, positions, norm_mix_w, w_in, w_alpha2, b_alpha2, gla_norm_w,
           lam_q1, lam_k1, lam_q2, lam_k2, diff_norm_w, w_out,
           norm_cross_w, norm_mem_w, w_cq, w_ckv, w_co, final_norm_w):
    B, S, D = x.shape
    M = mem.shape[1]
    depth = w_in.shape[0]
    T = B * S
    cs_t = _rope_tables(positions)
    sel = _rope_select_matrix()
    x2d = x.reshape(T, D)
    mem2d = mem.reshape(B * M, D)

    c_gv = 2 * GLA_QK
    c_ga = c_gv + 2 * GLA_WIDTH
    c_dq = c_ga + GLA_LOWRANK

    for l in range(depth):
        wl = w_in[l]
        ga_pad = jnp.pad(wl[:, c_ga:c_dq], ((0, 0), (0, LANES - GLA_LOWRANK)))
        wa = jnp.concatenate([wl[:, :c_gv], ga_pad], axis=1).astype(BF16)
        c_dv = c_dq + 2 * DIFF_WIDTH
        c_dg = c_dv + DIFF_WIDTH
        wb = jnp.concatenate([wl[:, c_gv:c_ga], wl[:, c_dq:c_dv], wl[:, c_dg:]],
                             axis=1).astype(BF16)
        wvt = wl[:, c_dv:c_dg].T.astype(BF16)
        wa2 = jnp.pad(w_alpha2[l], ((0, LANES - GLA_LOWRANK), (0, 0))).astype(BF16)
        ba2 = b_alpha2[l].reshape(1, GLA_QK).astype(F32)

        gq, gk, gv, ggs, g, dq, dk, dvt, dgs = _in_proj(
            x2d, norm_mix_w[l].reshape(1, D), wa, wa2, ba2, wb, wvt, cs_t, sel)

        o_gla = _gla(gq, gk, g, gv, ggs, gla_norm_w[l].reshape(1, GLA_DV), B, S)

        lam_init = 0.8 - 0.6 * math.exp(-0.3 * l)
        lam_vecs = jnp.stack([lam_q1[l], lam_k1[l], lam_q2[l], lam_k2[l]]).astype(F32)
        o_diff = _diff_attn(lam_vecs, dq, dk, dvt, dgs, diff_norm_w[l].reshape(1, DIFF_DV),
                            B, S, lam_init)

        kc, vc = _mem_kv(mem2d, norm_mem_w[l].reshape(1, D), w_ckv[l].astype(BF16), B, M)

        x2d = _out_cross(x2d, o_gla, o_diff, w_out[l].astype(BF16),
                         norm_cross_w[l].reshape(1, D), w_cq[l].astype(BF16), kc, vc,
                         w_co[l].astype(BF16), final_norm_w.reshape(1, D), S, M,
                         final_norm=(l == depth - 1))
    return x2d.reshape(B, S, D)
```

```python
import functools
import math

import jax
import jax.numpy as jnp
import numpy as np
from jax import lax
from jax.experimental import pallas as pl
from jax.experimental.pallas import tpu as pltpu

F32 = jnp.float32
BF16 = jnp.bfloat16

CHUNK = 64
GLA_HEADS = 4
GLA_DK = 64
GLA_DV = 128
GLA_QK = GLA_HEADS * GLA_DK
GLA_WIDTH = GLA_HEADS * GLA_DV
GLA_LOWRANK = 16
GLA_TAU = 16.0
DIFF_HEADS = 4
DIFF_DQK = 64
DIFF_DV = 128
DIFF_WIDTH = DIFF_HEADS * DIFF_DV
ROPE_DIM = 16
ROPE_THETA = 500000.0
CROSS_HEADS = 4
EPS = 1e-6
LOG2E = math.log2(math.e)
ROPE_SPLIT = 3

LANES = 128

TM_PROJ = 1024
TM_GLA = 1024
TK_ATT = 256
SUM_ROWS = 16
VT_ROWS = DIFF_DV + SUM_ROWS
SCORES_AHEAD = 1
TM_OUT = 1024
ROW_PARTS = 2
VMEM_LIMIT = 56 * 1024 * 1024

NT_DIMS = (((1,), (1,)), ((), ()))
TN_DIMS = (((0,), (0,)), ((), ()))


def _rms_scale(xf, w):
    return xf * lax.rsqrt(jnp.mean(xf * xf, axis=-1, keepdims=True) + EPS) * w


def _silu(t):
    return t * (1.0 / (1.0 + jnp.exp(-t)))


def _in_proj_kernel(x_ref, nw_ref, wa_ref, wa2_ref, ba2_ref, wb_ref, wvt_ref, cs_ref, sel_ref,
                    gq_ref, gk_ref, gv_ref, ggs_ref, g_ref, dq_ref, dk_ref, dvt_ref, dgs_ref):
    pm = x_ref.shape[0] // ROW_PARTS
    for part in range(ROW_PARTS):
        _in_proj_rows(slice(part * pm, (part + 1) * pm),
                      x_ref, nw_ref, wa_ref, wa2_ref, ba2_ref, wb_ref, wvt_ref, cs_ref, sel_ref,
                      gq_ref, gk_ref, gv_ref, ggs_ref, g_ref, dq_ref, dk_ref, dvt_ref, dgs_ref)


def _in_proj_rows(rows, x_ref, nw_ref, wa_ref, wa2_ref, ba2_ref, wb_ref, wvt_ref, cs_ref, sel_ref,
                  gq_ref, gk_ref, gv_ref, ggs_ref, g_ref, dq_ref, dk_ref, dvt_ref, dgs_ref):
    h = _rms_scale(x_ref[rows, :], nw_ref[...]).astype(BF16)

    def proj_b(group):
        return jnp.dot(h, wb_ref[:, group * GLA_WIDTH:(group + 1) * GLA_WIDTH],
                       preferred_element_type=F32)

    qkg = jnp.dot(h, wa_ref[...], preferred_element_type=F32)
    gq_ref[rows, :] = qkg[:, 0:GLA_QK] * (GLA_DK ** -0.5)
    gk_ref[rows, :] = qkg[:, GLA_QK:2 * GLA_QK]
    ga = qkg[:, 2 * GLA_QK:].astype(BF16)

    gv_ref[rows, :] = proj_b(0).astype(BF16)
    ggs_ref[rows, :] = _silu(proj_b(1)).astype(BF16)

    tabs = jnp.dot(cs_ref[rows, :], sel_ref[...], preferred_element_type=F32)
    lane = lax.broadcasted_iota(jnp.int32, (tabs.shape[0], LANES), 1)
    cos = tabs[:, :LANES] + jnp.where((lane % DIFF_DQK) < ROPE_DIM, 0.0, 1.0)
    sin = tabs[:, LANES:]
    first_half = (lane % DIFF_DQK) < (ROPE_DIM // 2)

    def rope(t):
        partner = jnp.where(first_half, pltpu.roll(t, LANES - ROPE_DIM // 2, 1),
                            pltpu.roll(t, ROPE_DIM // 2, 1))
        return t * cos + partner * sin

    q = proj_b(2)
    k = proj_b(3)
    for hd in range(DIFF_HEADS):
        sl = slice(hd * DIFF_DV, (hd + 1) * DIFF_DV)
        dq_ref[rows, sl] = (rope(q[:, sl]) * (DIFF_DQK ** -0.5 * LOG2E)).astype(BF16)
        dk_ref[rows, sl] = rope(k[:, sl]).astype(BF16)
    dvt = lax.dot_general(wvt_ref[...], h, NT_DIMS, preferred_element_type=F32).astype(BF16)
    for hd in range(DIFF_HEADS):
        base = hd * VT_ROWS
        dvt_ref[base:base + DIFF_DV, rows] = dvt[hd * DIFF_DV:(hd + 1) * DIFF_DV, :]
        dvt_ref[base + DIFF_DV:base + VT_ROWS, rows] = jnp.ones((SUM_ROWS, dvt.shape[1]), BF16)

    logit = jnp.dot(ga, wa2_ref[...], preferred_element_type=F32) + ba2_ref[...]
    log_sig = jnp.minimum(logit, 0.0) - jnp.log1p(jnp.exp(-jnp.abs(logit)))
    g_ref[rows, :] = log_sig * (1.0 / GLA_TAU)

    dgs_ref[rows, :] = _silu(proj_b(4)).astype(BF16)


def _in_proj(x2d, norm_w, wa, wa2, ba2, wb, wvt, cs_t, sel):
    T, D = x2d.shape
    tm = TM_PROJ
    row = lambda i: (i, 0)
    col = lambda i: (0, i)
    fixed = lambda i: (0, 0)
    out_shapes = (
        jax.ShapeDtypeStruct((T, GLA_QK), F32),
        jax.ShapeDtypeStruct((T, GLA_QK), F32),
        jax.ShapeDtypeStruct((T, GLA_WIDTH), BF16),
        jax.ShapeDtypeStruct((T, GLA_WIDTH), BF16),
        jax.ShapeDtypeStruct((T, GLA_QK), F32),
        jax.ShapeDtypeStruct((T, DIFF_WIDTH), BF16),
        jax.ShapeDtypeStruct((T, DIFF_WIDTH), BF16),
        jax.ShapeDtypeStruct((DIFF_HEADS * VT_ROWS, T), BF16),
        jax.ShapeDtypeStruct((T, DIFF_WIDTH), BF16),
    )
    out_specs = [pl.BlockSpec((tm, s.shape[1]), row) for s in out_shapes]
    out_specs[7] = pl.BlockSpec((DIFF_HEADS * VT_ROWS, tm), col)
    return pl.pallas_call(
        _in_proj_kernel,
        grid=(T // tm,),
        in_specs=[
            pl.BlockSpec((tm, D), row),
            pl.BlockSpec((1, D), fixed),
            pl.BlockSpec(wa.shape, fixed),
            pl.BlockSpec(wa2.shape, fixed),
            pl.BlockSpec(ba2.shape, fixed),
            pl.BlockSpec(wb.shape, fixed),
            pl.BlockSpec(wvt.shape, fixed),
            pl.BlockSpec((tm, cs_t.shape[1]), row),
            pl.BlockSpec(sel.shape, fixed),
        ],
        out_specs=tuple(out_specs),
        out_shape=out_shapes,
        compiler_params=pltpu.CompilerParams(dimension_semantics=("parallel",),
                                             vmem_limit_bytes=VMEM_LIMIT),
        name="in_proj",
    )(x2d, norm_w, wa, wa2, ba2, wb, wvt, cs_t, sel)


def _gla_kernel(gq_ref, gk_ref, g_ref, gv_ref, ggs_ref, nw_ref, o_ref,
                qd4_s, ki_s, kl4_s, dec_s, state_s):
    tm = gq_ref.shape[0]
    n_chunks = tm // CHUNK

    @pl.when(pl.program_id(1) == 0)
    def _():
        state_s[...] = jnp.zeros_like(state_s)

    g = g_ref[...]
    row_in_chunk = lax.broadcasted_iota(jnp.int32, g.shape, 0) % CHUNK
    G = g
    shift = 1
    while shift < CHUNK:
        G = G + jnp.where(row_in_chunk >= shift, pltpu.roll(G, shift, 0), 0.0)
        shift *= 2
    q = gq_ref[...]
    k = gk_ref[...]
    qd = (q * jnp.exp(G)).astype(BF16)
    ki_s[...] = (k * jnp.exp(-G)).astype(BF16)

    lane_head = lax.broadcasted_iota(jnp.int32, (CHUNK, GLA_QK), 1) // GLA_DK
    zero = jnp.zeros((CHUNK, GLA_QK), BF16)
    for c in range(n_chunks):
        rows = slice(c * CHUNK, (c + 1) * CHUNK)
        g_last = G[(c + 1) * CHUNK - 1:(c + 1) * CHUNK, :]
        kl = (k[rows, :] * jnp.exp(g_last - G[rows, :])).astype(BF16)
        dec_s[c:c + 1, :] = jnp.exp(g_last)
        for hd in range(GLA_HEADS):
            dst = slice((c * GLA_HEADS + hd) * CHUNK, (c * GLA_HEADS + hd + 1) * CHUNK)
            qd4_s[dst, :] = jnp.where(lane_head == hd, qd[rows, :], zero)
            kl4_s[dst, :] = jnp.where(lane_head == hd, kl, zero)

    r_i = lax.broadcasted_iota(jnp.int32, (GLA_HEADS * CHUNK, GLA_HEADS * CHUNK), 0)
    c_i = lax.broadcasted_iota(jnp.int32, (GLA_HEADS * CHUNK, GLA_HEADS * CHUNK), 1)
    keep = (r_i // CHUNK == c_i // CHUNK) & (c_i % CHUNK <= r_i % CHUNK)
    nw = nw_ref[...]

    def chunk_rows4(c):
        return slice(c * GLA_HEADS * CHUNK, (c + 1) * GLA_HEADS * CHUNK)

    def chunk_values(c):
        rows = slice(c * CHUNK, (c + 1) * CHUNK)
        return jnp.concatenate([gv_ref[rows, hd * GLA_DV:(hd + 1) * GLA_DV]
                                for hd in range(GLA_HEADS)], axis=0)

    def state_free_products(c):
        ki = ki_s[c * CHUNK:(c + 1) * CHUNK, :]
        k4 = jnp.concatenate([ki] * GLA_HEADS, axis=0)
        scores = lax.dot_general(qd4_s[chunk_rows4(c), :], k4, NT_DIMS, preferred_element_type=F32)
        kv_t = lax.dot_general(chunk_values(c), kl4_s[chunk_rows4(c), :], TN_DIMS,
                               preferred_element_type=F32)
        return scores, kv_t

    ahead = state_free_products(0)
    for c in range(n_chunks):
        rows = slice(c * CHUNK, (c + 1) * CHUNK)
        scores, kv_t = ahead
        if c + 1 < n_chunks:
            ahead = state_free_products(c + 1)
        q4 = qd4_s[chunk_rows4(c), :]
        st = state_s[...]
        scores = jnp.where(keep, scores, 0.0).astype(BF16)
        o = (jnp.dot(scores, chunk_values(c), preferred_element_type=F32)
             + lax.dot_general(q4, st.astype(BF16), NT_DIMS, preferred_element_type=F32))
        state_s[...] = dec_s[c:c + 1, :] * st + kv_t
        gate = jnp.concatenate([ggs_ref[rows, hd * GLA_DV:(hd + 1) * GLA_DV]
                                for hd in range(GLA_HEADS)], axis=0).astype(F32)
        o = (_rms_scale(o, nw) * gate).astype(BF16)
        for hd in range(GLA_HEADS):
            o_ref[rows, hd * GLA_DV:(hd + 1) * GLA_DV] = o[hd * CHUNK:(hd + 1) * CHUNK, :]


def _gla(gq, gk, g, gv, ggs, norm_w, batch, seq):
    tm = TM_GLA
    nt = seq // tm
    row = lambda b, i: (b * nt + i, 0)
    fixed = lambda b, i: (0, 0)
    T = gq.shape[0]
    return pl.pallas_call(
        _gla_kernel,
        grid=(batch, nt),
        in_specs=[
            pl.BlockSpec((tm, GLA_QK), row),
            pl.BlockSpec((tm, GLA_QK), row),
            pl.BlockSpec((tm, GLA_QK), row),
            pl.BlockSpec((tm, GLA_WIDTH), row),
            pl.BlockSpec((tm, GLA_WIDTH), row),
            pl.BlockSpec((1, GLA_DV), fixed),
        ],
        out_specs=pl.BlockSpec((tm, GLA_WIDTH), row),
        out_shape=jax.ShapeDtypeStruct((T, GLA_WIDTH), BF16),
        scratch_shapes=[
            pltpu.VMEM((GLA_HEADS * tm, GLA_QK), BF16),
            pltpu.VMEM((tm, GLA_QK), BF16),
            pltpu.VMEM((GLA_HEADS * tm, GLA_QK), BF16),
            pltpu.VMEM((tm // CHUNK, GLA_QK), F32),
            pltpu.VMEM((GLA_DV, GLA_QK), F32),
        ],
        compiler_params=pltpu.CompilerParams(dimension_semantics=("parallel", "arbitrary"),
                                             vmem_limit_bytes=VMEM_LIMIT),
        name="gla",
    )(gq, gk, g, gv, ggs, norm_w)


def _diff_attn_kernel(lam_ref, q_ref, k_ref, vt_ref, gs_ref, nw_ref, o_ref,
                      qq_s, *, lam_init):
    seq = q_ref.shape[0]
    tk = TK_ATT
    n_blk = seq // tk
    cw = 2 * tk

    lv = lam_ref[...]
    lam = (jnp.exp(jnp.sum(lv[0:1] * lv[1:2], axis=-1, keepdims=True))
           - jnp.exp(jnp.sum(lv[2:3] * lv[3:4], axis=-1, keepdims=True)) + lam_init)
    nw = nw_ref[...]

    kc = lax.broadcasted_iota(jnp.int32, (tk, cw), 0) // CHUNK
    qc = (lax.broadcasted_iota(jnp.int32, (tk, cw), 1) % tk) // CHUNK
    stair_bias = jnp.where(kc <= qc, 0.0, -jnp.inf).astype(F32)

    lane = lax.broadcasted_iota(jnp.int32, (tk, LANES), 1)
    for r in range(n_blk):
        q = q_ref[r * tk:(r + 1) * tk, :]
        zero = jnp.zeros_like(q)
        qq_s[r * cw:r * cw + tk, :] = jnp.where(lane < DIFF_DQK, q, zero)
        qq_s[r * cw + tk:(r + 1) * cw, :] = jnp.where(lane >= DIFF_DQK, q, zero)

    def scores(cb):
        nk = (cb + 1) * tk
        st = lax.dot_general(k_ref[0:nk, :], qq_s[cb * cw:(cb + 1) * cw, :], NT_DIMS,
                             preferred_element_type=F32)
        parts = [st[j * tk:(j + 1) * tk, :] for j in range(cb)] + [st[cb * tk:, :] + stair_bias]
        m = jnp.max(parts[0], axis=0, keepdims=True)
        for part in parts[1:]:
            m = jnp.maximum(m, jnp.max(part, axis=0, keepdims=True))
        return parts, m

    def finish(cb, parts, m):
        nk = (cb + 1) * tk
        pt = jnp.concatenate([jnp.exp2(part - m).astype(BF16) for part in parts], axis=0)
        pv = jnp.dot(vt_ref[:, 0:nk], pt, preferred_element_type=F32)
        o0t = pv[0:DIFF_DV, 0:tk] / pv[DIFF_DV:DIFF_DV + 1, 0:tk]
        o1t = pv[0:DIFF_DV, tk:] / pv[DIFF_DV:DIFF_DV + 1, tk:]
        o = (o0t - lam * o1t).T
        rows = slice(cb * tk, (cb + 1) * tk)
        o = _rms_scale(o, nw) * (1.0 - lam_init) * gs_ref[rows, :].astype(F32)
        o_ref[rows, :] = o.astype(BF16)

    pending = [scores(cb) for cb in range(SCORES_AHEAD)]
    for cb in range(n_blk):
        if cb + SCORES_AHEAD < n_blk:
            pending.append(scores(cb + SCORES_AHEAD))
        finish(cb, *pending.pop(0))


def _diff_attn(lam_vecs, dq, dk, dvt, dgs, norm_w, batch, seq, lam_init):
    assert TK_ATT % CHUNK == 0 and seq % TK_ATT == 0
    T = dq.shape[0]
    bh = lambda b, h: (b, h)
    hb = lambda b, h: (h, b)
    fixed = lambda b, h: (0, 0)
    blk = pl.BlockSpec((seq, DIFF_DV), bh)
    return pl.pallas_call(
        functools.partial(_diff_attn_kernel, lam_init=lam_init),
        grid=(batch, DIFF_HEADS),
        in_specs=[pl.BlockSpec(lam_vecs.shape, fixed), blk, blk,
                  pl.BlockSpec((VT_ROWS, seq), hb), blk,
                  pl.BlockSpec((1, DIFF_DV), fixed)],
        out_specs=blk,
        out_shape=jax.ShapeDtypeStruct((T, DIFF_WIDTH), BF16),
        scratch_shapes=[
            pltpu.VMEM((2 * seq, DIFF_DV), BF16),
        ],
        compiler_params=pltpu.CompilerParams(dimension_semantics=("parallel", "parallel"),
                                             vmem_limit_bytes=VMEM_LIMIT),
        name="diff_attn",
    )(lam_vecs, dq, dk, dvt, dgs, norm_w)


def _mem_kv_kernel(mem_ref, nw_ref, w_ref, k_ref, v_ref):
    d = mem_ref.shape[1]
    mn = _rms_scale(mem_ref[...], nw_ref[...]).astype(BF16)
    k_ref[...] = jnp.dot(mn, w_ref[:, :d], preferred_element_type=F32).astype(BF16)
    v_ref[...] = jnp.dot(mn, w_ref[:, d:], preferred_element_type=F32).astype(BF16)


def _mem_kv(mem2d, norm_w, w_ckv, batch, mem_len):
    D = mem2d.shape[1]
    row = lambda b: (b, 0)
    fixed = lambda b: (0, 0)
    shp = jax.ShapeDtypeStruct((batch * mem_len, D), BF16)
    return pl.pallas_call(
        _mem_kv_kernel,
        grid=(batch,),
        in_specs=[pl.BlockSpec((mem_len, D), row), pl.BlockSpec((1, D), fixed),
                  pl.BlockSpec(w_ckv.shape, fixed)],
        out_specs=(pl.BlockSpec((mem_len, D), row), pl.BlockSpec((mem_len, D), row)),
        out_shape=(shp, shp),
        compiler_params=pltpu.CompilerParams(dimension_semantics=("parallel",),
                                             vmem_limit_bytes=VMEM_LIMIT),
        name="mem_kv",
    )(mem2d, norm_w, w_ckv)


def _out_cross_kernel(x_ref, og_ref, od_ref, wo_ref, ncw_ref, wq_ref, k_ref, v_ref, wco_ref,
                      fnw_ref, o_ref, ctx_s, *, final_norm):
    d = x_ref.shape[1]
    dh = d // CROSS_HEADS
    pm = x_ref.shape[0] // ROW_PARTS
    parts = [slice(p * pm, (p + 1) * pm) for p in range(ROW_PARTS)]
    x1 = [x_ref[r, :]
          + jnp.dot(og_ref[r, :], wo_ref[0:GLA_WIDTH, :], preferred_element_type=F32)
          + jnp.dot(od_ref[r, :], wo_ref[GLA_WIDTH:, :], preferred_element_type=F32)
          for r in parts]
    q = [(jnp.dot(_rms_scale(x, ncw_ref[...]).astype(BF16), wq_ref[...],
                  preferred_element_type=F32) * (dh ** -0.5)).astype(BF16) for x in x1]
    def head_scores(hd, part):
        sl = slice(hd * dh, (hd + 1) * dh)
        return lax.dot_general(q[part][:, sl], k_ref[:, sl], NT_DIMS,
                               preferred_element_type=F32)

    items = [(hd, part) for hd in range(CROSS_HEADS) for part in range(ROW_PARTS)]
    s_next = head_scores(*items[0])
    for n, (hd, part) in enumerate(items):
        s = s_next
        if n + 1 < len(items):
            s_next = head_scores(*items[n + 1])
        sl = slice(hd * dh, (hd + 1) * dh)
        p = jnp.exp(s - jnp.max(s, axis=-1, keepdims=True))
        p = p / jnp.sum(p, axis=-1, keepdims=True)
        ctx_s[parts[part], sl] = jnp.dot(p.astype(BF16), v_ref[:, sl],
                                         preferred_element_type=F32).astype(BF16)
    for r, x in zip(parts, x1):
        x2 = x + jnp.dot(ctx_s[r, :], wco_ref[...], preferred_element_type=F32)
        if final_norm:
            x2 = _rms_scale(x2, fnw_ref[...])
        o_ref[r, :] = x2


def _out_cross(x2d, og, od, w_out, ncw, w_cq, kc, vc, w_co, fnw, seq, mem_len, final_norm):
    T, D = x2d.shape
    tm = TM_OUT
    per_b = seq // tm
    row = lambda i: (i, 0)
    fixed = lambda i: (0, 0)
    bmap = lambda i: (i // per_b, 0)
    return pl.pallas_call(
        functools.partial(_out_cross_kernel, final_norm=final_norm),
        grid=(T // tm,),
        in_specs=[
            pl.BlockSpec((tm, D), row),
            pl.BlockSpec((tm, GLA_WIDTH), row),
            pl.BlockSpec((tm, DIFF_WIDTH), row),
            pl.BlockSpec(w_out.shape, fixed),
            pl.BlockSpec((1, D), fixed),
            pl.BlockSpec(w_cq.shape, fixed),
            pl.BlockSpec((mem_len, D), bmap),
            pl.BlockSpec((mem_len, D), bmap),
            pl.BlockSpec(w_co.shape, fixed),
            pl.BlockSpec((1, D), fixed),
        ],
        out_specs=pl.BlockSpec((tm, D), row),
        out_shape=jax.ShapeDtypeStruct((T, D), F32),
        scratch_shapes=[pltpu.VMEM((tm, D), BF16)],
        compiler_params=pltpu.CompilerParams(dimension_semantics=("parallel",),
                                             vmem_limit_bytes=VMEM_LIMIT),
        name="out_cross",
    )(x2d, og, od, w_out, ncw, w_cq, kc, vc, w_co, fnw)


def _rope_tables(positions):
    half = ROPE_DIM // 2
    T = positions.size
    per_row = LANES // half
    inv_freq = ROPE_THETA ** (-(jnp.arange(0, ROPE_DIM, 2, dtype=F32) / ROPE_DIM))
    pos = positions.reshape(T // per_row, per_row).astype(F32)
    ang = jnp.repeat(pos, half, axis=1) * jnp.tile(inv_freq, per_row)
    pieces = []
    rest_c, rest_s = jnp.cos(ang), jnp.sin(ang)
    for _ in range(ROPE_SPLIT):
        pc, ps = rest_c.astype(BF16), rest_s.astype(BF16)
        pieces += [pc.reshape(T, half), ps.reshape(T, half)]
        rest_c, rest_s = rest_c - pc.astype(F32), rest_s - ps.astype(F32)
    return jnp.concatenate(pieces, axis=1)


def _rope_select_matrix():
    half = ROPE_DIM // 2
    sel = np.zeros((ROPE_DIM, 2 * LANES), np.float32)
    for j in range(LANES):
        jm = j % DIFF_DQK
        if jm < ROPE_DIM:
            sel[jm % half, j] = 1.0
            sel[half + jm % half, LANES + j] = -1.0 if jm < half else 1.0
    return jnp.asarray(np.tile(sel, (ROPE_SPLIT, 1)), BF16)


def kernel(x, mem, positions, norm_mix_w, w_in, w_alpha2, b_alpha2, gla_norm_w,
           lam_q1, lam_k1, lam_q2, lam_k2, diff_norm_w, w_out,
           norm_cross_w, norm_mem_w, w_cq, w_ckv, w_co, final_norm_w):
    B, S, D = x.shape
    M = mem.shape[1]
    depth = w_in.shape[0]
    T = B * S
    cs_t = _rope_tables(positions)
    sel = _rope_select_matrix()
    x2d = x.reshape(T, D)
    mem2d = mem.reshape(B * M, D)

    c_gv = 2 * GLA_QK
    c_ga = c_gv + 2 * GLA_WIDTH
    c_dq = c_ga + GLA_LOWRANK

    for l in range(depth):
        wl = w_in[l]
        ga_pad = jnp.pad(wl[:, c_ga:c_dq], ((0, 0), (0, LANES - GLA_LOWRANK)))
        wa = jnp.concatenate([wl[:, :c_gv], ga_pad], axis=1).astype(BF16)
        c_dv = c_dq + 2 * DIFF_WIDTH
        c_dg = c_dv + DIFF_WIDTH
        wb = jnp.concatenate([wl[:, c_gv:c_ga], wl[:, c_dq:c_dv], wl[:, c_dg:]],
                             axis=1).astype(BF16)
        wvt = wl[:, c_dv:c_dg].T.astype(BF16)
        wa2 = jnp.pad(w_alpha2[l], ((0, LANES - GLA_LOWRANK), (0, 0))).astype(BF16)
        ba2 = b_alpha2[l].reshape(1, GLA_QK).astype(F32)

        gq, gk, gv, ggs, g, dq, dk, dvt, dgs = _in_proj(
            x2d, norm_mix_w[l].reshape(1, D), wa, wa2, ba2, wb, wvt, cs_t, sel)

        o_gla = _gla(gq, gk, g, gv, ggs, gla_norm_w[l].reshape(1, GLA_DV), B, S)

        lam_init = 0.8 - 0.6 * math.exp(-0.3 * l)
        lam_vecs = jnp.stack([lam_q1[l], lam_k1[l], lam_q2[l], lam_k2[l]]).astype(F32)
        o_diff = _diff_attn(lam_vecs, dq, dk, dvt, dgs, diff_norm_w[l].reshape(1, DIFF_DV),
                            B, S, lam_init)

        kc, vc = _mem_kv(mem2d, norm_mem_w[l].reshape(1, D), w_ckv[l].astype(BF16), B, M)

        x2d = _out_cross(x2d, o_gla, o_diff, w_out[l].astype(BF16),
                         norm_cross_w[l].reshape(1, D), w_cq[l].astype(BF16), kc, vc,
                         w_co[l].astype(BF16), final_norm_w.reshape(1, D), S, M,
                         final_norm=(l == depth - 1))
    return x2d.reshape(B, S, D)
```

```python
import functools
import math

import jax
import jax.numpy as jnp
import numpy as np
from jax import lax
from jax.experimental import pallas as pl
from jax.experimental.pallas import tpu as pltpu

F32 = jnp.float32
BF16 = jnp.bfloat16

CHUNK = 64
GLA_HEADS = 4
GLA_DK = 64
GLA_DV = 128
GLA_QK = GLA_HEADS * GLA_DK
GLA_WIDTH = GLA_HEADS * GLA_DV
GLA_LOWRANK = 16
GLA_TAU = 16.0
DIFF_HEADS = 4
DIFF_DQK = 64
DIFF_DV = 128
DIFF_WIDTH = DIFF_HEADS * DIFF_DV
ROPE_DIM = 16
ROPE_THETA = 500000.0
CROSS_HEADS = 4
EPS = 1e-6
LOG2E = math.log2(math.e)
ROPE_SPLIT = 3

LANES = 128

TM_PROJ = 1024
TM_GLA = 1024
TK_ATT = 256
SUM_ROWS = 16
VT_ROWS = DIFF_DV + SUM_ROWS
SCORES_AHEAD = 2
TM_OUT = 1024
ROW_PARTS = 2
VMEM_LIMIT = 56 * 1024 * 1024

NT_DIMS = (((1,), (1,)), ((), ()))
TN_DIMS = (((0,), (0,)), ((), ()))


def _rms_scale(xf, w):
    return xf * lax.rsqrt(jnp.mean(xf * xf, axis=-1, keepdims=True) + EPS) * w


def _silu(t):
    return t * (1.0 / (1.0 + jnp.exp(-t)))


def _in_proj_kernel(x_ref, nw_ref, wa_ref, wa2_ref, ba2_ref, wb_ref, wvt_ref, cs_ref, sel_ref,
                    gq_ref, gk_ref, gv_ref, ggs_ref, g_ref, dq_ref, dk_ref, dvt_ref, dgs_ref):
    pm = x_ref.shape[0] // ROW_PARTS
    for part in range(ROW_PARTS):
        _in_proj_rows(slice(part * pm, (part + 1) * pm),
                      x_ref, nw_ref, wa_ref, wa2_ref, ba2_ref, wb_ref, wvt_ref, cs_ref, sel_ref,
                      gq_ref, gk_ref, gv_ref, ggs_ref, g_ref, dq_ref, dk_ref, dvt_ref, dgs_ref)


def _in_proj_rows(rows, x_ref, nw_ref, wa_ref, wa2_ref, ba2_ref, wb_ref, wvt_ref, cs_ref, sel_ref,
                  gq_ref, gk_ref, gv_ref, ggs_ref, g_ref, dq_ref, dk_ref, dvt_ref, dgs_ref):
    h = _rms_scale(x_ref[rows, :], nw_ref[...]).astype(BF16)

    def proj_b(group):
        return jnp.dot(h, wb_ref[:, group * GLA_WIDTH:(group + 1) * GLA_WIDTH],
                       preferred_element_type=F32)

    qkg = jnp.dot(h, wa_ref[...], preferred_element_type=F32)
    gq_ref[rows, :] = qkg[:, 0:GLA_QK] * (GLA_DK ** -0.5)
    gk_ref[rows, :] = qkg[:, GLA_QK:2 * GLA_QK]
    ga = qkg[:, 2 * GLA_QK:].astype(BF16)

    gv_ref[rows, :] = proj_b(0).astype(BF16)
    ggs_ref[rows, :] = _silu(proj_b(1)).astype(BF16)

    tabs = jnp.dot(cs_ref[rows, :], sel_ref[...], preferred_element_type=F32)
    lane = lax.broadcasted_iota(jnp.int32, (tabs.shape[0], LANES), 1)
    cos = tabs[:, :LANES] + jnp.where((lane % DIFF_DQK) < ROPE_DIM, 0.0, 1.0)
    sin = tabs[:, LANES:]
    first_half = (lane % DIFF_DQK) < (ROPE_DIM // 2)

    def rope(t):
        partner = jnp.where(first_half, pltpu.roll(t, LANES - ROPE_DIM // 2, 1),
                            pltpu.roll(t, ROPE_DIM // 2, 1))
        return t * cos + partner * sin

    q = proj_b(2)
    k = proj_b(3)
    for hd in range(DIFF_HEADS):
        sl = slice(hd * DIFF_DV, (hd + 1) * DIFF_DV)
        dq_ref[rows, sl] = (rope(q[:, sl]) * (DIFF_DQK ** -0.5 * LOG2E)).astype(BF16)
        dk_ref[rows, sl] = rope(k[:, sl]).astype(BF16)
    dvt = lax.dot_general(wvt_ref[...], h, NT_DIMS, preferred_element_type=F32).astype(BF16)
    for hd in range(DIFF_HEADS):
        base = hd * VT_ROWS
        dvt_ref[base:base + DIFF_DV, rows] = dvt[hd * DIFF_DV:(hd + 1) * DIFF_DV, :]
        dvt_ref[base + DIFF_DV:base + VT_ROWS, rows] = jnp.ones((SUM_ROWS, dvt.shape[1]), BF16)

    logit = jnp.dot(ga, wa2_ref[...], preferred_element_type=F32) + ba2_ref[...]
    log_sig = jnp.minimum(logit, 0.0) - jnp.log1p(jnp.exp(-jnp.abs(logit)))
    g_ref[rows, :] = log_sig * (1.0 / GLA_TAU)

    dgs_ref[rows, :] = _silu(proj_b(4)).astype(BF16)


def _in_proj(x2d, norm_w, wa, wa2, ba2, wb, wvt, cs_t, sel):
    T, D = x2d.shape
    tm = TM_PROJ
    row = lambda i: (i, 0)
    col = lambda i: (0, i)
    fixed = lambda i: (0, 0)
    out_shapes = (
        jax.ShapeDtypeStruct((T, GLA_QK), F32),
        jax.ShapeDtypeStruct((T, GLA_QK), F32),
        jax.ShapeDtypeStruct((T, GLA_WIDTH), BF16),
        jax.ShapeDtypeStruct((T, GLA_WIDTH), BF16),
        jax.ShapeDtypeStruct((T, GLA_QK), F32),
        jax.ShapeDtypeStruct((T, DIFF_WIDTH), BF16),
        jax.ShapeDtypeStruct((T, DIFF_WIDTH), BF16),
        jax.ShapeDtypeStruct((DIFF_HEADS * VT_ROWS, T), BF16),
        jax.ShapeDtypeStruct((T, DIFF_WIDTH), BF16),
    )
    out_specs = [pl.BlockSpec((tm, s.shape[1]), row) for s in out_shapes]
    out_specs[7] = pl.BlockSpec((DIFF_HEADS * VT_ROWS, tm), col)
    return pl.pallas_call(
        _in_proj_kernel,
        grid=(T // tm,),
        in_specs=[
            pl.BlockSpec((tm, D), row),
            pl.BlockSpec((1, D), fixed),
            pl.BlockSpec(wa.shape, fixed),
            pl.BlockSpec(wa2.shape, fixed),
            pl.BlockSpec(ba2.shape, fixed),
            pl.BlockSpec(wb.shape, fixed),
            pl.BlockSpec(wvt.shape, fixed),
            pl.BlockSpec((tm, cs_t.shape[1]), row),
            pl.BlockSpec(sel.shape, fixed),
        ],
        out_specs=tuple(out_specs),
        out_shape=out_shapes,
        compiler_params=pltpu.CompilerParams(dimension_semantics=("parallel",),
                                             vmem_limit_bytes=VMEM_LIMIT),
        name="in_proj",
    )(x2d, norm_w, wa, wa2, ba2, wb, wvt, cs_t, sel)


def _gla_kernel(gq_ref, gk_ref, g_ref, gv_ref, ggs_ref, nw_ref, o_ref,
                qd4_s, ki_s, kl4_s, dec_s, state_s):
    tm = gq_ref.shape[0]
    n_chunks = tm // CHUNK

    @pl.when(pl.program_id(1) == 0)
    def _():
        state_s[...] = jnp.zeros_like(state_s)

    g = g_ref[...]
    row_in_chunk = lax.broadcasted_iota(jnp.int32, g.shape, 0) % CHUNK
    G = g
    shift = 1
    while shift < CHUNK:
        G = G + jnp.where(row_in_chunk >= shift, pltpu.roll(G, shift, 0), 0.0)
        shift *= 2
    q = gq_ref[...]
    k = gk_ref[...]
    qd = (q * jnp.exp(G)).astype(BF16)
    ki_s[...] = (k * jnp.exp(-G)).astype(BF16)

    lane_head = lax.broadcasted_iota(jnp.int32, (CHUNK, GLA_QK), 1) // GLA_DK
    zero = jnp.zeros((CHUNK, GLA_QK), BF16)
    for c in range(n_chunks):
        rows = slice(c * CHUNK, (c + 1) * CHUNK)
        g_last = G[(c + 1) * CHUNK - 1:(c + 1) * CHUNK, :]
        kl = (k[rows, :] * jnp.exp(g_last - G[rows, :])).astype(BF16)
        dec_s[c:c + 1, :] = jnp.exp(g_last)
        for hd in range(GLA_HEADS):
            dst = slice((c * GLA_HEADS + hd) * CHUNK, (c * GLA_HEADS + hd + 1) * CHUNK)
            qd4_s[dst, :] = jnp.where(lane_head == hd, qd[rows, :], zero)
            kl4_s[dst, :] = jnp.where(lane_head == hd, kl, zero)

    r_i = lax.broadcasted_iota(jnp.int32, (GLA_HEADS * CHUNK, GLA_HEADS * CHUNK), 0)
    c_i = lax.broadcasted_iota(jnp.int32, (GLA_HEADS * CHUNK, GLA_HEADS * CHUNK), 1)
    keep = (r_i // CHUNK == c_i // CHUNK) & (c_i % CHUNK <= r_i % CHUNK)
    nw = nw_ref[...]

    def chunk_rows4(c):
        return slice(c * GLA_HEADS * CHUNK, (c + 1) * GLA_HEADS * CHUNK)

    def chunk_values(c):
        rows = slice(c * CHUNK, (c + 1) * CHUNK)
        return jnp.concatenate([gv_ref[rows, hd * GLA_DV:(hd + 1) * GLA_DV]
                                for hd in range(GLA_HEADS)], axis=0)

    def state_free_products(c):
        ki = ki_s[c * CHUNK:(c + 1) * CHUNK, :]
        k4 = jnp.concatenate([ki] * GLA_HEADS, axis=0)
        scores = lax.dot_general(qd4_s[chunk_rows4(c), :], k4, NT_DIMS, preferred_element_type=F32)
        kv_t = lax.dot_general(chunk_values(c), kl4_s[chunk_rows4(c), :], TN_DIMS,
                               preferred_element_type=F32)
        return scores, kv_t

    ahead = state_free_products(0)
    for c in range(n_chunks):
        rows = slice(c * CHUNK, (c + 1) * CHUNK)
        scores, kv_t = ahead
        if c + 1 < n_chunks:
            ahead = state_free_products(c + 1)
        q4 = qd4_s[chunk_rows4(c), :]
        st = state_s[...]
        scores = jnp.where(keep, scores, 0.0).astype(BF16)
        o = (jnp.dot(scores, chunk_values(c), preferred_element_type=F32)
             + lax.dot_general(q4, st.astype(BF16), NT_DIMS, preferred_element_type=F32))
        state_s[...] = dec_s[c:c + 1, :] * st + kv_t
        gate = jnp.concatenate([ggs_ref[rows, hd * GLA_DV:(hd + 1) * GLA_DV]
                                for hd in range(GLA_HEADS)], axis=0).astype(F32)
        o = (_rms_scale(o, nw) * gate).astype(BF16)
        for hd in range(GLA_HEADS):
            o_ref[rows, hd * GLA_DV:(hd + 1) * GLA_DV] = o[hd * CHUNK:(hd + 1) * CHUNK, :]


def _gla(gq, gk, g, gv, ggs, norm_w, batch, seq):
    tm = TM_GLA
    nt = seq // tm
    row = lambda b, i: (b * nt + i, 0)
    fixed = lambda b, i: (0, 0)
    T = gq.shape[0]
    return pl.pallas_call(
        _gla_kernel,
        grid=(batch, nt),
        in_specs=[
            pl.BlockSpec((tm, GLA_QK), row),
            pl.BlockSpec((tm, GLA_QK), row),
            pl.BlockSpec((tm, GLA_QK), row),
            pl.BlockSpec((tm, GLA_WIDTH), row),
            pl.BlockSpec((tm, GLA_WIDTH), row),
            pl.BlockSpec((1, GLA_DV), fixed),
        ],
        out_specs=pl.BlockSpec((tm, GLA_WIDTH), row),
        out_shape=jax.ShapeDtypeStruct((T, GLA_WIDTH), BF16),
        scratch_shapes=[
            pltpu.VMEM((GLA_HEADS * tm, GLA_QK), BF16),
            pltpu.VMEM((tm, GLA_QK), BF16),
            pltpu.VMEM((GLA_HEADS * tm, GLA_QK), BF16),
            pltpu.VMEM((tm // CHUNK, GLA_QK), F32),
            pltpu.VMEM((GLA_DV, GLA_QK), F32),
        ],
        compiler_params=pltpu.CompilerParams(dimension_semantics=("parallel", "arbitrary"),
                                             vmem_limit_bytes=VMEM_LIMIT),
        name="gla",
    )(gq, gk, g, gv, ggs, norm_w)


def _diff_attn_kernel(lam_ref, q_ref, k_ref, vt_ref, gs_ref, nw_ref, o_ref,
                      qq_s, *, lam_init):
    seq = q_ref.shape[0]
    tk = TK_ATT
    n_blk = seq // tk
    cw = 2 * tk

    lv = lam_ref[...]
    lam = (jnp.exp(jnp.sum(lv[0:1] * lv[1:2], axis=-1, keepdims=True))
           - jnp.exp(jnp.sum(lv[2:3] * lv[3:4], axis=-1, keepdims=True)) + lam_init)
    nw = nw_ref[...]

    kc = lax.broadcasted_iota(jnp.int32, (tk, cw), 0) // CHUNK
    qc = (lax.broadcasted_iota(jnp.int32, (tk, cw), 1) % tk) // CHUNK
    stair_bias = jnp.where(kc <= qc, 0.0, -jnp.inf).astype(F32)

    lane = lax.broadcasted_iota(jnp.int32, (tk, LANES), 1)
    for r in range(n_blk):
        q = q_ref[r * tk:(r + 1) * tk, :]
        zero = jnp.zeros_like(q)
        qq_s[r * cw:r * cw + tk, :] = jnp.where(lane < DIFF_DQK, q, zero)
        qq_s[r * cw + tk:(r + 1) * cw, :] = jnp.where(lane >= DIFF_DQK, q, zero)

    def scores(cb):
        nk = (cb + 1) * tk
        st = lax.dot_general(k_ref[0:nk, :], qq_s[cb * cw:(cb + 1) * cw, :], NT_DIMS,
                             preferred_element_type=F32)
        parts = [st[j * tk:(j + 1) * tk, :] for j in range(cb)] + [st[cb * tk:, :] + stair_bias]
        m = jnp.max(parts[0], axis=0, keepdims=True)
        for part in parts[1:]:
            m = jnp.maximum(m, jnp.max(part, axis=0, keepdims=True))
        return parts, m

    def finish(cb, parts, m):
        nk = (cb + 1) * tk
        pt = jnp.concatenate([jnp.exp2(part - m).astype(BF16) for part in parts], axis=0)
        pv = jnp.dot(vt_ref[:, 0:nk], pt, preferred_element_type=F32)
        o0t = pv[0:DIFF_DV, 0:tk] / pv[DIFF_DV:DIFF_DV + 1, 0:tk]
        o1t = pv[0:DIFF_DV, tk:] / pv[DIFF_DV:DIFF_DV + 1, tk:]
        o = (o0t - lam * o1t).T
        rows = slice(cb * tk, (cb + 1) * tk)
        o = _rms_scale(o, nw) * (1.0 - lam_init) * gs_ref[rows, :].astype(F32)
        o_ref[rows, :] = o.astype(BF16)

    pending = [scores(cb) for cb in range(SCORES_AHEAD)]
    for cb in range(n_blk):
        if cb + SCORES_AHEAD < n_blk:
            pending.append(scores(cb + SCORES_AHEAD))
        finish(cb, *pending.pop(0))


def _diff_attn(lam_vecs, dq, dk, dvt, dgs, norm_w, batch, seq, lam_init):
    assert TK_ATT % CHUNK == 0 and seq % TK_ATT == 0
    T = dq.shape[0]
    bh = lambda b, h: (b, h)
    hb = lambda b, h: (h, b)
    fixed = lambda b, h: (0, 0)
    blk = pl.BlockSpec((seq, DIFF_DV), bh)
    return pl.pallas_call(
        functools.partial(_diff_attn_kernel, lam_init=lam_init),
        grid=(batch, DIFF_HEADS),
        in_specs=[pl.BlockSpec(lam_vecs.shape, fixed), blk, blk,
                  pl.BlockSpec((VT_ROWS, seq), hb), blk,
                  pl.BlockSpec((1, DIFF_DV), fixed)],
        out_specs=blk,
        out_shape=jax.ShapeDtypeStruct((T, DIFF_WIDTH), BF16),
        scratch_shapes=[
            pltpu.VMEM((2 * seq, DIFF_DV), BF16),
        ],
        compiler_params=pltpu.CompilerParams(dimension_semantics=("parallel", "parallel"),
                                             vmem_limit_bytes=VMEM_LIMIT),
        name="diff_attn",
    )(lam_vecs, dq, dk, dvt, dgs, norm_w)


def _mem_kv_kernel(mem_ref, nw_ref, w_ref, k_ref, v_ref):
    d = mem_ref.shape[1]
    mn = _rms_scale(mem_ref[...], nw_ref[...]).astype(BF16)
    k_ref[...] = jnp.dot(mn, w_ref[:, :d], preferred_element_type=F32).astype(BF16)
    v_ref[...] = jnp.dot(mn, w_ref[:, d:], preferred_element_type=F32).astype(BF16)


def _mem_kv(mem2d, norm_w, w_ckv, batch, mem_len):
    D = mem2d.shape[1]
    row = lambda b: (b, 0)
    fixed = lambda b: (0, 0)
    shp = jax.ShapeDtypeStruct((batch * mem_len, D), BF16)
    return pl.pallas_call(
        _mem_kv_kernel,
        grid=(batch,),
        in_specs=[pl.BlockSpec((mem_len, D), row), pl.BlockSpec((1, D), fixed),
                  pl.BlockSpec(w_ckv.shape, fixed)],
        out_specs=(pl.BlockSpec((mem_len, D), row), pl.BlockSpec((mem_len, D), row)),
        out_shape=(shp, shp),
        compiler_params=pltpu.CompilerParams(dimension_semantics=("parallel",),
                                             vmem_limit_bytes=VMEM_LIMIT),
        name="mem_kv",
    )(mem2d, norm_w, w_ckv)


def _out_cross_kernel(x_ref, og_ref, od_ref, wo_ref, ncw_ref, wq_ref, k_ref, v_ref, wco_ref,
                      fnw_ref, o_ref, ctx_s, *, final_norm):
    d = x_ref.shape[1]
    dh = d // CROSS_HEADS
    pm = x_ref.shape[0] // ROW_PARTS
    parts = [slice(p * pm, (p + 1) * pm) for p in range(ROW_PARTS)]
    x1 = [x_ref[r, :]
          + jnp.dot(og_ref[r, :], wo_ref[0:GLA_WIDTH, :], preferred_element_type=F32)
          + jnp.dot(od_ref[r, :], wo_ref[GLA_WIDTH:, :], preferred_element_type=F32)
          for r in parts]
    q = [(jnp.dot(_rms_scale(x, ncw_ref[...]).astype(BF16), wq_ref[...],
                  preferred_element_type=F32) * (dh ** -0.5)).astype(BF16) for x in x1]
    def head_scores(hd, part):
        sl = slice(hd * dh, (hd + 1) * dh)
        return lax.dot_general(q[part][:, sl], k_ref[:, sl], NT_DIMS,
                               preferred_element_type=F32)

    items = [(hd, part) for hd in range(CROSS_HEADS) for part in range(ROW_PARTS)]
    s_next = head_scores(*items[0])
    for n, (hd, part) in enumerate(items):
        s = s_next
        if n + 1 < len(items):
            s_next = head_scores(*items[n + 1])
        sl = slice(hd * dh, (hd + 1) * dh)
        p = jnp.exp(s - jnp.max(s, axis=-1, keepdims=True))
        p = p / jnp.sum(p, axis=-1, keepdims=True)
        ctx_s[parts[part], sl] = jnp.dot(p.astype(BF16), v_ref[:, sl],
                                         preferred_element_type=F32).astype(BF16)
    for r, x in zip(parts, x1):
        x2 = x + jnp.dot(ctx_s[r, :], wco_ref[...], preferred_element_type=F32)
        if final_norm:
            x2 = _rms_scale(x2, fnw_ref[...])
        o_ref[r, :] = x2


def _out_cross(x2d, og, od, w_out, ncw, w_cq, kc, vc, w_co, fnw, seq, mem_len, final_norm):
    T, D = x2d.shape
    tm = TM_OUT
    per_b = seq // tm
    row = lambda i: (i, 0)
    fixed = lambda i: (0, 0)
    bmap = lambda i: (i // per_b, 0)
    return pl.pallas_call(
        functools.partial(_out_cross_kernel, final_norm=final_norm),
        grid=(T // tm,),
        in_specs=[
            pl.BlockSpec((tm, D), row),
            pl.BlockSpec((tm, GLA_WIDTH), row),
            pl.BlockSpec((tm, DIFF_WIDTH), row),
            pl.BlockSpec(w_out.shape, fixed),
            pl.BlockSpec((1, D), fixed),
            pl.BlockSpec(w_cq.shape, fixed),
            pl.BlockSpec((mem_len, D), bmap),
            pl.BlockSpec((mem_len, D), bmap),
            pl.BlockSpec(w_co.shape, fixed),
            pl.BlockSpec((1, D), fixed),
        ],
        out_specs=pl.BlockSpec((tm, D), row),
        out_shape=jax.ShapeDtypeStruct((T, D), F32),
        scratch_shapes=[pltpu.VMEM((tm, D), BF16)],
        compiler_params=pltpu.CompilerParams(dimension_semantics=("parallel",),
                                             vmem_limit_bytes=VMEM_LIMIT),
        name="out_cross",
    )(x2d, og, od, w_out, ncw, w_cq, kc, vc, w_co, fnw)


def _rope_tables(positions):
    half = ROPE_DIM // 2
    T = positions.size
    per_row = LANES // half
    inv_freq = ROPE_THETA ** (-(jnp.arange(0, ROPE_DIM, 2, dtype=F32) / ROPE_DIM))
    pos = positions.reshape(T // per_row, per_row).astype(F32)
    ang = jnp.repeat(pos, half, axis=1) * jnp.tile(inv_freq, per_row)
    pieces = []
    rest_c, rest_s = jnp.cos(ang), jnp.sin(ang)
    for _ in range(ROPE_SPLIT):
        pc, ps = rest_c.astype(BF16), rest_s.astype(BF16)
        pieces += [pc.reshape(T, half), ps.reshape(T, half)]
        rest_c, rest_s = rest_c - pc.astype(F32), rest_s - ps.astype(F32)
    return jnp.concatenate(pieces, axis=1)


def _rope_select_matrix():
    half = ROPE_DIM // 2
    sel = np.zeros((ROPE_DIM, 2 * LANES), np.float32)
    for j in range(LANES):
        jm = j % DIFF_DQK
        if jm < ROPE_DIM:
            sel[jm % half, j] = 1.0
            sel[half + jm % half, LANES + j] = -1.0 if jm < half else 1.0
    return jnp.asarray(np.tile(sel, (ROPE_SPLIT, 1)), BF16)


def kernel(x, mem, positions, norm_mix_w, w_in, w_alpha2, b_alpha2, gla_norm_w,
           lam_q1, lam_k1, lam_q2, lam_k2, diff_norm_w, w_out,
           norm_cross_w, norm_mem_w, w_cq, w_ckv, w_co, final_norm_w):
    B, S, D = x.shape
    M = mem.shape[1]
    depth = w_in.shape[0]
    T = B * S
    cs_t = _rope_tables(positions)
    sel = _rope_select_matrix()
    x2d = x.reshape(T, D)
    mem2d = mem.reshape(B * M, D)

    c_gv = 2 * GLA_QK
    c_ga = c_gv + 2 * GLA_WIDTH
    c_dq = c_ga + GLA_LOWRANK

    for l in range(depth):
        wl = w_in[l]
        ga_pad = jnp.pad(wl[:, c_ga:c_dq], ((0, 0), (0, LANES - GLA_LOWRANK)))
        wa = jnp.concatenate([wl[:, :c_gv], ga_pad], axis=1).astype(BF16)
        c_dv = c_dq + 2 * DIFF_WIDTH
        c_dg = c_dv + DIFF_WIDTH
        wb = jnp.concatenate([wl[:, c_gv:c_ga], wl[:, c_dq:c_dv], wl[:, c_dg:]],
                             axis=1).astype(BF16)
        wvt = wl[:, c_dv:c_dg].T.astype(BF16)
        wa2 = jnp.pad(w_alpha2[l], ((0, LANES - GLA_LOWRANK), (0, 0))).astype(BF16)
        ba2 = b_alpha2[l].reshape(1, GLA_QK).astype(F32)

        gq, gk, gv, ggs, g, dq, dk, dvt, dgs = _in_proj(
            x2d, norm_mix_w[l].reshape(1, D), wa, wa2, ba2, wb, wvt, cs_t, sel)

        o_gla = _gla(gq, gk, g, gv, ggs, gla_norm_w[l].reshape(1, GLA_DV), B, S)

        lam_init = 0.8 - 0.6 * math.exp(-0.3 * l)
        lam_vecs = jnp.stack([lam_q1[l], lam_k1[l], lam_q2[l], lam_k2[l]]).astype(F32)
        o_diff = _diff_attn(lam_vecs, dq, dk, dvt, dgs, diff_norm_w[l].reshape(1, DIFF_DV),
                            B, S, lam_init)

        kc, vc = _mem_kv(mem2d, norm_mem_w[l].reshape(1, D), w_ckv[l].astype(BF16), B, M)

        x2d = _out_cross(x2d, o_gla, o_diff, w_out[l].astype(BF16),
                         norm_cross_w[l].reshape(1, D), w_cq[l].astype(BF16), kc, vc,
                         w_co[l].astype(BF16), final_norm_w.reshape(1, D), S, M,
                         final_norm=(l == depth - 1))
    return x2d.reshape(B, S, D)
```

```python
import functools
import math

import jax
import jax.numpy as jnp
import numpy as np
from jax import lax
from jax.experimental import pallas as pl
from jax.experimental.pallas import tpu as pltpu

F32 = jnp.float32
BF16 = jnp.bfloat16

CHUNK = 64
GLA_HEADS = 4
GLA_DK = 64
GLA_DV = 128
GLA_QK = GLA_HEADS * GLA_DK
GLA_WIDTH = GLA_HEADS * GLA_DV
GLA_LOWRANK = 16
GLA_TAU = 16.0
DIFF_HEADS = 4
DIFF_DQK = 64
DIFF_DV = 128
DIFF_WIDTH = DIFF_HEADS * DIFF_DV
ROPE_DIM = 16
ROPE_THETA = 500000.0
CROSS_HEADS = 4
EPS = 1e-6
LOG2E = math.log2(math.e)
ROPE_SPLIT = 3

LANES = 128

TM_PROJ = 1024
TM_GLA = 1024
TK_ATT = 256
SUM_ROWS = 16
VT_ROWS = DIFF_DV + SUM_ROWS
SCORES_AHEAD = 2
TM_OUT = 1024
ROW_PARTS = 2
VMEM_LIMIT = 56 * 1024 * 1024

NT_DIMS = (((1,), (1,)), ((), ()))
TN_DIMS = (((0,), (0,)), ((), ()))


def _rms_scale(xf, w):
    return xf * lax.rsqrt(jnp.mean(xf * xf, axis=-1, keepdims=True) + EPS) * w


def _silu(t):
    return t * (1.0 / (1.0 + jnp.exp(-t)))


def _in_proj_kernel(x_ref, nw_ref, wa_ref, wa2_ref, ba2_ref, wb_ref, wvt_ref, cs_ref, sel_ref,
                    gq_ref, gk_ref, gv_ref, ggs_ref, g_ref, dq_ref, dk_ref, dvt_ref, dgs_ref):
    pm = x_ref.shape[0] // ROW_PARTS
    for part in range(ROW_PARTS):
        _in_proj_rows(slice(part * pm, (part + 1) * pm),
                      x_ref, nw_ref, wa_ref, wa2_ref, ba2_ref, wb_ref, wvt_ref, cs_ref, sel_ref,
                      gq_ref, gk_ref, gv_ref, ggs_ref, g_ref, dq_ref, dk_ref, dvt_ref, dgs_ref)


def _in_proj_rows(rows, x_ref, nw_ref, wa_ref, wa2_ref, ba2_ref, wb_ref, wvt_ref, cs_ref, sel_ref,
                  gq_ref, gk_ref, gv_ref, ggs_ref, g_ref, dq_ref, dk_ref, dvt_ref, dgs_ref):
    h = _rms_scale(x_ref[rows, :], nw_ref[...]).astype(BF16)

    def proj_b(group):
        return jnp.dot(h, wb_ref[:, group * GLA_WIDTH:(group + 1) * GLA_WIDTH],
                       preferred_element_type=F32)

    qkg = jnp.dot(h, wa_ref[...], preferred_element_type=F32)
    gq_ref[rows, :] = qkg[:, 0:GLA_QK] * (GLA_DK ** -0.5)
    gk_ref[rows, :] = qkg[:, GLA_QK:2 * GLA_QK]
    ga = qkg[:, 2 * GLA_QK:].astype(BF16)

    gv_ref[rows, :] = proj_b(0).astype(BF16)
    ggs_ref[rows, :] = _silu(proj_b(1)).astype(BF16)

    tabs = jnp.dot(cs_ref[rows, :], sel_ref[...], preferred_element_type=F32)
    lane = lax.broadcasted_iota(jnp.int32, (tabs.shape[0], LANES), 1)
    cos = tabs[:, :LANES] + jnp.where((lane % DIFF_DQK) < ROPE_DIM, 0.0, 1.0)
    sin = tabs[:, LANES:]
    first_half = (lane % DIFF_DQK) < (ROPE_DIM // 2)

    def rope(t):
        partner = jnp.where(first_half, pltpu.roll(t, LANES - ROPE_DIM // 2, 1),
                            pltpu.roll(t, ROPE_DIM // 2, 1))
        return t * cos + partner * sin

    q = proj_b(2)
    k = proj_b(3)
    for hd in range(DIFF_HEADS):
        sl = slice(hd * DIFF_DV, (hd + 1) * DIFF_DV)
        dq_ref[rows, sl] = (rope(q[:, sl]) * (DIFF_DQK ** -0.5 * LOG2E)).astype(BF16)
        dk_ref[rows, sl] = rope(k[:, sl]).astype(BF16)
    dvt = lax.dot_general(wvt_ref[...], h, NT_DIMS, preferred_element_type=F32).astype(BF16)
    for hd in range(DIFF_HEADS):
        base = hd * VT_ROWS
        dvt_ref[base:base + DIFF_DV, rows] = dvt[hd * DIFF_DV:(hd + 1) * DIFF_DV, :]
        dvt_ref[base + DIFF_DV:base + VT_ROWS, rows] = jnp.ones((SUM_ROWS, dvt.shape[1]), BF16)

    logit = jnp.dot(ga, wa2_ref[...], preferred_element_type=F32) + ba2_ref[...]
    log_sig = jnp.minimum(logit, 0.0) - jnp.log1p(jnp.exp(-jnp.abs(logit)))
    g_ref[rows, :] = log_sig * (1.0 / GLA_TAU)

    dgs_ref[rows, :] = _silu(proj_b(4)).astype(BF16)


def _in_proj(x2d, norm_w, wa, wa2, ba2, wb, wvt, cs_t, sel):
    T, D = x2d.shape
    tm = TM_PROJ
    row = lambda i: (i, 0)
    col = lambda i: (0, i)
    fixed = lambda i: (0, 0)
    out_shapes = (
        jax.ShapeDtypeStruct((T, GLA_QK), F32),
        jax.ShapeDtypeStruct((T, GLA_QK), F32),
        jax.ShapeDtypeStruct((T, GLA_WIDTH), BF16),
        jax.ShapeDtypeStruct((T, GLA_WIDTH), BF16),
        jax.ShapeDtypeStruct((T, GLA_QK), F32),
        jax.ShapeDtypeStruct((T, DIFF_WIDTH), BF16),
        jax.ShapeDtypeStruct((T, DIFF_WIDTH), BF16),
        jax.ShapeDtypeStruct((DIFF_HEADS * VT_ROWS, T), BF16),
        jax.ShapeDtypeStruct((T, DIFF_WIDTH), BF16),
    )
    out_specs = [pl.BlockSpec((tm, s.shape[1]), row) for s in out_shapes]
    out_specs[7] = pl.BlockSpec((DIFF_HEADS * VT_ROWS, tm), col)
    return pl.pallas_call(
        _in_proj_kernel,
        grid=(T // tm,),
        in_specs=[
            pl.BlockSpec((tm, D), row),
            pl.BlockSpec((1, D), fixed),
            pl.BlockSpec(wa.shape, fixed),
            pl.BlockSpec(wa2.shape, fixed),
            pl.BlockSpec(ba2.shape, fixed),
            pl.BlockSpec(wb.shape, fixed),
            pl.BlockSpec(wvt.shape, fixed),
            pl.BlockSpec((tm, cs_t.shape[1]), row),
            pl.BlockSpec(sel.shape, fixed),
        ],
        out_specs=tuple(out_specs),
        out_shape=out_shapes,
        compiler_params=pltpu.CompilerParams(dimension_semantics=("parallel",),
                                             vmem_limit_bytes=VMEM_LIMIT),
        name="in_proj",
    )(x2d, norm_w, wa, wa2, ba2, wb, wvt, cs_t, sel)


def _gla_kernel(gq_ref, gk_ref, g_ref, gv_ref, ggs_ref, nw_ref, o_ref,
                qd4_s, ki_s, kl4_s, dec_s, state_s):
    tm = gq_ref.shape[0]
    n_chunks = tm // CHUNK

    @pl.when(pl.program_id(1) == 0)
    def _():
        state_s[...] = jnp.zeros_like(state_s)

    g = g_ref[...]
    row_in_chunk = lax.broadcasted_iota(jnp.int32, g.shape, 0) % CHUNK
    G = g
    shift = 1
    while shift < CHUNK:
        G = G + jnp.where(row_in_chunk >= shift, pltpu.roll(G, shift, 0), 0.0)
        shift *= 2
    q = gq_ref[...]
    k = gk_ref[...]
    qd = (q * jnp.exp(G)).astype(BF16)
    ki_s[...] = (k * jnp.exp(-G)).astype(BF16)

    lane_head = lax.broadcasted_iota(jnp.int32, (CHUNK, GLA_QK), 1) // GLA_DK
    zero = jnp.zeros((CHUNK, GLA_QK), BF16)
    for c in range(n_chunks):
        rows = slice(c * CHUNK, (c + 1) * CHUNK)
        g_last = G[(c + 1) * CHUNK - 1:(c + 1) * CHUNK, :]
        kl = (k[rows, :] * jnp.exp(g_last - G[rows, :])).astype(BF16)
        dec_s[c:c + 1, :] = jnp.exp(g_last)
        for hd in range(GLA_HEADS):
            dst = slice((c * GLA_HEADS + hd) * CHUNK, (c * GLA_HEADS + hd + 1) * CHUNK)
            qd4_s[dst, :] = jnp.where(lane_head == hd, qd[rows, :], zero)
            kl4_s[dst, :] = jnp.where(lane_head == hd, kl, zero)

    r_i = lax.broadcasted_iota(jnp.int32, (GLA_HEADS * CHUNK, GLA_HEADS * CHUNK), 0)
    c_i = lax.broadcasted_iota(jnp.int32, (GLA_HEADS * CHUNK, GLA_HEADS * CHUNK), 1)
    keep = (r_i // CHUNK == c_i // CHUNK) & (c_i % CHUNK <= r_i % CHUNK)
    nw = nw_ref[...]

    def chunk_rows4(c):
        return slice(c * GLA_HEADS * CHUNK, (c + 1) * GLA_HEADS * CHUNK)

    def chunk_values(c):
        rows = slice(c * CHUNK, (c + 1) * CHUNK)
        return jnp.concatenate([gv_ref[rows, hd * GLA_DV:(hd + 1) * GLA_DV]
                                for hd in range(GLA_HEADS)], axis=0)

    def state_free_products(c):
        ki = ki_s[c * CHUNK:(c + 1) * CHUNK, :]
        k4 = jnp.concatenate([ki] * GLA_HEADS, axis=0)
        scores = lax.dot_general(qd4_s[chunk_rows4(c), :], k4, NT_DIMS, preferred_element_type=F32)
        kv_t = lax.dot_general(chunk_values(c), kl4_s[chunk_rows4(c), :], TN_DIMS,
                               preferred_element_type=F32)
        return scores, kv_t

    ahead = state_free_products(0)
    for c in range(n_chunks):
        rows = slice(c * CHUNK, (c + 1) * CHUNK)
        scores, kv_t = ahead
        if c + 1 < n_chunks:
            ahead = state_free_products(c + 1)
        q4 = qd4_s[chunk_rows4(c), :]
        st = state_s[...]
        scores = jnp.where(keep, scores, 0.0).astype(BF16)
        o = (jnp.dot(scores, chunk_values(c), preferred_element_type=F32)
             + lax.dot_general(q4, st.astype(BF16), NT_DIMS, preferred_element_type=F32))
        state_s[...] = dec_s[c:c + 1, :] * st + kv_t
        gate = jnp.concatenate([ggs_ref[rows, hd * GLA_DV:(hd + 1) * GLA_DV]
                                for hd in range(GLA_HEADS)], axis=0).astype(F32)
        o = (_rms_scale(o, nw) * gate).astype(BF16)
        for hd in range(GLA_HEADS):
            o_ref[rows, hd * GLA_DV:(hd + 1) * GLA_DV] = o[hd * CHUNK:(hd + 1) * CHUNK, :]


def _gla(gq, gk, g, gv, ggs, norm_w, batch, seq):
    tm = TM_GLA
    nt = seq // tm
    row = lambda b, i: (b * nt + i, 0)
    fixed = lambda b, i: (0, 0)
    T = gq.shape[0]
    return pl.pallas_call(
        _gla_kernel,
        grid=(batch, nt),
        in_specs=[
            pl.BlockSpec((tm, GLA_QK), row),
            pl.BlockSpec((tm, GLA_QK), row),
            pl.BlockSpec((tm, GLA_QK), row),
            pl.BlockSpec((tm, GLA_WIDTH), row),
            pl.BlockSpec((tm, GLA_WIDTH), row),
            pl.BlockSpec((1, GLA_DV), fixed),
        ],
        out_specs=pl.BlockSpec((tm, GLA_WIDTH), row),
        out_shape=jax.ShapeDtypeStruct((T, GLA_WIDTH), BF16),
        scratch_shapes=[
            pltpu.VMEM((GLA_HEADS * tm, GLA_QK), BF16),
            pltpu.VMEM((tm, GLA_QK), BF16),
            pltpu.VMEM((GLA_HEADS * tm, GLA_QK), BF16),
            pltpu.VMEM((tm // CHUNK, GLA_QK), F32),
            pltpu.VMEM((GLA_DV, GLA_QK), F32),
        ],
        compiler_params=pltpu.CompilerParams(dimension_semantics=("parallel", "arbitrary"),
                                             vmem_limit_bytes=VMEM_LIMIT),
        name="gla",
    )(gq, gk, g, gv, ggs, norm_w)


def _diff_attn_kernel(lam_ref, q_ref, k_ref, vt_ref, gs_ref, nw_ref, o_ref,
                      qq_s, *, lam_init):
    seq = q_ref.shape[0]
    tk = TK_ATT
    n_blk = seq // tk
    cw = 2 * tk

    lv = lam_ref[...]
    lam = (jnp.exp(jnp.sum(lv[0:1] * lv[1:2], axis=-1, keepdims=True))
           - jnp.exp(jnp.sum(lv[2:3] * lv[3:4], axis=-1, keepdims=True)) + lam_init)
    nw = nw_ref[...]

    qc = (lax.broadcasted_iota(jnp.int32, (cw, tk), 0) % tk) // CHUNK
    kc = lax.broadcasted_iota(jnp.int32, (cw, tk), 1) // CHUNK
    stair_bias = jnp.where(kc <= qc, 0.0, -jnp.inf).astype(F32)

    lane = lax.broadcasted_iota(jnp.int32, (tk, LANES), 1)
    for r in range(n_blk):
        q = q_ref[r * tk:(r + 1) * tk, :]
        zero = jnp.zeros_like(q)
        qq_s[r * cw:r * cw + tk, :] = jnp.where(lane < DIFF_DQK, q, zero)
        qq_s[r * cw + tk:(r + 1) * cw, :] = jnp.where(lane >= DIFF_DQK, q, zero)

    def scores(cb):
        nk = (cb + 1) * tk
        s = lax.dot_general(qq_s[cb * cw:(cb + 1) * cw, :], k_ref[0:nk, :], NT_DIMS,
                            preferred_element_type=F32)
        parts = [s[:, j * tk:(j + 1) * tk] for j in range(cb)] + [s[:, cb * tk:] + stair_bias]
        mm = parts[0]
        for part in parts[1:]:
            mm = jnp.maximum(mm, part)
        return parts, jnp.max(mm, axis=-1, keepdims=True)

    def finish(cb, parts, m):
        nk = (cb + 1) * tk
        p = jnp.concatenate([jnp.exp2(part - m).astype(BF16) for part in parts], axis=1)
        pv = lax.dot_general(p, vt_ref[:, 0:nk], NT_DIMS, preferred_element_type=F32)
        o0 = pv[0:tk, 0:DIFF_DV] / pv[0:tk, DIFF_DV:DIFF_DV + 1]
        o1 = pv[tk:, 0:DIFF_DV] / pv[tk:, DIFF_DV:DIFF_DV + 1]
        o = o0 - lam * o1
        rows = slice(cb * tk, (cb + 1) * tk)
        o = _rms_scale(o, nw) * (1.0 - lam_init) * gs_ref[rows, :].astype(F32)
        o_ref[rows, :] = o.astype(BF16)

    pending = [scores(cb) for cb in range(SCORES_AHEAD)]
    for cb in range(n_blk):
        if cb + SCORES_AHEAD < n_blk:
            pending.append(scores(cb + SCORES_AHEAD))
        finish(cb, *pending.pop(0))


def _diff_attn(lam_vecs, dq, dk, dvt, dgs, norm_w, batch, seq, lam_init):
    assert TK_ATT % CHUNK == 0 and seq % TK_ATT == 0
    T = dq.shape[0]
    bh = lambda b, h: (b, h)
    hb = lambda b, h: (h, b)
    fixed = lambda b, h: (0, 0)
    blk = pl.BlockSpec((seq, DIFF_DV), bh)
    return pl.pallas_call(
        functools.partial(_diff_attn_kernel, lam_init=lam_init),
        grid=(batch, DIFF_HEADS),
        in_specs=[pl.BlockSpec(lam_vecs.shape, fixed), blk, blk,
                  pl.BlockSpec((VT_ROWS, seq), hb), blk,
                  pl.BlockSpec((1, DIFF_DV), fixed)],
        out_specs=blk,
        out_shape=jax.ShapeDtypeStruct((T, DIFF_WIDTH), BF16),
        scratch_shapes=[
            pltpu.VMEM((2 * seq, DIFF_DV), BF16),
        ],
        compiler_params=pltpu.CompilerParams(dimension_semantics=("parallel", "parallel"),
                                             vmem_limit_bytes=VMEM_LIMIT),
        name="diff_attn",
    )(lam_vecs, dq, dk, dvt, dgs, norm_w)


def _mem_kv_kernel(mem_ref, nw_ref, w_ref, k_ref, v_ref):
    d = mem_ref.shape[1]
    mn = _rms_scale(mem_ref[...], nw_ref[...]).astype(BF16)
    k_ref[...] = jnp.dot(mn, w_ref[:, :d], preferred_element_type=F32).astype(BF16)
    v_ref[...] = jnp.dot(mn, w_ref[:, d:], preferred_element_type=F32).astype(BF16)


def _mem_kv(mem2d, norm_w, w_ckv, batch, mem_len):
    D = mem2d.shape[1]
    row = lambda b: (b, 0)
    fixed = lambda b: (0, 0)
    shp = jax.ShapeDtypeStruct((batch * mem_len, D), BF16)
    return pl.pallas_call(
        _mem_kv_kernel,
        grid=(batch,),
        in_specs=[pl.BlockSpec((mem_len, D), row), pl.BlockSpec((1, D), fixed),
                  pl.BlockSpec(w_ckv.shape, fixed)],
        out_specs=(pl.BlockSpec((mem_len, D), row), pl.BlockSpec((mem_len, D), row)),
        out_shape=(shp, shp),
        compiler_params=pltpu.CompilerParams(dimension_semantics=("parallel",),
                                             vmem_limit_bytes=VMEM_LIMIT),
        name="mem_kv",
    )(mem2d, norm_w, w_ckv)


def _out_cross_kernel(x_ref, og_ref, od_ref, wo_ref, ncw_ref, wq_ref, k_ref, v_ref, wco_ref,
                      fnw_ref, o_ref, ctx_s, *, final_norm):
    d = x_ref.shape[1]
    dh = d // CROSS_HEADS
    pm = x_ref.shape[0] // ROW_PARTS
    parts = [slice(p * pm, (p + 1) * pm) for p in range(ROW_PARTS)]
    x1 = [x_ref[r, :]
          + jnp.dot(og_ref[r, :], wo_ref[0:GLA_WIDTH, :], preferred_element_type=F32)
          + jnp.dot(od_ref[r, :], wo_ref[GLA_WIDTH:, :], preferred_element_type=F32)
          for r in parts]
    q = [(jnp.dot(_rms_scale(x, ncw_ref[...]).astype(BF16), wq_ref[...],
                  preferred_element_type=F32) * (dh ** -0.5)).astype(BF16) for x in x1]
    def head_scores(hd, part):
        sl = slice(hd * dh, (hd + 1) * dh)
        return lax.dot_general(q[part][:, sl], k_ref[:, sl], NT_DIMS,
                               preferred_element_type=F32)

    items = [(hd, part) for hd in range(CROSS_HEADS) for part in range(ROW_PARTS)]
    s_next = head_scores(*items[0])
    for n, (hd, part) in enumerate(items):
        s = s_next
        if n + 1 < len(items):
            s_next = head_scores(*items[n + 1])
        sl = slice(hd * dh, (hd + 1) * dh)
        p = jnp.exp(s - jnp.max(s, axis=-1, keepdims=True))
        p = p / jnp.sum(p, axis=-1, keepdims=True)
        ctx_s[parts[part], sl] = jnp.dot(p.astype(BF16), v_ref[:, sl],
                                         preferred_element_type=F32).astype(BF16)
    for r, x in zip(parts, x1):
        x2 = x + jnp.dot(ctx_s[r, :], wco_ref[...], preferred_element_type=F32)
        if final_norm:
            x2 = _rms_scale(x2, fnw_ref[...])
        o_ref[r, :] = x2


def _out_cross(x2d, og, od, w_out, ncw, w_cq, kc, vc, w_co, fnw, seq, mem_len, final_norm):
    T, D = x2d.shape
    tm = TM_OUT
    per_b = seq // tm
    row = lambda i: (i, 0)
    fixed = lambda i: (0, 0)
    bmap = lambda i: (i // per_b, 0)
    return pl.pallas_call(
        functools.partial(_out_cross_kernel, final_norm=final_norm),
        grid=(T // tm,),
        in_specs=[
            pl.BlockSpec((tm, D), row),
            pl.BlockSpec((tm, GLA_WIDTH), row),
            pl.BlockSpec((tm, DIFF_WIDTH), row),
            pl.BlockSpec(w_out.shape, fixed),
            pl.BlockSpec((1, D), fixed),
            pl.BlockSpec(w_cq.shape, fixed),
            pl.BlockSpec((mem_len, D), bmap),
            pl.BlockSpec((mem_len, D), bmap),
            pl.BlockSpec(w_co.shape, fixed),
            pl.BlockSpec((1, D), fixed),
        ],
        out_specs=pl.BlockSpec((tm, D), row),
        out_shape=jax.ShapeDtypeStruct((T, D), F32),
        scratch_shapes=[pltpu.VMEM((tm, D), BF16)],
        compiler_params=pltpu.CompilerParams(dimension_semantics=("parallel",),
                                             vmem_limit_bytes=VMEM_LIMIT),
        name="out_cross",
    )(x2d, og, od, w_out, ncw, w_cq, kc, vc, w_co, fnw)


def _rope_tables(positions):
    half = ROPE_DIM // 2
    T = positions.size
    per_row = LANES // half
    inv_freq = ROPE_THETA ** (-(jnp.arange(0, ROPE_DIM, 2, dtype=F32) / ROPE_DIM))
    pos = positions.reshape(T // per_row, per_row).astype(F32)
    ang = jnp.repeat(pos, half, axis=1) * jnp.tile(inv_freq, per_row)
    pieces = []
    rest_c, rest_s = jnp.cos(ang), jnp.sin(ang)
    for _ in range(ROPE_SPLIT):
        pc, ps = rest_c.astype(BF16), rest_s.astype(BF16)
        pieces += [pc.reshape(T, half), ps.reshape(T, half)]
        rest_c, rest_s = rest_c - pc.astype(F32), rest_s - ps.astype(F32)
    return jnp.concatenate(pieces, axis=1)


def _rope_select_matrix():
    half = ROPE_DIM // 2
    sel = np.zeros((ROPE_DIM, 2 * LANES), np.float32)
    for j in range(LANES):
        jm = j % DIFF_DQK
        if jm < ROPE_DIM:
            sel[jm % half, j] = 1.0
            sel[half + jm % half, LANES + j] = -1.0 if jm < half else 1.0
    return jnp.asarray(np.tile(sel, (ROPE_SPLIT, 1)), BF16)


def kernel(x, mem, positions, norm_mix_w, w_in, w_alpha2, b_alpha2, gla_norm_w,
           lam_q1, lam_k1, lam_q2, lam_k2, diff_norm_w, w_out,
           norm_cross_w, norm_mem_w, w_cq, w_ckv, w_co, final_norm_w):
    B, S, D = x.shape
    M = mem.shape[1]
    depth = w_in.shape[0]
    T = B * S
    cs_t = _rope_tables(positions)
    sel = _rope_select_matrix()
    x2d = x.reshape(T, D)
    mem2d = mem.reshape(B * M, D)

    c_gv = 2 * GLA_QK
    c_ga = c_gv + 2 * GLA_WIDTH
    c_dq = c_ga + GLA_LOWRANK

    for l in range(depth):
        wl = w_in[l]
        ga_pad = jnp.pad(wl[:, c_ga:c_dq], ((0, 0), (0, LANES - GLA_LOWRANK)))
        wa = jnp.concatenate([wl[:, :c_gv], ga_pad], axis=1).astype(BF16)
        c_dv = c_dq + 2 * DIFF_WIDTH
        c_dg = c_dv + DIFF_WIDTH
        wb = jnp.concatenate([wl[:, c_gv:c_ga], wl[:, c_dq:c_dv], wl[:, c_dg:]],
                             axis=1).astype(BF16)
        wvt = wl[:, c_dv:c_dg].T.astype(BF16)
        wa2 = jnp.pad(w_alpha2[l], ((0, LANES - GLA_LOWRANK), (0, 0))).astype(BF16)
        ba2 = b_alpha2[l].reshape(1, GLA_QK).astype(F32)

        gq, gk, gv, ggs, g, dq, dk, dvt, dgs = _in_proj(
            x2d, norm_mix_w[l].reshape(1, D), wa, wa2, ba2, wb, wvt, cs_t, sel)

        o_gla = _gla(gq, gk, g, gv, ggs, gla_norm_w[l].reshape(1, GLA_DV), B, S)

        lam_init = 0.8 - 0.6 * math.exp(-0.3 * l)
        lam_vecs = jnp.stack([lam_q1[l], lam_k1[l], lam_q2[l], lam_k2[l]]).astype(F32)
        o_diff = _diff_attn(lam_vecs, dq, dk, dvt, dgs, diff_norm_w[l].reshape(1, DIFF_DV),
                            B, S, lam_init)

        kc, vc = _mem_kv(mem2d, norm_mem_w[l].reshape(1, D), w_ckv[l].astype(BF16), B, M)

        x2d = _out_cross(x2d, o_gla, o_diff, w_out[l].astype(BF16),
                         norm_cross_w[l].reshape(1, D), w_cq[l].astype(BF16), kc, vc,
                         w_co[l].astype(BF16), final_norm_w.reshape(1, D), S, M,
                         final_norm=(l == depth - 1))
    return x2d.reshape(B, S, D)
```

```python
import functools
import math

import jax
import jax.numpy as jnp
import numpy as np
from jax import lax
from jax.experimental import pallas as pl
from jax.experimental.pallas import tpu as pltpu

F32 = jnp.float32
BF16 = jnp.bfloat16

CHUNK = 64
GLA_HEADS = 4
GLA_DK = 64
GLA_DV = 128
GLA_QK = GLA_HEADS * GLA_DK
GLA_WIDTH = GLA_HEADS * GLA_DV
GLA_LOWRANK = 16
GLA_TAU = 16.0
DIFF_HEADS = 4
DIFF_DQK = 64
DIFF_DV = 128
DIFF_WIDTH = DIFF_HEADS * DIFF_DV
ROPE_DIM = 16
ROPE_THETA = 500000.0
CROSS_HEADS = 4
EPS = 1e-6
LOG2E = math.log2(math.e)
ROPE_SPLIT = 3

LANES = 128

TM_PROJ = 1024
TM_GLA = 1024
TK_ATT = 256
SUM_ROWS = 16
VT_ROWS = DIFF_DV + SUM_ROWS
SCORES_AHEAD = 2
TM_OUT = 1024
ROW_PARTS = 2
VMEM_LIMIT = 56 * 1024 * 1024

NT_DIMS = (((1,), (1,)), ((), ()))
TN_DIMS = (((0,), (0,)), ((), ()))


def _rms_scale(xf, w):
    return xf * lax.rsqrt(jnp.mean(xf * xf, axis=-1, keepdims=True) + EPS) * w


def _silu(t):
    return t * (1.0 / (1.0 + jnp.exp(-t)))


def _in_proj_kernel(x_ref, nw_ref, wa_ref, wa2_ref, ba2_ref, wb_ref, wvt_ref, cs_ref, sel_ref,
                    gq_ref, gk_ref, gv_ref, ggs_ref, g_ref, dq_ref, dk_ref, dvt_ref, dgs_ref):
    pm = x_ref.shape[0] // ROW_PARTS
    for part in range(ROW_PARTS):
        _in_proj_rows(slice(part * pm, (part + 1) * pm),
                      x_ref, nw_ref, wa_ref, wa2_ref, ba2_ref, wb_ref, wvt_ref, cs_ref, sel_ref,
                      gq_ref, gk_ref, gv_ref, ggs_ref, g_ref, dq_ref, dk_ref, dvt_ref, dgs_ref)


def _in_proj_rows(rows, x_ref, nw_ref, wa_ref, wa2_ref, ba2_ref, wb_ref, wvt_ref, cs_ref, sel_ref,
                  gq_ref, gk_ref, gv_ref, ggs_ref, g_ref, dq_ref, dk_ref, dvt_ref, dgs_ref):
    h = _rms_scale(x_ref[rows, :], nw_ref[...]).astype(BF16)

    def proj_b(group):
        return jnp.dot(h, wb_ref[:, group * GLA_WIDTH:(group + 1) * GLA_WIDTH],
                       preferred_element_type=F32)

    qkg = jnp.dot(h, wa_ref[...], preferred_element_type=F32)
    gq_ref[rows, :] = qkg[:, 0:GLA_QK] * (GLA_DK ** -0.5)
    gk_ref[rows, :] = qkg[:, GLA_QK:2 * GLA_QK]
    ga = qkg[:, 2 * GLA_QK:].astype(BF16)

    gv_ref[rows, :] = proj_b(0).astype(BF16)
    ggs_ref[rows, :] = _silu(proj_b(1)).astype(BF16)

    tabs = jnp.dot(cs_ref[rows, :], sel_ref[...], preferred_element_type=F32)
    lane = lax.broadcasted_iota(jnp.int32, (tabs.shape[0], LANES), 1)
    cos = tabs[:, :LANES] + jnp.where((lane % DIFF_DQK) < ROPE_DIM, 0.0, 1.0)
    sin = tabs[:, LANES:]
    first_half = (lane % DIFF_DQK) < (ROPE_DIM // 2)

    def rope(t):
        partner = jnp.where(first_half, pltpu.roll(t, LANES - ROPE_DIM // 2, 1),
                            pltpu.roll(t, ROPE_DIM // 2, 1))
        return t * cos + partner * sin

    q = proj_b(2)
    k = proj_b(3)
    for hd in range(DIFF_HEADS):
        sl = slice(hd * DIFF_DV, (hd + 1) * DIFF_DV)
        dq_ref[rows, sl] = (rope(q[:, sl]) * (DIFF_DQK ** -0.5 * LOG2E)).astype(BF16)
        dk_ref[rows, sl] = rope(k[:, sl]).astype(BF16)
    dvt = lax.dot_general(wvt_ref[...], h, NT_DIMS, preferred_element_type=F32).astype(BF16)
    for hd in range(DIFF_HEADS):
        base = hd * VT_ROWS
        dvt_ref[base:base + DIFF_DV, rows] = dvt[hd * DIFF_DV:(hd + 1) * DIFF_DV, :]
        dvt_ref[base + DIFF_DV:base + VT_ROWS, rows] = jnp.ones((SUM_ROWS, dvt.shape[1]), BF16)

    logit = jnp.dot(ga, wa2_ref[...], preferred_element_type=F32) + ba2_ref[...]
    log_sig = jnp.minimum(logit, 0.0) - jnp.log1p(jnp.exp(-jnp.abs(logit)))
    g_ref[rows, :] = log_sig * (1.0 / GLA_TAU)

    dgs_ref[rows, :] = _silu(proj_b(4)).astype(BF16)


def _in_proj(x2d, norm_w, wa, wa2, ba2, wb, wvt, cs_t, sel):
    T, D = x2d.shape
    tm = TM_PROJ
    row = lambda i: (i, 0)
    col = lambda i: (0, i)
    fixed = lambda i: (0, 0)
    out_shapes = (
        jax.ShapeDtypeStruct((T, GLA_QK), F32),
        jax.ShapeDtypeStruct((T, GLA_QK), F32),
        jax.ShapeDtypeStruct((T, GLA_WIDTH), BF16),
        jax.ShapeDtypeStruct((T, GLA_WIDTH), BF16),
        jax.ShapeDtypeStruct((T, GLA_QK), F32),
        jax.ShapeDtypeStruct((T, DIFF_WIDTH), BF16),
        jax.ShapeDtypeStruct((T, DIFF_WIDTH), BF16),
        jax.ShapeDtypeStruct((DIFF_HEADS * VT_ROWS, T), BF16),
        jax.ShapeDtypeStruct((T, DIFF_WIDTH), BF16),
    )
    out_specs = [pl.BlockSpec((tm, s.shape[1]), row) for s in out_shapes]
    out_specs[7] = pl.BlockSpec((DIFF_HEADS * VT_ROWS, tm), col)
    return pl.pallas_call(
        _in_proj_kernel,
        grid=(T // tm,),
        in_specs=[
            pl.BlockSpec((tm, D), row),
            pl.BlockSpec((1, D), fixed),
            pl.BlockSpec(wa.shape, fixed),
            pl.BlockSpec(wa2.shape, fixed),
            pl.BlockSpec(ba2.shape, fixed),
            pl.BlockSpec(wb.shape, fixed),
            pl.BlockSpec(wvt.shape, fixed),
            pl.BlockSpec((tm, cs_t.shape[1]), row),
            pl.BlockSpec(sel.shape, fixed),
        ],
        out_specs=tuple(out_specs),
        out_shape=out_shapes,
        compiler_params=pltpu.CompilerParams(dimension_semantics=("parallel",),
                                             vmem_limit_bytes=VMEM_LIMIT),
        name="in_proj",
    )(x2d, norm_w, wa, wa2, ba2, wb, wvt, cs_t, sel)


def _gla_kernel(gq_ref, gk_ref, g_ref, gv_ref, ggs_ref, nw_ref, o_ref,
                qd4_s, ki_s, kl4_s, dec_s, state_s):
    tm = gq_ref.shape[0]
    n_chunks = tm // CHUNK

    @pl.when(pl.program_id(1) == 0)
    def _():
        state_s[...] = jnp.zeros_like(state_s)

    g = g_ref[...]
    row_in_chunk = lax.broadcasted_iota(jnp.int32, g.shape, 0) % CHUNK
    G = g
    shift = 1
    while shift < CHUNK:
        G = G + jnp.where(row_in_chunk >= shift, pltpu.roll(G, shift, 0), 0.0)
        shift *= 2
    q = gq_ref[...]
    k = gk_ref[...]
    qd = (q * jnp.exp(G)).astype(BF16)
    ki_s[...] = (k * jnp.exp(-G)).astype(BF16)

    lane_head = lax.broadcasted_iota(jnp.int32, (CHUNK, GLA_QK), 1) // GLA_DK
    zero = jnp.zeros((CHUNK, GLA_QK), BF16)
    for c in range(n_chunks):
        rows = slice(c * CHUNK, (c + 1) * CHUNK)
        g_last = G[(c + 1) * CHUNK - 1:(c + 1) * CHUNK, :]
        kl = (k[rows, :] * jnp.exp(g_last - G[rows, :])).astype(BF16)
        dec_s[c:c + 1, :] = jnp.exp(g_last)
        for hd in range(GLA_HEADS):
            dst = slice((c * GLA_HEADS + hd) * CHUNK, (c * GLA_HEADS + hd + 1) * CHUNK)
            qd4_s[dst, :] = jnp.where(lane_head == hd, qd[rows, :], zero)
            kl4_s[dst, :] = jnp.where(lane_head == hd, kl, zero)

    r_i = lax.broadcasted_iota(jnp.int32, (GLA_HEADS * CHUNK, GLA_HEADS * CHUNK), 0)
    c_i = lax.broadcasted_iota(jnp.int32, (GLA_HEADS * CHUNK, GLA_HEADS * CHUNK), 1)
    keep = (r_i // CHUNK == c_i // CHUNK) & (c_i % CHUNK <= r_i % CHUNK)
    nw = nw_ref[...]

    def chunk_rows4(c):
        return slice(c * GLA_HEADS * CHUNK, (c + 1) * GLA_HEADS * CHUNK)

    def chunk_values(c):
        rows = slice(c * CHUNK, (c + 1) * CHUNK)
        return jnp.concatenate([gv_ref[rows, hd * GLA_DV:(hd + 1) * GLA_DV]
                                for hd in range(GLA_HEADS)], axis=0)

    def state_free_products(c):
        ki = ki_s[c * CHUNK:(c + 1) * CHUNK, :]
        k4 = jnp.concatenate([ki] * GLA_HEADS, axis=0)
        scores = lax.dot_general(qd4_s[chunk_rows4(c), :], k4, NT_DIMS, preferred_element_type=F32)
        kv_t = lax.dot_general(chunk_values(c), kl4_s[chunk_rows4(c), :], TN_DIMS,
                               preferred_element_type=F32)
        return scores, kv_t

    ahead = state_free_products(0)
    for c in range(n_chunks):
        rows = slice(c * CHUNK, (c + 1) * CHUNK)
        scores, kv_t = ahead
        if c + 1 < n_chunks:
            ahead = state_free_products(c + 1)
        q4 = qd4_s[chunk_rows4(c), :]
        st = state_s[...]
        scores = jnp.where(keep, scores, 0.0).astype(BF16)
        o = (jnp.dot(scores, chunk_values(c), preferred_element_type=F32)
             + lax.dot_general(q4, st.astype(BF16), NT_DIMS, preferred_element_type=F32))
        state_s[...] = dec_s[c:c + 1, :] * st + kv_t
        gate = jnp.concatenate([ggs_ref[rows, hd * GLA_DV:(hd + 1) * GLA_DV]
                                for hd in range(GLA_HEADS)], axis=0).astype(F32)
        o = (_rms_scale(o, nw) * gate).astype(BF16)
        for hd in range(GLA_HEADS):
            o_ref[rows, hd * GLA_DV:(hd + 1) * GLA_DV] = o[hd * CHUNK:(hd + 1) * CHUNK, :]


def _gla(gq, gk, g, gv, ggs, norm_w, batch, seq):
    tm = TM_GLA
    nt = seq // tm
    row = lambda b, i: (b * nt + i, 0)
    fixed = lambda b, i: (0, 0)
    T = gq.shape[0]
    return pl.pallas_call(
        _gla_kernel,
        grid=(batch, nt),
        in_specs=[
            pl.BlockSpec((tm, GLA_QK), row),
            pl.BlockSpec((tm, GLA_QK), row),
            pl.BlockSpec((tm, GLA_QK), row),
            pl.BlockSpec((tm, GLA_WIDTH), row),
            pl.BlockSpec((tm, GLA_WIDTH), row),
            pl.BlockSpec((1, GLA_DV), fixed),
        ],
        out_specs=pl.BlockSpec((tm, GLA_WIDTH), row),
        out_shape=jax.ShapeDtypeStruct((T, GLA_WIDTH), BF16),
        scratch_shapes=[
            pltpu.VMEM((GLA_HEADS * tm, GLA_QK), BF16),
            pltpu.VMEM((tm, GLA_QK), BF16),
            pltpu.VMEM((GLA_HEADS * tm, GLA_QK), BF16),
            pltpu.VMEM((tm // CHUNK, GLA_QK), F32),
            pltpu.VMEM((GLA_DV, GLA_QK), F32),
        ],
        compiler_params=pltpu.CompilerParams(dimension_semantics=("parallel", "arbitrary"),
                                             vmem_limit_bytes=VMEM_LIMIT),
        name="gla",
    )(gq, gk, g, gv, ggs, norm_w)


def _diff_attn_kernel(lam_ref, q_ref, k_ref, vt_ref, gs_ref, nw_ref, o_ref,
                      qq_s, *, lam_init):
    seq = q_ref.shape[0]
    tk = TK_ATT
    n_blk = seq // tk
    cw = 2 * tk

    lv = lam_ref[...]
    lam = (jnp.exp(jnp.sum(lv[0:1] * lv[1:2], axis=-1, keepdims=True))
           - jnp.exp(jnp.sum(lv[2:3] * lv[3:4], axis=-1, keepdims=True)) + lam_init)
    nw = nw_ref[...]

    qc = (lax.broadcasted_iota(jnp.int32, (cw, tk), 0) % tk) // CHUNK
    kc = lax.broadcasted_iota(jnp.int32, (cw, tk), 1) // CHUNK
    stair_bias = jnp.where(kc <= qc, 0.0, -jnp.inf).astype(F32)

    lane = lax.broadcasted_iota(jnp.int32, (tk, LANES), 1)
    for r in range(n_blk):
        q = q_ref[r * tk:(r + 1) * tk, :]
        zero = jnp.zeros_like(q)
        qq_s[r * cw:r * cw + tk, :] = jnp.where(lane < DIFF_DQK, q, zero)
        qq_s[r * cw + tk:(r + 1) * cw, :] = jnp.where(lane >= DIFF_DQK, q, zero)

    def scores(cb):
        nk = (cb + 1) * tk
        s = lax.dot_general(qq_s[cb * cw:(cb + 1) * cw, :], k_ref[0:nk, :], NT_DIMS,
                            preferred_element_type=F32)
        parts = [s[:, j * tk:(j + 1) * tk] for j in range(cb)] + [s[:, cb * tk:] + stair_bias]
        mm = parts[0]
        for part in parts[1:]:
            mm = jnp.maximum(mm, part)
        return parts, jnp.max(mm, axis=-1, keepdims=True)

    def finish(cb, parts, m):
        nk = (cb + 1) * tk
        p = jnp.concatenate([jnp.exp2(part - m).astype(BF16) for part in parts], axis=1)
        pv = lax.dot_general(p, vt_ref[:, 0:nk], NT_DIMS, preferred_element_type=F32)
        o0 = pv[0:tk, 0:DIFF_DV] / pv[0:tk, DIFF_DV:DIFF_DV + 1]
        o1 = pv[tk:, 0:DIFF_DV] / pv[tk:, DIFF_DV:DIFF_DV + 1]
        o = o0 - lam * o1
        rows = slice(cb * tk, (cb + 1) * tk)
        o = _rms_scale(o, nw) * (1.0 - lam_init) * gs_ref[rows, :].astype(F32)
        o_ref[rows, :] = o.astype(BF16)

    pending = [scores(cb) for cb in range(SCORES_AHEAD)]
    for cb in range(n_blk):
        if cb + SCORES_AHEAD < n_blk:
            pending.append(scores(cb + SCORES_AHEAD))
        finish(cb, *pending.pop(0))


def _diff_attn(lam_vecs, dq, dk, dvt, dgs, norm_w, batch, seq, lam_init):
    assert TK_ATT % CHUNK == 0 and seq % TK_ATT == 0
    T = dq.shape[0]
    bh = lambda b, h: (b, h)
    hb = lambda b, h: (h, b)
    fixed = lambda b, h: (0, 0)
    blk = pl.BlockSpec((seq, DIFF_DV), bh)
    return pl.pallas_call(
        functools.partial(_diff_attn_kernel, lam_init=lam_init),
        grid=(batch, DIFF_HEADS),
        in_specs=[pl.BlockSpec(lam_vecs.shape, fixed), blk, blk,
                  pl.BlockSpec((VT_ROWS, seq), hb), blk,
                  pl.BlockSpec((1, DIFF_DV), fixed)],
        out_specs=blk,
        out_shape=jax.ShapeDtypeStruct((T, DIFF_WIDTH), BF16),
        scratch_shapes=[
            pltpu.VMEM((2 * seq, DIFF_DV), BF16),
        ],
        compiler_params=pltpu.CompilerParams(dimension_semantics=("parallel", "parallel"),
                                             vmem_limit_bytes=VMEM_LIMIT),
        name="diff_attn",
    )(lam_vecs, dq, dk, dvt, dgs, norm_w)


def _mem_kv_kernel(mem_ref, nw_ref, w_ref, k_ref, v_ref):
    d = mem_ref.shape[1]
    mn = _rms_scale(mem_ref[...], nw_ref[...]).astype(BF16)
    k_ref[...] = jnp.dot(mn, w_ref[:, :d], preferred_element_type=F32).astype(BF16)
    v_ref[...] = jnp.dot(mn, w_ref[:, d:], preferred_element_type=F32).astype(BF16)


def _mem_kv(mem2d, norm_w, w_ckv, batch, mem_len):
    D = mem2d.shape[1]
    row = lambda b: (b, 0)
    fixed = lambda b: (0, 0)
    shp = jax.ShapeDtypeStruct((batch * mem_len, D), BF16)
    return pl.pallas_call(
        _mem_kv_kernel,
        grid=(batch,),
        in_specs=[pl.BlockSpec((mem_len, D), row), pl.BlockSpec((1, D), fixed),
                  pl.BlockSpec(w_ckv.shape, fixed)],
        out_specs=(pl.BlockSpec((mem_len, D), row), pl.BlockSpec((mem_len, D), row)),
        out_shape=(shp, shp),
        compiler_params=pltpu.CompilerParams(dimension_semantics=("parallel",),
                                             vmem_limit_bytes=VMEM_LIMIT),
        name="mem_kv",
    )(mem2d, norm_w, w_ckv)


def _out_cross_kernel(x_ref, og_ref, od_ref, wo_ref, ncw_ref, wq_ref, k_ref, v_ref, wco_ref,
                      fnw_ref, o_ref, ctx_s, *, final_norm):
    d = x_ref.shape[1]
    dh = d // CROSS_HEADS
    pm = x_ref.shape[0] // ROW_PARTS
    parts = [slice(p * pm, (p + 1) * pm) for p in range(ROW_PARTS)]
    x1 = [x_ref[r, :]
          + jnp.dot(og_ref[r, :], wo_ref[0:GLA_WIDTH, :], preferred_element_type=F32)
          + jnp.dot(od_ref[r, :], wo_ref[GLA_WIDTH:, :], preferred_element_type=F32)
          for r in parts]
    q = [(jnp.dot(_rms_scale(x, ncw_ref[...]).astype(BF16), wq_ref[...],
                  preferred_element_type=F32) * (dh ** -0.5 * LOG2E)).astype(BF16) for x in x1]
    def head_scores(hd, part):
        sl = slice(hd * dh, (hd + 1) * dh)
        return lax.dot_general(q[part][:, sl], k_ref[:, sl], NT_DIMS,
                               preferred_element_type=F32)

    items = [(hd, part) for hd in range(CROSS_HEADS) for part in range(ROW_PARTS)]
    s_next = head_scores(*items[0])
    for n, (hd, part) in enumerate(items):
        s = s_next
        if n + 1 < len(items):
            s_next = head_scores(*items[n + 1])
        sl = slice(hd * dh, (hd + 1) * dh)
        p = jnp.exp2(s - jnp.max(s, axis=-1, keepdims=True))
        ctx = jnp.dot(p.astype(BF16), v_ref[:, sl], preferred_element_type=F32)
        ctx_s[parts[part], sl] = (ctx / jnp.sum(p, axis=-1, keepdims=True)).astype(BF16)
    for r, x in zip(parts, x1):
        x2 = x + jnp.dot(ctx_s[r, :], wco_ref[...], preferred_element_type=F32)
        if final_norm:
            x2 = _rms_scale(x2, fnw_ref[...])
        o_ref[r, :] = x2


def _out_cross(x2d, og, od, w_out, ncw, w_cq, kc, vc, w_co, fnw, seq, mem_len, final_norm):
    T, D = x2d.shape
    tm = TM_OUT
    per_b = seq // tm
    row = lambda i: (i, 0)
    fixed = lambda i: (0, 0)
    bmap = lambda i: (i // per_b, 0)
    return pl.pallas_call(
        functools.partial(_out_cross_kernel, final_norm=final_norm),
        grid=(T // tm,),
        in_specs=[
            pl.BlockSpec((tm, D), row),
            pl.BlockSpec((tm, GLA_WIDTH), row),
            pl.BlockSpec((tm, DIFF_WIDTH), row),
            pl.BlockSpec(w_out.shape, fixed),
            pl.BlockSpec((1, D), fixed),
            pl.BlockSpec(w_cq.shape, fixed),
            pl.BlockSpec((mem_len, D), bmap),
            pl.BlockSpec((mem_len, D), bmap),
            pl.BlockSpec(w_co.shape, fixed),
            pl.BlockSpec((1, D), fixed),
        ],
        out_specs=pl.BlockSpec((tm, D), row),
        out_shape=jax.ShapeDtypeStruct((T, D), F32),
        scratch_shapes=[pltpu.VMEM((tm, D), BF16)],
        compiler_params=pltpu.CompilerParams(dimension_semantics=("parallel",),
                                             vmem_limit_bytes=VMEM_LIMIT),
        name="out_cross",
    )(x2d, og, od, w_out, ncw, w_cq, kc, vc, w_co, fnw)


def _rope_tables(positions):
    half = ROPE_DIM // 2
    T = positions.size
    per_row = LANES // half
    inv_freq = ROPE_THETA ** (-(jnp.arange(0, ROPE_DIM, 2, dtype=F32) / ROPE_DIM))
    pos = positions.reshape(T // per_row, per_row).astype(F32)
    ang = jnp.repeat(pos, half, axis=1) * jnp.tile(inv_freq, per_row)
    pieces = []
    rest_c, rest_s = jnp.cos(ang), jnp.sin(ang)
    for _ in range(ROPE_SPLIT):
        pc, ps = rest_c.astype(BF16), rest_s.astype(BF16)
        pieces += [pc.reshape(T, half), ps.reshape(T, half)]
        rest_c, rest_s = rest_c - pc.astype(F32), rest_s - ps.astype(F32)
    return jnp.concatenate(pieces, axis=1)


def _rope_select_matrix():
    half = ROPE_DIM // 2
    sel = np.zeros((ROPE_DIM, 2 * LANES), np.float32)
    for j in range(LANES):
        jm = j % DIFF_DQK
        if jm < ROPE_DIM:
            sel[jm % half, j] = 1.0
            sel[half + jm % half, LANES + j] = -1.0 if jm < half else 1.0
    return jnp.asarray(np.tile(sel, (ROPE_SPLIT, 1)), BF16)


def kernel(x, mem, positions, norm_mix_w, w_in, w_alpha2, b_alpha2, gla_norm_w,
           lam_q1, lam_k1, lam_q2, lam_k2, diff_norm_w, w_out,
           norm_cross_w, norm_mem_w, w_cq, w_ckv, w_co, final_norm_w):
    B, S, D = x.shape
    M = mem.shape[1]
    depth = w_in.shape[0]
    T = B * S
    cs_t = _rope_tables(positions)
    sel = _rope_select_matrix()
    x2d = x.reshape(T, D)
    mem2d = mem.reshape(B * M, D)

    c_gv = 2 * GLA_QK
    c_ga = c_gv + 2 * GLA_WIDTH
    c_dq = c_ga + GLA_LOWRANK

    for l in range(depth):
        wl = w_in[l]
        ga_pad = jnp.pad(wl[:, c_ga:c_dq], ((0, 0), (0, LANES - GLA_LOWRANK)))
        wa = jnp.concatenate([wl[:, :c_gv], ga_pad], axis=1).astype(BF16)
        c_dv = c_dq + 2 * DIFF_WIDTH
        c_dg = c_dv + DIFF_WIDTH
        wb = jnp.concatenate([wl[:, c_gv:c_ga], wl[:, c_dq:c_dv], wl[:, c_dg:]],
                             axis=1).astype(BF16)
        wvt = wl[:, c_dv:c_dg].T.astype(BF16)
        wa2 = jnp.pad(w_alpha2[l], ((0, LANES - GLA_LOWRANK), (0, 0))).astype(BF16)
        ba2 = b_alpha2[l].reshape(1, GLA_QK).astype(F32)

        gq, gk, gv, ggs, g, dq, dk, dvt, dgs = _in_proj(
            x2d, norm_mix_w[l].reshape(1, D), wa, wa2, ba2, wb, wvt, cs_t, sel)

        o_gla = _gla(gq, gk, g, gv, ggs, gla_norm_w[l].reshape(1, GLA_DV), B, S)

        lam_init = 0.8 - 0.6 * math.exp(-0.3 * l)
        lam_vecs = jnp.stack([lam_q1[l], lam_k1[l], lam_q2[l], lam_k2[l]]).astype(F32)
        o_diff = _diff_attn(lam_vecs, dq, dk, dvt, dgs, diff_norm_w[l].reshape(1, DIFF_DV),
                            B, S, lam_init)

        kc, vc = _mem_kv(mem2d, norm_mem_w[l].reshape(1, D), w_ckv[l].astype(BF16), B, M)

        x2d = _out_cross(x2d, o_gla, o_diff, w_out[l].astype(BF16),
                         norm_cross_w[l].reshape(1, D), w_cq[l].astype(BF16), kc, vc,
                         w_co[l].astype(BF16), final_norm_w.reshape(1, D), S, M,
                         final_norm=(l == depth - 1))
    return x2d.reshape(B, S, D)
```

```python
import functools
import math

import jax
import jax.numpy as jnp
import numpy as np
from jax import lax
from jax.experimental import pallas as pl
from jax.experimental.pallas import tpu as pltpu

F32 = jnp.float32
BF16 = jnp.bfloat16

CHUNK = 64
GLA_HEADS = 4
GLA_DK = 64
GLA_DV = 128
GLA_QK = GLA_HEADS * GLA_DK
GLA_WIDTH = GLA_HEADS * GLA_DV
GLA_LOWRANK = 16
GLA_TAU = 16.0
DIFF_HEADS = 4
DIFF_DQK = 64
DIFF_DV = 128
DIFF_WIDTH = DIFF_HEADS * DIFF_DV
ROPE_DIM = 16
ROPE_THETA = 500000.0
CROSS_HEADS = 4
EPS = 1e-6
LOG2E = math.log2(math.e)
ROPE_SPLIT = 3

LANES = 128

TM_PROJ = 1024
TM_GLA = 1024
TK_ATT = 256
SUM_ROWS = 16
VT_ROWS = DIFF_DV + SUM_ROWS
SCORES_AHEAD = 3
TM_OUT = 1024
ROW_PARTS = 2
VMEM_LIMIT = 56 * 1024 * 1024

NT_DIMS = (((1,), (1,)), ((), ()))
TN_DIMS = (((0,), (0,)), ((), ()))


def _rms_scale(xf, w):
    return xf * lax.rsqrt(jnp.mean(xf * xf, axis=-1, keepdims=True) + EPS) * w


def _silu(t):
    return t * (1.0 / (1.0 + jnp.exp(-t)))


def _in_proj_kernel(x_ref, nw_ref, wa_ref, wa2_ref, ba2_ref, wb_ref, wvt_ref, cs_ref, sel_ref,
                    gq_ref, gk_ref, gv_ref, ggs_ref, g_ref, dq_ref, dk_ref, dvt_ref, dgs_ref):
    pm = x_ref.shape[0] // ROW_PARTS
    for part in range(ROW_PARTS):
        _in_proj_rows(slice(part * pm, (part + 1) * pm),
                      x_ref, nw_ref, wa_ref, wa2_ref, ba2_ref, wb_ref, wvt_ref, cs_ref, sel_ref,
                      gq_ref, gk_ref, gv_ref, ggs_ref, g_ref, dq_ref, dk_ref, dvt_ref, dgs_ref)


def _in_proj_rows(rows, x_ref, nw_ref, wa_ref, wa2_ref, ba2_ref, wb_ref, wvt_ref, cs_ref, sel_ref,
                  gq_ref, gk_ref, gv_ref, ggs_ref, g_ref, dq_ref, dk_ref, dvt_ref, dgs_ref):
    h = _rms_scale(x_ref[rows, :], nw_ref[...]).astype(BF16)

    def proj_b(group):
        return jnp.dot(h, wb_ref[:, group * GLA_WIDTH:(group + 1) * GLA_WIDTH],
                       preferred_element_type=F32)

    qkg = jnp.dot(h, wa_ref[...], preferred_element_type=F32)
    gq_ref[rows, :] = qkg[:, 0:GLA_QK] * (GLA_DK ** -0.5)
    gk_ref[rows, :] = qkg[:, GLA_QK:2 * GLA_QK]
    ga = qkg[:, 2 * GLA_QK:].astype(BF16)

    gv_ref[rows, :] = proj_b(0).astype(BF16)
    ggs_ref[rows, :] = _silu(proj_b(1)).astype(BF16)

    tabs = jnp.dot(cs_ref[rows, :], sel_ref[...], preferred_element_type=F32)
    lane = lax.broadcasted_iota(jnp.int32, (tabs.shape[0], LANES), 1)
    cos = tabs[:, :LANES] + jnp.where((lane % DIFF_DQK) < ROPE_DIM, 0.0, 1.0)
    sin = tabs[:, LANES:]
    first_half = (lane % DIFF_DQK) < (ROPE_DIM // 2)

    def rope(t):
        partner = jnp.where(first_half, pltpu.roll(t, LANES - ROPE_DIM // 2, 1),
                            pltpu.roll(t, ROPE_DIM // 2, 1))
        return t * cos + partner * sin

    q = proj_b(2)
    k = proj_b(3)
    for hd in range(DIFF_HEADS):
        sl = slice(hd * DIFF_DV, (hd + 1) * DIFF_DV)
        qr = rope(q[:, sl]) * (DIFF_DQK ** -0.5 * LOG2E)
        q0 = jnp.where(lane < DIFF_DQK, qr, 0.0).astype(BF16)
        q1 = jnp.where(lane >= DIFF_DQK, qr, 0.0).astype(BF16)
        for r in range(qr.shape[0] // TK_ATT):
            src = slice(r * TK_ATT, (r + 1) * TK_ATT)
            base = 2 * (rows.start + r * TK_ATT)
            dq_ref[base:base + TK_ATT, sl] = q0[src, :]
            dq_ref[base + TK_ATT:base + 2 * TK_ATT, sl] = q1[src, :]
        dk_ref[rows, sl] = rope(k[:, sl]).astype(BF16)
    dvt = lax.dot_general(wvt_ref[...], h, NT_DIMS, preferred_element_type=F32).astype(BF16)
    for hd in range(DIFF_HEADS):
        base = hd * VT_ROWS
        dvt_ref[base:base + DIFF_DV, rows] = dvt[hd * DIFF_DV:(hd + 1) * DIFF_DV, :]
        dvt_ref[base + DIFF_DV:base + VT_ROWS, rows] = jnp.ones((SUM_ROWS, dvt.shape[1]), BF16)

    logit = jnp.dot(ga, wa2_ref[...], preferred_element_type=F32) + ba2_ref[...]
    log_sig = jnp.minimum(logit, 0.0) - jnp.log1p(jnp.exp(-jnp.abs(logit)))
    g_ref[rows, :] = log_sig * (1.0 / GLA_TAU)

    dgs_ref[rows, :] = _silu(proj_b(4)).astype(BF16)


def _in_proj(x2d, norm_w, wa, wa2, ba2, wb, wvt, cs_t, sel):
    T, D = x2d.shape
    tm = TM_PROJ
    row = lambda i: (i, 0)
    col = lambda i: (0, i)
    fixed = lambda i: (0, 0)
    out_shapes = (
        jax.ShapeDtypeStruct((T, GLA_QK), F32),
        jax.ShapeDtypeStruct((T, GLA_QK), F32),
        jax.ShapeDtypeStruct((T, GLA_WIDTH), BF16),
        jax.ShapeDtypeStruct((T, GLA_WIDTH), BF16),
        jax.ShapeDtypeStruct((T, GLA_QK), F32),
        jax.ShapeDtypeStruct((2 * T, DIFF_WIDTH), BF16),
        jax.ShapeDtypeStruct((T, DIFF_WIDTH), BF16),
        jax.ShapeDtypeStruct((DIFF_HEADS * VT_ROWS, T), BF16),
        jax.ShapeDtypeStruct((T, DIFF_WIDTH), BF16),
    )
    out_specs = [pl.BlockSpec((tm, s.shape[1]), row) for s in out_shapes]
    out_specs[5] = pl.BlockSpec((2 * tm, DIFF_WIDTH), row)
    out_specs[7] = pl.BlockSpec((DIFF_HEADS * VT_ROWS, tm), col)
    return pl.pallas_call(
        _in_proj_kernel,
        grid=(T // tm,),
        in_specs=[
            pl.BlockSpec((tm, D), row),
            pl.BlockSpec((1, D), fixed),
            pl.BlockSpec(wa.shape, fixed),
            pl.BlockSpec(wa2.shape, fixed),
            pl.BlockSpec(ba2.shape, fixed),
            pl.BlockSpec(wb.shape, fixed),
            pl.BlockSpec(wvt.shape, fixed),
            pl.BlockSpec((tm, cs_t.shape[1]), row),
            pl.BlockSpec(sel.shape, fixed),
        ],
        out_specs=tuple(out_specs),
        out_shape=out_shapes,
        compiler_params=pltpu.CompilerParams(dimension_semantics=("parallel",),
                                             vmem_limit_bytes=VMEM_LIMIT),
        name="in_proj",
    )(x2d, norm_w, wa, wa2, ba2, wb, wvt, cs_t, sel)


def _gla_kernel(gq_ref, gk_ref, g_ref, gv_ref, ggs_ref, nw_ref, o_ref,
                qd4_s, ki_s, kl4_s, dec_s, state_s):
    tm = gq_ref.shape[0]
    n_chunks = tm // CHUNK

    @pl.when(pl.program_id(1) == 0)
    def _():
        state_s[...] = jnp.zeros_like(state_s)

    g = g_ref[...]
    row_in_chunk = lax.broadcasted_iota(jnp.int32, g.shape, 0) % CHUNK
    G = g
    shift = 1
    while shift < CHUNK:
        G = G + jnp.where(row_in_chunk >= shift, pltpu.roll(G, shift, 0), 0.0)
        shift *= 2
    q = gq_ref[...]
    k = gk_ref[...]
    qd = (q * jnp.exp(G)).astype(BF16)
    ki_s[...] = (k * jnp.exp(-G)).astype(BF16)

    lane_head = lax.broadcasted_iota(jnp.int32, (CHUNK, GLA_QK), 1) // GLA_DK
    zero = jnp.zeros((CHUNK, GLA_QK), BF16)
    for c in range(n_chunks):
        rows = slice(c * CHUNK, (c + 1) * CHUNK)
        g_last = G[(c + 1) * CHUNK - 1:(c + 1) * CHUNK, :]
        kl = (k[rows, :] * jnp.exp(g_last - G[rows, :])).astype(BF16)
        dec_s[c:c + 1, :] = jnp.exp(g_last)
        for hd in range(GLA_HEADS):
            dst = slice((c * GLA_HEADS + hd) * CHUNK, (c * GLA_HEADS + hd + 1) * CHUNK)
            qd4_s[dst, :] = jnp.where(lane_head == hd, qd[rows, :], zero)
            kl4_s[dst, :] = jnp.where(lane_head == hd, kl, zero)

    r_i = lax.broadcasted_iota(jnp.int32, (GLA_HEADS * CHUNK, GLA_HEADS * CHUNK), 0)
    c_i = lax.broadcasted_iota(jnp.int32, (GLA_HEADS * CHUNK, GLA_HEADS * CHUNK), 1)
    keep = (r_i // CHUNK == c_i // CHUNK) & (c_i % CHUNK <= r_i % CHUNK)
    nw = nw_ref[...]

    def chunk_rows4(c):
        return slice(c * GLA_HEADS * CHUNK, (c + 1) * GLA_HEADS * CHUNK)

    def chunk_values(c):
        rows = slice(c * CHUNK, (c + 1) * CHUNK)
        return jnp.concatenate([gv_ref[rows, hd * GLA_DV:(hd + 1) * GLA_DV]
                                for hd in range(GLA_HEADS)], axis=0)

    def state_free_products(c):
        ki = ki_s[c * CHUNK:(c + 1) * CHUNK, :]
        k4 = jnp.concatenate([ki] * GLA_HEADS, axis=0)
        scores = lax.dot_general(qd4_s[chunk_rows4(c), :], k4, NT_DIMS, preferred_element_type=F32)
        kv_t = lax.dot_general(chunk_values(c), kl4_s[chunk_rows4(c), :], TN_DIMS,
                               preferred_element_type=F32)
        return scores, kv_t

    ahead = state_free_products(0)
    for c in range(n_chunks):
        rows = slice(c * CHUNK, (c + 1) * CHUNK)
        scores, kv_t = ahead
        if c + 1 < n_chunks:
            ahead = state_free_products(c + 1)
        q4 = qd4_s[chunk_rows4(c), :]
        st = state_s[...]
        scores = jnp.where(keep, scores, 0.0).astype(BF16)
        o = (jnp.dot(scores, chunk_values(c), preferred_element_type=F32)
             + lax.dot_general(q4, st.astype(BF16), NT_DIMS, preferred_element_type=F32))
        state_s[...] = dec_s[c:c + 1, :] * st + kv_t
        gate = jnp.concatenate([ggs_ref[rows, hd * GLA_DV:(hd + 1) * GLA_DV]
                                for hd in range(GLA_HEADS)], axis=0).astype(F32)
        o = (_rms_scale(o, nw) * gate).astype(BF16)
        for hd in range(GLA_HEADS):
            o_ref[rows, hd * GLA_DV:(hd + 1) * GLA_DV] = o[hd * CHUNK:(hd + 1) * CHUNK, :]


def _gla(gq, gk, g, gv, ggs, norm_w, batch, seq):
    tm = TM_GLA
    nt = seq // tm
    row = lambda b, i: (b * nt + i, 0)
    fixed = lambda b, i: (0, 0)
    T = gq.shape[0]
    return pl.pallas_call(
        _gla_kernel,
        grid=(batch, nt),
        in_specs=[
            pl.BlockSpec((tm, GLA_QK), row),
            pl.BlockSpec((tm, GLA_QK), row),
            pl.BlockSpec((tm, GLA_QK), row),
            pl.BlockSpec((tm, GLA_WIDTH), row),
            pl.BlockSpec((tm, GLA_WIDTH), row),
            pl.BlockSpec((1, GLA_DV), fixed),
        ],
        out_specs=pl.BlockSpec((tm, GLA_WIDTH), row),
        out_shape=jax.ShapeDtypeStruct((T, GLA_WIDTH), BF16),
        scratch_shapes=[
            pltpu.VMEM((GLA_HEADS * tm, GLA_QK), BF16),
            pltpu.VMEM((tm, GLA_QK), BF16),
            pltpu.VMEM((GLA_HEADS * tm, GLA_QK), BF16),
            pltpu.VMEM((tm // CHUNK, GLA_QK), F32),
            pltpu.VMEM((GLA_DV, GLA_QK), F32),
        ],
        compiler_params=pltpu.CompilerParams(dimension_semantics=("parallel", "arbitrary"),
                                             vmem_limit_bytes=VMEM_LIMIT),
        name="gla",
    )(gq, gk, g, gv, ggs, norm_w)


def _diff_attn_kernel(lam_ref, qq_ref, k_ref, vt_ref, gs_ref, nw_ref, o_ref, *, lam_init):
    seq = k_ref.shape[0]
    tk = TK_ATT
    n_blk = seq // tk
    cw = 2 * tk

    lv = lam_ref[...]
    lam = (jnp.exp(jnp.sum(lv[0:1] * lv[1:2], axis=-1, keepdims=True))
           - jnp.exp(jnp.sum(lv[2:3] * lv[3:4], axis=-1, keepdims=True)) + lam_init)
    nw = nw_ref[...]

    qc = (lax.broadcasted_iota(jnp.int32, (cw, tk), 0) % tk) // CHUNK
    kc = lax.broadcasted_iota(jnp.int32, (cw, tk), 1) // CHUNK
    stair_bias = jnp.where(kc <= qc, 0.0, -jnp.inf).astype(F32)

    def scores(cb):
        nk = (cb + 1) * tk
        s = lax.dot_general(qq_ref[cb * cw:(cb + 1) * cw, :], k_ref[0:nk, :], NT_DIMS,
                            preferred_element_type=F32)
        parts = [s[:, j * tk:(j + 1) * tk] for j in range(cb)] + [s[:, cb * tk:] + stair_bias]
        mm = parts[0]
        for part in parts[1:]:
            mm = jnp.maximum(mm, part)
        return parts, jnp.max(mm, axis=-1, keepdims=True)

    def finish(cb, parts, m):
        nk = (cb + 1) * tk
        p = jnp.concatenate([jnp.exp2(part - m).astype(BF16) for part in parts], axis=1)
        pv = lax.dot_general(p, vt_ref[:, 0:nk], NT_DIMS, preferred_element_type=F32)
        o0 = pv[0:tk, 0:DIFF_DV] / pv[0:tk, DIFF_DV:DIFF_DV + 1]
        o1 = pv[tk:, 0:DIFF_DV] / pv[tk:, DIFF_DV:DIFF_DV + 1]
        o = o0 - lam * o1
        rows = slice(cb * tk, (cb + 1) * tk)
        o = _rms_scale(o, nw) * (1.0 - lam_init) * gs_ref[rows, :].astype(F32)
        o_ref[rows, :] = o.astype(BF16)

    pending = [scores(cb) for cb in range(SCORES_AHEAD)]
    for cb in range(n_blk):
        if cb + SCORES_AHEAD < n_blk:
            pending.append(scores(cb + SCORES_AHEAD))
        finish(cb, *pending.pop(0))


def _diff_attn(lam_vecs, dq, dk, dvt, dgs, norm_w, batch, seq, lam_init):
    assert TK_ATT % CHUNK == 0 and seq % TK_ATT == 0
    T = dk.shape[0]
    bh = lambda b, h: (b, h)
    hb = lambda b, h: (h, b)
    fixed = lambda b, h: (0, 0)
    blk = pl.BlockSpec((seq, DIFF_DV), bh)
    return pl.pallas_call(
        functools.partial(_diff_attn_kernel, lam_init=lam_init),
        grid=(batch, DIFF_HEADS),
        in_specs=[pl.BlockSpec(lam_vecs.shape, fixed),
                  pl.BlockSpec((2 * seq, DIFF_DV), bh), blk,
                  pl.BlockSpec((VT_ROWS, seq), hb), blk,
                  pl.BlockSpec((1, DIFF_DV), fixed)],
        out_specs=blk,
        out_shape=jax.ShapeDtypeStruct((T, DIFF_WIDTH), BF16),
        compiler_params=pltpu.CompilerParams(dimension_semantics=("parallel", "parallel"),
                                             vmem_limit_bytes=VMEM_LIMIT),
        name="diff_attn",
    )(lam_vecs, dq, dk, dvt, dgs, norm_w)


def _mem_kv_kernel(mem_ref, nw_ref, w_ref, k_ref, v_ref):
    d = mem_ref.shape[1]
    mn = _rms_scale(mem_ref[...], nw_ref[...]).astype(BF16)
    k_ref[...] = jnp.dot(mn, w_ref[:, :d], preferred_element_type=F32).astype(BF16)
    v_ref[...] = jnp.dot(mn, w_ref[:, d:], preferred_element_type=F32).astype(BF16)


def _mem_kv(mem2d, norm_w, w_ckv, batch, mem_len):
    D = mem2d.shape[1]
    row = lambda b: (b, 0)
    fixed = lambda b: (0, 0)
    shp = jax.ShapeDtypeStruct((batch * mem_len, D), BF16)
    return pl.pallas_call(
        _mem_kv_kernel,
        grid=(batch,),
        in_specs=[pl.BlockSpec((mem_len, D), row), pl.BlockSpec((1, D), fixed),
                  pl.BlockSpec(w_ckv.shape, fixed)],
        out_specs=(pl.BlockSpec((mem_len, D), row), pl.BlockSpec((mem_len, D), row)),
        out_shape=(shp, shp),
        compiler_params=pltpu.CompilerParams(dimension_semantics=("parallel",),
                                             vmem_limit_bytes=VMEM_LIMIT),
        name="mem_kv",
    )(mem2d, norm_w, w_ckv)


def _out_cross_kernel(x_ref, og_ref, od_ref, wo_ref, ncw_ref, wq_ref, k_ref, v_ref, wco_ref,
                      fnw_ref, o_ref, ctx_s, *, final_norm):
    d = x_ref.shape[1]
    dh = d // CROSS_HEADS
    pm = x_ref.shape[0] // ROW_PARTS
    parts = [slice(p * pm, (p + 1) * pm) for p in range(ROW_PARTS)]
    x1 = [x_ref[r, :]
          + jnp.dot(og_ref[r, :], wo_ref[0:GLA_WIDTH, :], preferred_element_type=F32)
          + jnp.dot(od_ref[r, :], wo_ref[GLA_WIDTH:, :], preferred_element_type=F32)
          for r in parts]
    q = [(jnp.dot(_rms_scale(x, ncw_ref[...]).astype(BF16), wq_ref[...],
                  preferred_element_type=F32) * (dh ** -0.5 * LOG2E)).astype(BF16) for x in x1]
    def head_scores(hd, part):
        sl = slice(hd * dh, (hd + 1) * dh)
        return lax.dot_general(q[part][:, sl], k_ref[:, sl], NT_DIMS,
                               preferred_element_type=F32)

    items = [(hd, part) for hd in range(CROSS_HEADS) for part in range(ROW_PARTS)]
    s_next = head_scores(*items[0])
    for n, (hd, part) in enumerate(items):
        s = s_next
        if n + 1 < len(items):
            s_next = head_scores(*items[n + 1])
        sl = slice(hd * dh, (hd + 1) * dh)
        p = jnp.exp2(s - jnp.max(s, axis=-1, keepdims=True))
        ctx = jnp.dot(p.astype(BF16), v_ref[:, sl], preferred_element_type=F32)
        ctx_s[parts[part], sl] = (ctx / jnp.sum(p, axis=-1, keepdims=True)).astype(BF16)
    for r, x in zip(parts, x1):
        x2 = x + jnp.dot(ctx_s[r, :], wco_ref[...], preferred_element_type=F32)
        if final_norm:
            x2 = _rms_scale(x2, fnw_ref[...])
        o_ref[r, :] = x2


def _out_cross(x2d, og, od, w_out, ncw, w_cq, kc, vc, w_co, fnw, seq, mem_len, final_norm):
    T, D = x2d.shape
    tm = TM_OUT
    per_b = seq // tm
    row = lambda i: (i, 0)
    fixed = lambda i: (0, 0)
    bmap = lambda i: (i // per_b, 0)
    return pl.pallas_call(
        functools.partial(_out_cross_kernel, final_norm=final_norm),
        grid=(T // tm,),
        in_specs=[
            pl.BlockSpec((tm, D), row),
            pl.BlockSpec((tm, GLA_WIDTH), row),
            pl.BlockSpec((tm, DIFF_WIDTH), row),
            pl.BlockSpec(w_out.shape, fixed),
            pl.BlockSpec((1, D), fixed),
            pl.BlockSpec(w_cq.shape, fixed),
            pl.BlockSpec((mem_len, D), bmap),
            pl.BlockSpec((mem_len, D), bmap),
            pl.BlockSpec(w_co.shape, fixed),
            pl.BlockSpec((1, D), fixed),
        ],
        out_specs=pl.BlockSpec((tm, D), row),
        out_shape=jax.ShapeDtypeStruct((T, D), F32),
        scratch_shapes=[pltpu.VMEM((tm, D), BF16)],
        compiler_params=pltpu.CompilerParams(dimension_semantics=("parallel",),
                                             vmem_limit_bytes=VMEM_LIMIT),
        name="out_cross",
    )(x2d, og, od, w_out, ncw, w_cq, kc, vc, w_co, fnw)


def _rope_tables(positions):
    half = ROPE_DIM // 2
    T = positions.size
    per_row = LANES // half
    inv_freq = ROPE_THETA ** (-(jnp.arange(0, ROPE_DIM, 2, dtype=F32) / ROPE_DIM))
    pos = positions.reshape(T // per_row, per_row).astype(F32)
    ang = jnp.repeat(pos, half, axis=1) * jnp.tile(inv_freq, per_row)
    pieces = []
    rest_c, rest_s = jnp.cos(ang), jnp.sin(ang)
    for _ in range(ROPE_SPLIT):
        pc, ps = rest_c.astype(BF16), rest_s.astype(BF16)
        pieces += [pc.reshape(T, half), ps.reshape(T, half)]
        rest_c, rest_s = rest_c - pc.astype(F32), rest_s - ps.astype(F32)
    return jnp.concatenate(pieces, axis=1)


def _rope_select_matrix():
    half = ROPE_DIM // 2
    sel = np.zeros((ROPE_DIM, 2 * LANES), np.float32)
    for j in range(LANES):
        jm = j % DIFF_DQK
        if jm < ROPE_DIM:
            sel[jm % half, j] = 1.0
            sel[half + jm % half, LANES + j] = -1.0 if jm < half else 1.0
    return jnp.asarray(np.tile(sel, (ROPE_SPLIT, 1)), BF16)


def kernel(x, mem, positions, norm_mix_w, w_in, w_alpha2, b_alpha2, gla_norm_w,
           lam_q1, lam_k1, lam_q2, lam_k2, diff_norm_w, w_out,
           norm_cross_w, norm_mem_w, w_cq, w_ckv, w_co, final_norm_w):
    B, S, D = x.shape
    M = mem.shape[1]
    depth = w_in.shape[0]
    T = B * S
    cs_t = _rope_tables(positions)
    sel = _rope_select_matrix()
    x2d = x.reshape(T, D)
    mem2d = mem.reshape(B * M, D)

    c_gv = 2 * GLA_QK
    c_ga = c_gv + 2 * GLA_WIDTH
    c_dq = c_ga + GLA_LOWRANK

    for l in range(depth):
        wl = w_in[l]
        ga_pad = jnp.pad(wl[:, c_ga:c_dq], ((0, 0), (0, LANES - GLA_LOWRANK)))
        wa = jnp.concatenate([wl[:, :c_gv], ga_pad], axis=1).astype(BF16)
        c_dv = c_dq + 2 * DIFF_WIDTH
        c_dg = c_dv + DIFF_WIDTH
        wb = jnp.concatenate([wl[:, c_gv:c_ga], wl[:, c_dq:c_dv], wl[:, c_dg:]],
                             axis=1).astype(BF16)
        wvt = wl[:, c_dv:c_dg].T.astype(BF16)
        wa2 = jnp.pad(w_alpha2[l], ((0, LANES - GLA_LOWRANK), (0, 0))).astype(BF16)
        ba2 = b_alpha2[l].reshape(1, GLA_QK).astype(F32)

        gq, gk, gv, ggs, g, dq, dk, dvt, dgs = _in_proj(
            x2d, norm_mix_w[l].reshape(1, D), wa, wa2, ba2, wb, wvt, cs_t, sel)

        o_gla = _gla(gq, gk, g, gv, ggs, gla_norm_w[l].reshape(1, GLA_DV), B, S)

        lam_init = 0.8 - 0.6 * math.exp(-0.3 * l)
        lam_vecs = jnp.stack([lam_q1[l], lam_k1[l], lam_q2[l], lam_k2[l]]).astype(F32)
        o_diff = _diff_attn(lam_vecs, dq, dk, dvt, dgs, diff_norm_w[l].reshape(1, DIFF_DV),
                            B, S, lam_init)

        kc, vc = _mem_kv(mem2d, norm_mem_w[l].reshape(1, D), w_ckv[l].astype(BF16), B, M)

        x2d = _out_cross(x2d, o_gla, o_diff, w_out[l].astype(BF16),
                         norm_cross_w[l].reshape(1, D), w_cq[l].astype(BF16), kc, vc,
                         w_co[l].astype(BF16), final_norm_w.reshape(1, D), S, M,
                         final_norm=(l == depth - 1))
    return x2d.reshape(B, S, D)
```

```python
import functools
import math

import jax
import jax.numpy as jnp
import numpy as np
from jax import lax
from jax.experimental import pallas as pl
from jax.experimental.pallas import tpu as pltpu

F32 = jnp.float32
BF16 = jnp.bfloat16

CHUNK = 64
GLA_HEADS = 4
GLA_DK = 64
GLA_DV = 128
GLA_QK = GLA_HEADS * GLA_DK
GLA_WIDTH = GLA_HEADS * GLA_DV
GLA_LOWRANK = 16
GLA_TAU = 16.0
DIFF_HEADS = 4
DIFF_DQK = 64
DIFF_DV = 128
DIFF_WIDTH = DIFF_HEADS * DIFF_DV
ROPE_DIM = 16
ROPE_THETA = 500000.0
CROSS_HEADS = 4
EPS = 1e-6
LOG2E = math.log2(math.e)
ROPE_SPLIT = 3

LANES = 128

TM_PROJ = 1024
TM_GLA = 2048
TK_ATT = 256
SUM_ROWS = 16
VT_ROWS = DIFF_DV + SUM_ROWS
SCORES_AHEAD = 3
TM_OUT = 1024
ROW_PARTS = 2
OUT_ROW_PARTS = 2
VMEM_LIMIT = 56 * 1024 * 1024

NT_DIMS = (((1,), (1,)), ((), ()))
TN_DIMS = (((0,), (0,)), ((), ()))


def _rms_scale(xf, w):
    return xf * lax.rsqrt(jnp.mean(xf * xf, axis=-1, keepdims=True) + EPS) * w


def _silu(t):
    return t * (1.0 / (1.0 + jnp.exp(-t)))


def _in_proj_kernel(x_ref, nw_ref, wa_ref, wa2_ref, ba2_ref, wb_ref, wvt_ref, cs_ref, sel_ref,
                    gq_ref, gk_ref, gv_ref, ggs_ref, g_ref, dq_ref, dk_ref, dvt_ref, dgs_ref):
    pm = x_ref.shape[0] // ROW_PARTS
    for part in range(ROW_PARTS):
        _in_proj_rows(slice(part * pm, (part + 1) * pm),
                      x_ref, nw_ref, wa_ref, wa2_ref, ba2_ref, wb_ref, wvt_ref, cs_ref, sel_ref,
                      gq_ref, gk_ref, gv_ref, ggs_ref, g_ref, dq_ref, dk_ref, dvt_ref, dgs_ref)


def _in_proj_rows(rows, x_ref, nw_ref, wa_ref, wa2_ref, ba2_ref, wb_ref, wvt_ref, cs_ref, sel_ref,
                  gq_ref, gk_ref, gv_ref, ggs_ref, g_ref, dq_ref, dk_ref, dvt_ref, dgs_ref):
    h = _rms_scale(x_ref[rows, :], nw_ref[...]).astype(BF16)

    def proj_b(group):
        return jnp.dot(h, wb_ref[:, group * GLA_WIDTH:(group + 1) * GLA_WIDTH],
                       preferred_element_type=F32)

    qkg = jnp.dot(h, wa_ref[...], preferred_element_type=F32)
    gq_ref[rows, :] = qkg[:, 0:GLA_QK] * (GLA_DK ** -0.5)
    gk_ref[rows, :] = qkg[:, GLA_QK:2 * GLA_QK]
    ga = qkg[:, 2 * GLA_QK:].astype(BF16)

    gv_ref[rows, :] = proj_b(0).astype(BF16)
    ggs_ref[rows, :] = _silu(proj_b(1)).astype(BF16)

    tabs = jnp.dot(cs_ref[rows, :], sel_ref[...], preferred_element_type=F32)
    lane = lax.broadcasted_iota(jnp.int32, (tabs.shape[0], LANES), 1)
    cos = tabs[:, :LANES] + jnp.where((lane % DIFF_DQK) < ROPE_DIM, 0.0, 1.0)
    sin = tabs[:, LANES:]
    first_half = (lane % DIFF_DQK) < (ROPE_DIM // 2)

    def rope(t):
        partner = jnp.where(first_half, pltpu.roll(t, LANES - ROPE_DIM // 2, 1),
                            pltpu.roll(t, ROPE_DIM // 2, 1))
        return t * cos + partner * sin

    q = proj_b(2)
    k = proj_b(3)
    for hd in range(DIFF_HEADS):
        sl = slice(hd * DIFF_DV, (hd + 1) * DIFF_DV)
        qr = rope(q[:, sl]) * (DIFF_DQK ** -0.5 * LOG2E)
        q0 = jnp.where(lane < DIFF_DQK, qr, 0.0).astype(BF16)
        q1 = jnp.where(lane >= DIFF_DQK, qr, 0.0).astype(BF16)
        for r in range(qr.shape[0] // TK_ATT):
            src = slice(r * TK_ATT, (r + 1) * TK_ATT)
            base = 2 * (rows.start + r * TK_ATT)
            dq_ref[base:base + TK_ATT, sl] = q0[src, :]
            dq_ref[base + TK_ATT:base + 2 * TK_ATT, sl] = q1[src, :]
        dk_ref[rows, sl] = rope(k[:, sl]).astype(BF16)
    dvt = lax.dot_general(wvt_ref[...], h, NT_DIMS, preferred_element_type=F32).astype(BF16)
    for hd in range(DIFF_HEADS):
        base = hd * VT_ROWS
        dvt_ref[base:base + DIFF_DV, rows] = dvt[hd * DIFF_DV:(hd + 1) * DIFF_DV, :]
        dvt_ref[base + DIFF_DV:base + VT_ROWS, rows] = jnp.ones((SUM_ROWS, dvt.shape[1]), BF16)

    logit = jnp.dot(ga, wa2_ref[...], preferred_element_type=F32) + ba2_ref[...]
    log_sig = jnp.minimum(logit, 0.0) - jnp.log1p(jnp.exp(-jnp.abs(logit)))
    g_ref[rows, :] = log_sig * (1.0 / GLA_TAU)

    dgs_ref[rows, :] = _silu(proj_b(4)).astype(BF16)


def _in_proj(x2d, norm_w, wa, wa2, ba2, wb, wvt, cs_t, sel):
    T, D = x2d.shape
    tm = TM_PROJ
    row = lambda i: (i, 0)
    col = lambda i: (0, i)
    fixed = lambda i: (0, 0)
    out_shapes = (
        jax.ShapeDtypeStruct((T, GLA_QK), F32),
        jax.ShapeDtypeStruct((T, GLA_QK), F32),
        jax.ShapeDtypeStruct((T, GLA_WIDTH), BF16),
        jax.ShapeDtypeStruct((T, GLA_WIDTH), BF16),
        jax.ShapeDtypeStruct((T, GLA_QK), F32),
        jax.ShapeDtypeStruct((2 * T, DIFF_WIDTH), BF16),
        jax.ShapeDtypeStruct((T, DIFF_WIDTH), BF16),
        jax.ShapeDtypeStruct((DIFF_HEADS * VT_ROWS, T), BF16),
        jax.ShapeDtypeStruct((T, DIFF_WIDTH), BF16),
    )
    out_specs = [pl.BlockSpec((tm, s.shape[1]), row) for s in out_shapes]
    out_specs[5] = pl.BlockSpec((2 * tm, DIFF_WIDTH), row)
    out_specs[7] = pl.BlockSpec((DIFF_HEADS * VT_ROWS, tm), col)
    return pl.pallas_call(
        _in_proj_kernel,
        grid=(T // tm,),
        in_specs=[
            pl.BlockSpec((tm, D), row),
            pl.BlockSpec((1, D), fixed),
            pl.BlockSpec(wa.shape, fixed),
            pl.BlockSpec(wa2.shape, fixed),
            pl.BlockSpec(ba2.shape, fixed),
            pl.BlockSpec(wb.shape, fixed),
            pl.BlockSpec(wvt.shape, fixed),
            pl.BlockSpec((tm, cs_t.shape[1]), row),
            pl.BlockSpec(sel.shape, fixed),
        ],
        out_specs=tuple(out_specs),
        out_shape=out_shapes,
        compiler_params=pltpu.CompilerParams(dimension_semantics=("parallel",),
                                             vmem_limit_bytes=VMEM_LIMIT),
        name="in_proj",
    )(x2d, norm_w, wa, wa2, ba2, wb, wvt, cs_t, sel)


def _gla_kernel(gq_ref, gk_ref, g_ref, gv_ref, ggs_ref, nw_ref, o_ref,
                qd4_s, ki_s, kl4_s, dec_s, state_s):
    tm = gq_ref.shape[0]
    n_chunks = tm // CHUNK

    @pl.when(pl.program_id(1) == 0)
    def _():
        state_s[...] = jnp.zeros_like(state_s)

    g = g_ref[...]
    row_in_chunk = lax.broadcasted_iota(jnp.int32, g.shape, 0) % CHUNK
    G = g
    shift = 1
    while shift < CHUNK:
        G = G + jnp.where(row_in_chunk >= shift, pltpu.roll(G, shift, 0), 0.0)
        shift *= 2
    q = gq_ref[...]
    k = gk_ref[...]
    qd = (q * jnp.exp(G)).astype(BF16)
    ki_s[...] = (k * jnp.exp(-G)).astype(BF16)

    lane_head = lax.broadcasted_iota(jnp.int32, (CHUNK, GLA_QK), 1) // GLA_DK
    zero = jnp.zeros((CHUNK, GLA_QK), BF16)
    for c in range(n_chunks):
        rows = slice(c * CHUNK, (c + 1) * CHUNK)
        g_last = G[(c + 1) * CHUNK - 1:(c + 1) * CHUNK, :]
        kl = (k[rows, :] * jnp.exp(g_last - G[rows, :])).astype(BF16)
        dec_s[c:c + 1, :] = jnp.exp(g_last)
        for hd in range(GLA_HEADS):
            dst = slice((c * GLA_HEADS + hd) * CHUNK, (c * GLA_HEADS + hd + 1) * CHUNK)
            qd4_s[dst, :] = jnp.where(lane_head == hd, qd[rows, :], zero)
            kl4_s[dst, :] = jnp.where(lane_head == hd, kl, zero)

    r_i = lax.broadcasted_iota(jnp.int32, (GLA_HEADS * CHUNK, GLA_HEADS * CHUNK), 0)
    c_i = lax.broadcasted_iota(jnp.int32, (GLA_HEADS * CHUNK, GLA_HEADS * CHUNK), 1)
    keep = (r_i // CHUNK == c_i // CHUNK) & (c_i % CHUNK <= r_i % CHUNK)
    nw = nw_ref[...]

    def chunk_rows4(c):
        return slice(c * GLA_HEADS * CHUNK, (c + 1) * GLA_HEADS * CHUNK)

    def chunk_values(c):
        rows = slice(c * CHUNK, (c + 1) * CHUNK)
        return jnp.concatenate([gv_ref[rows, hd * GLA_DV:(hd + 1) * GLA_DV]
                                for hd in range(GLA_HEADS)], axis=0)

    def state_free_products(c):
        ki = ki_s[c * CHUNK:(c + 1) * CHUNK, :]
        k4 = jnp.concatenate([ki] * GLA_HEADS, axis=0)
        scores = lax.dot_general(qd4_s[chunk_rows4(c), :], k4, NT_DIMS, preferred_element_type=F32)
        kv_t = lax.dot_general(chunk_values(c), kl4_s[chunk_rows4(c), :], TN_DIMS,
                               preferred_element_type=F32)
        return scores, kv_t

    ahead = state_free_products(0)
    for c in range(n_chunks):
        rows = slice(c * CHUNK, (c + 1) * CHUNK)
        scores, kv_t = ahead
        if c + 1 < n_chunks:
            ahead = state_free_products(c + 1)
        q4 = qd4_s[chunk_rows4(c), :]
        st = state_s[...]
        scores = jnp.where(keep, scores, 0.0).astype(BF16)
        o = (jnp.dot(scores, chunk_values(c), preferred_element_type=F32)
             + lax.dot_general(q4, st.astype(BF16), NT_DIMS, preferred_element_type=F32))
        state_s[...] = dec_s[c:c + 1, :] * st + kv_t
        gate = jnp.concatenate([ggs_ref[rows, hd * GLA_DV:(hd + 1) * GLA_DV]
                                for hd in range(GLA_HEADS)], axis=0).astype(F32)
        o = (_rms_scale(o, nw) * gate).astype(BF16)
        for hd in range(GLA_HEADS):
            o_ref[rows, hd * GLA_DV:(hd + 1) * GLA_DV] = o[hd * CHUNK:(hd + 1) * CHUNK, :]


def _gla(gq, gk, g, gv, ggs, norm_w, batch, seq):
    tm = TM_GLA
    nt = seq // tm
    row = lambda b, i: (b * nt + i, 0)
    fixed = lambda b, i: (0, 0)
    T = gq.shape[0]
    return pl.pallas_call(
        _gla_kernel,
        grid=(batch, nt),
        in_specs=[
            pl.BlockSpec((tm, GLA_QK), row),
            pl.BlockSpec((tm, GLA_QK), row),
            pl.BlockSpec((tm, GLA_QK), row),
            pl.BlockSpec((tm, GLA_WIDTH), row),
            pl.BlockSpec((tm, GLA_WIDTH), row),
            pl.BlockSpec((1, GLA_DV), fixed),
        ],
        out_specs=pl.BlockSpec((tm, GLA_WIDTH), row),
        out_shape=jax.ShapeDtypeStruct((T, GLA_WIDTH), BF16),
        scratch_shapes=[
            pltpu.VMEM((GLA_HEADS * tm, GLA_QK), BF16),
            pltpu.VMEM((tm, GLA_QK), BF16),
            pltpu.VMEM((GLA_HEADS * tm, GLA_QK), BF16),
            pltpu.VMEM((tm // CHUNK, GLA_QK), F32),
            pltpu.VMEM((GLA_DV, GLA_QK), F32),
        ],
        compiler_params=pltpu.CompilerParams(dimension_semantics=("parallel", "arbitrary"),
                                             vmem_limit_bytes=VMEM_LIMIT),
        name="gla",
    )(gq, gk, g, gv, ggs, norm_w)


def _diff_attn_kernel(lam_ref, qq_ref, k_ref, vt_ref, gs_ref, nw_ref, o_ref, *, lam_init):
    seq = k_ref.shape[0]
    tk = TK_ATT
    n_blk = seq // tk
    cw = 2 * tk

    lv = lam_ref[...]
    lam = (jnp.exp(jnp.sum(lv[0:1] * lv[1:2], axis=-1, keepdims=True))
           - jnp.exp(jnp.sum(lv[2:3] * lv[3:4], axis=-1, keepdims=True)) + lam_init)
    nw = nw_ref[...]

    qc = (lax.broadcasted_iota(jnp.int32, (cw, tk), 0) % tk) // CHUNK
    kc = lax.broadcasted_iota(jnp.int32, (cw, tk), 1) // CHUNK
    stair_bias = jnp.where(kc <= qc, 0.0, -jnp.inf).astype(F32)

    def scores(cb):
        nk = (cb + 1) * tk
        s = lax.dot_general(qq_ref[cb * cw:(cb + 1) * cw, :], k_ref[0:nk, :], NT_DIMS,
                            preferred_element_type=F32)
        parts = [s[:, j * tk:(j + 1) * tk] for j in range(cb)] + [s[:, cb * tk:] + stair_bias]
        mm = parts[0]
        for part in parts[1:]:
            mm = jnp.maximum(mm, part)
        return parts, jnp.max(mm, axis=-1, keepdims=True)

    def finish(cb, parts, m):
        nk = (cb + 1) * tk
        p = jnp.concatenate([jnp.exp2(part - m).astype(BF16) for part in parts], axis=1)
        pv = lax.dot_general(p, vt_ref[:, 0:nk], NT_DIMS, preferred_element_type=F32)
        o0 = pv[0:tk, 0:DIFF_DV] / pv[0:tk, DIFF_DV:DIFF_DV + 1]
        o1 = pv[tk:, 0:DIFF_DV] / pv[tk:, DIFF_DV:DIFF_DV + 1]
        o = o0 - lam * o1
        rows = slice(cb * tk, (cb + 1) * tk)
        o = _rms_scale(o, nw) * (1.0 - lam_init) * gs_ref[rows, :].astype(F32)
        o_ref[rows, :] = o.astype(BF16)

    pending = [scores(cb) for cb in range(SCORES_AHEAD)]
    for cb in range(n_blk):
        if cb + SCORES_AHEAD < n_blk:
            pending.append(scores(cb + SCORES_AHEAD))
        finish(cb, *pending.pop(0))


def _diff_attn(lam_vecs, dq, dk, dvt, dgs, norm_w, batch, seq, lam_init):
    assert TK_ATT % CHUNK == 0 and seq % TK_ATT == 0
    T = dk.shape[0]
    bh = lambda b, h: (b, h)
    hb = lambda b, h: (h, b)
    fixed = lambda b, h: (0, 0)
    blk = pl.BlockSpec((seq, DIFF_DV), bh)
    return pl.pallas_call(
        functools.partial(_diff_attn_kernel, lam_init=lam_init),
        grid=(batch, DIFF_HEADS),
        in_specs=[pl.BlockSpec(lam_vecs.shape, fixed),
                  pl.BlockSpec((2 * seq, DIFF_DV), bh), blk,
                  pl.BlockSpec((VT_ROWS, seq), hb), blk,
                  pl.BlockSpec((1, DIFF_DV), fixed)],
        out_specs=blk,
        out_shape=jax.ShapeDtypeStruct((T, DIFF_WIDTH), BF16),
        compiler_params=pltpu.CompilerParams(dimension_semantics=("parallel", "parallel"),
                                             vmem_limit_bytes=VMEM_LIMIT),
        name="diff_attn",
    )(lam_vecs, dq, dk, dvt, dgs, norm_w)


def _mem_kv_kernel(mem_ref, nw_ref, w_ref, k_ref, v_ref):
    d = mem_ref.shape[1]
    mn = _rms_scale(mem_ref[...], nw_ref[...]).astype(BF16)
    k_ref[...] = jnp.dot(mn, w_ref[:, :d], preferred_element_type=F32).astype(BF16)
    v_ref[...] = jnp.dot(mn, w_ref[:, d:], preferred_element_type=F32).astype(BF16)


def _mem_kv(mem2d, norm_w, w_ckv, batch, mem_len):
    D = mem2d.shape[1]
    row = lambda b: (b, 0)
    fixed = lambda b: (0, 0)
    shp = jax.ShapeDtypeStruct((batch * mem_len, D), BF16)
    return pl.pallas_call(
        _mem_kv_kernel,
        grid=(batch,),
        in_specs=[pl.BlockSpec((mem_len, D), row), pl.BlockSpec((1, D), fixed),
                  pl.BlockSpec(w_ckv.shape, fixed)],
        out_specs=(pl.BlockSpec((mem_len, D), row), pl.BlockSpec((mem_len, D), row)),
        out_shape=(shp, shp),
        compiler_params=pltpu.CompilerParams(dimension_semantics=("parallel",),
                                             vmem_limit_bytes=VMEM_LIMIT),
        name="mem_kv",
    )(mem2d, norm_w, w_ckv)


def _out_cross_kernel(x_ref, og_ref, od_ref, wo_ref, ncw_ref, wq_ref, k_ref, v_ref, wco_ref,
                      fnw_ref, o_ref, ctx_s, *, final_norm):
    d = x_ref.shape[1]
    dh = d // CROSS_HEADS
    pm = x_ref.shape[0] // OUT_ROW_PARTS
    parts = [slice(p * pm, (p + 1) * pm) for p in range(OUT_ROW_PARTS)]
    x1 = [x_ref[r, :]
          + jnp.dot(og_ref[r, :], wo_ref[0:GLA_WIDTH, :], preferred_element_type=F32)
          + jnp.dot(od_ref[r, :], wo_ref[GLA_WIDTH:, :], preferred_element_type=F32)
          for r in parts]
    q = [(jnp.dot(_rms_scale(x, ncw_ref[...]).astype(BF16), wq_ref[...],
                  preferred_element_type=F32) * (dh ** -0.5 * LOG2E)).astype(BF16) for x in x1]
    def head_scores(hd, part):
        sl = slice(hd * dh, (hd + 1) * dh)
        return lax.dot_general(q[part][:, sl], k_ref[:, sl], NT_DIMS,
                               preferred_element_type=F32)

    items = [(hd, part) for hd in range(CROSS_HEADS) for part in range(OUT_ROW_PARTS)]
    s_next = head_scores(*items[0])
    for n, (hd, part) in enumerate(items):
        s = s_next
        if n + 1 < len(items):
            s_next = head_scores(*items[n + 1])
        sl = slice(hd * dh, (hd + 1) * dh)
        p = jnp.exp2(s - jnp.max(s, axis=-1, keepdims=True))
        ctx = jnp.dot(p.astype(BF16), v_ref[:, sl], preferred_element_type=F32)
        ctx_s[parts[part], sl] = (ctx / jnp.sum(p, axis=-1, keepdims=True)).astype(BF16)
    for r, x in zip(parts, x1):
        x2 = x + jnp.dot(ctx_s[r, :], wco_ref[...], preferred_element_type=F32)
        if final_norm:
            x2 = _rms_scale(x2, fnw_ref[...])
        o_ref[r, :] = x2


def _out_cross(x2d, og, od, w_out, ncw, w_cq, kc, vc, w_co, fnw, seq, mem_len, final_norm):
    T, D = x2d.shape
    tm = TM_OUT
    per_b = seq // tm
    row = lambda i: (i, 0)
    fixed = lambda i: (0, 0)
    bmap = lambda i: (i // per_b, 0)
    return pl.pallas_call(
        functools.partial(_out_cross_kernel, final_norm=final_norm),
        grid=(T // tm,),
        in_specs=[
            pl.BlockSpec((tm, D), row),
            pl.BlockSpec((tm, GLA_WIDTH), row),
            pl.BlockSpec((tm, DIFF_WIDTH), row),
            pl.BlockSpec(w_out.shape, fixed),
            pl.BlockSpec((1, D), fixed),
            pl.BlockSpec(w_cq.shape, fixed),
            pl.BlockSpec((mem_len, D), bmap),
            pl.BlockSpec((mem_len, D), bmap),
            pl.BlockSpec(w_co.shape, fixed),
            pl.BlockSpec((1, D), fixed),
        ],
        out_specs=pl.BlockSpec((tm, D), row),
        out_shape=jax.ShapeDtypeStruct((T, D), F32),
        scratch_shapes=[pltpu.VMEM((tm, D), BF16)],
        compiler_params=pltpu.CompilerParams(dimension_semantics=("parallel",),
                                             vmem_limit_bytes=VMEM_LIMIT),
        name="out_cross",
    )(x2d, og, od, w_out, ncw, w_cq, kc, vc, w_co, fnw)


def _rope_tables(positions):
    half = ROPE_DIM // 2
    T = positions.size
    per_row = LANES // half
    inv_freq = ROPE_THETA ** (-(jnp.arange(0, ROPE_DIM, 2, dtype=F32) / ROPE_DIM))
    pos = positions.reshape(T // per_row, per_row).astype(F32)
    ang = jnp.repeat(pos, half, axis=1) * jnp.tile(inv_freq, per_row)
    pieces = []
    rest_c, rest_s = jnp.cos(ang), jnp.sin(ang)
    for _ in range(ROPE_SPLIT):
        pc, ps = rest_c.astype(BF16), rest_s.astype(BF16)
        pieces += [pc.reshape(T, half), ps.reshape(T, half)]
        rest_c, rest_s = rest_c - pc.astype(F32), rest_s - ps.astype(F32)
    return jnp.concatenate(pieces, axis=1)


def _rope_select_matrix():
    half = ROPE_DIM // 2
    sel = np.zeros((ROPE_DIM, 2 * LANES), np.float32)
    for j in range(LANES):
        jm = j % DIFF_DQK
        if jm < ROPE_DIM:
            sel[jm % half, j] = 1.0
            sel[half + jm % half, LANES + j] = -1.0 if jm < half else 1.0
    return jnp.asarray(np.tile(sel, (ROPE_SPLIT, 1)), BF16)


def kernel(x, mem, positions, norm_mix_w, w_in, w_alpha2, b_alpha2, gla_norm_w,
           lam_q1, lam_k1, lam_q2, lam_k2, diff_norm_w, w_out,
           norm_cross_w, norm_mem_w, w_cq, w_ckv, w_co, final_norm_w):
    B, S, D = x.shape
    M = mem.shape[1]
    depth = w_in.shape[0]
    T = B * S
    cs_t = _rope_tables(positions)
    sel = _rope_select_matrix()
    x2d = x.reshape(T, D)
    mem2d = mem.reshape(B * M, D)

    c_gv = 2 * GLA_QK
    c_ga = c_gv + 2 * GLA_WIDTH
    c_dq = c_ga + GLA_LOWRANK

    for l in range(depth):
        wl = w_in[l]
        ga_pad = jnp.pad(wl[:, c_ga:c_dq], ((0, 0), (0, LANES - GLA_LOWRANK)))
        wa = jnp.concatenate([wl[:, :c_gv], ga_pad], axis=1).astype(BF16)
        c_dv = c_dq + 2 * DIFF_WIDTH
        c_dg = c_dv + DIFF_WIDTH
        wb = jnp.concatenate([wl[:, c_gv:c_ga], wl[:, c_dq:c_dv], wl[:, c_dg:]],
                             axis=1).astype(BF16)
        wvt = wl[:, c_dv:c_dg].T.astype(BF16)
        wa2 = jnp.pad(w_alpha2[l], ((0, LANES - GLA_LOWRANK), (0, 0))).astype(BF16)
        ba2 = b_alpha2[l].reshape(1, GLA_QK).astype(F32)

        gq, gk, gv, ggs, g, dq, dk, dvt, dgs = _in_proj(
            x2d, norm_mix_w[l].reshape(1, D), wa, wa2, ba2, wb, wvt, cs_t, sel)

        o_gla = _gla(gq, gk, g, gv, ggs, gla_norm_w[l].reshape(1, GLA_DV), B, S)

        lam_init = 0.8 - 0.6 * math.exp(-0.3 * l)
        lam_vecs = jnp.stack([lam_q1[l], lam_k1[l], lam_q2[l], lam_k2[l]]).astype(F32)
        o_diff = _diff_attn(lam_vecs, dq, dk, dvt, dgs, diff_norm_w[l].reshape(1, DIFF_DV),
                            B, S, lam_init)

        kc, vc = _mem_kv(mem2d, norm_mem_w[l].reshape(1, D), w_ckv[l].astype(BF16), B, M)

        x2d = _out_cross(x2d, o_gla, o_diff, w_out[l].astype(BF16),
                         norm_cross_w[l].reshape(1, D), w_cq[l].astype(BF16), kc, vc,
                         w_co[l].astype(BF16), final_norm_w.reshape(1, D), S, M,
                         final_norm=(l == depth - 1))
    return x2d.reshape(B, S, D)
```

```python
import functools
import math

import jax
import jax.numpy as jnp
import numpy as np
from jax import lax
from jax.experimental import pallas as pl
from jax.experimental.pallas import tpu as pltpu

F32 = jnp.float32
BF16 = jnp.bfloat16

CHUNK = 64
GLA_HEADS = 4
GLA_DK = 64
GLA_DV = 128
GLA_QK = GLA_HEADS * GLA_DK
GLA_WIDTH = GLA_HEADS * GLA_DV
GLA_LOWRANK = 16
GLA_TAU = 16.0
DIFF_HEADS = 4
DIFF_DQK = 64
DIFF_DV = 128
DIFF_WIDTH = DIFF_HEADS * DIFF_DV
ROPE_DIM = 16
ROPE_THETA = 500000.0
CROSS_HEADS = 4
EPS = 1e-6
LOG2E = math.log2(math.e)
ROPE_SPLIT = 3

LANES = 128

TM_PROJ = 1024
TM_GLA = 2048
TK_ATT = 256
SUM_ROWS = 16
VT_ROWS = DIFF_DV + SUM_ROWS
SCORES_AHEAD = 3
TM_OUT = 1024
ROW_PARTS = 2
OUT_ROW_PARTS = 2
VMEM_LIMIT = 56 * 1024 * 1024

NT_DIMS = (((1,), (1,)), ((), ()))
TN_DIMS = (((0,), (0,)), ((), ()))


def _rms_scale(xf, w):
    return xf * lax.rsqrt(jnp.mean(xf * xf, axis=-1, keepdims=True) + EPS) * w


def _silu(t):
    return t * (1.0 / (1.0 + jnp.exp(-t)))


def _in_proj_kernel(x_ref, nw_ref, wa_ref, wa2_ref, ba2_ref, wb_ref, wvt_ref, cs_ref, sel_ref,
                    gq_ref, gk_ref, gv_ref, ggs_ref, g_ref, dq_ref, dk_ref, dvt_ref, dgs_ref):
    pm = x_ref.shape[0] // ROW_PARTS
    for part in range(ROW_PARTS):
        _in_proj_rows(slice(part * pm, (part + 1) * pm),
                      x_ref, nw_ref, wa_ref, wa2_ref, ba2_ref, wb_ref, wvt_ref, cs_ref, sel_ref,
                      gq_ref, gk_ref, gv_ref, ggs_ref, g_ref, dq_ref, dk_ref, dvt_ref, dgs_ref)


def _in_proj_rows(rows, x_ref, nw_ref, wa_ref, wa2_ref, ba2_ref, wb_ref, wvt_ref, cs_ref, sel_ref,
                  gq_ref, gk_ref, gv_ref, ggs_ref, g_ref, dq_ref, dk_ref, dvt_ref, dgs_ref):
    h = _rms_scale(x_ref[rows, :], nw_ref[...]).astype(BF16)

    def proj_b(group):
        return jnp.dot(h, wb_ref[:, group * GLA_WIDTH:(group + 1) * GLA_WIDTH],
                       preferred_element_type=F32)

    qkg = jnp.dot(h, wa_ref[...], preferred_element_type=F32)
    gq_ref[rows, :] = qkg[:, 0:GLA_QK] * (GLA_DK ** -0.5)
    gk_ref[rows, :] = qkg[:, GLA_QK:2 * GLA_QK]
    ga = qkg[:, 2 * GLA_QK:].astype(BF16)

    gv_ref[rows, :] = proj_b(0).astype(BF16)
    ggs_ref[rows, :] = _silu(proj_b(1)).astype(BF16)

    tabs = jnp.dot(cs_ref[rows, :], sel_ref[...], preferred_element_type=F32)
    lane = lax.broadcasted_iota(jnp.int32, (tabs.shape[0], LANES), 1)
    cos = tabs[:, :LANES] + jnp.where((lane % DIFF_DQK) < ROPE_DIM, 0.0, 1.0)
    sin = tabs[:, LANES:]
    first_half = (lane % DIFF_DQK) < (ROPE_DIM // 2)

    def rope(t):
        partner = jnp.where(first_half, pltpu.roll(t, LANES - ROPE_DIM // 2, 1),
                            pltpu.roll(t, ROPE_DIM // 2, 1))
        return t * cos + partner * sin

    q = proj_b(2)
    k = proj_b(3)
    for hd in range(DIFF_HEADS):
        sl = slice(hd * DIFF_DV, (hd + 1) * DIFF_DV)
        qr = rope(q[:, sl]) * (DIFF_DQK ** -0.5 * LOG2E)
        q0 = jnp.where(lane < DIFF_DQK, qr, 0.0).astype(BF16)
        q1 = jnp.where(lane >= DIFF_DQK, qr, 0.0).astype(BF16)
        for r in range(qr.shape[0] // TK_ATT):
            src = slice(r * TK_ATT, (r + 1) * TK_ATT)
            base = 2 * (rows.start + r * TK_ATT)
            dq_ref[base:base + TK_ATT, sl] = q0[src, :]
            dq_ref[base + TK_ATT:base + 2 * TK_ATT, sl] = q1[src, :]
        dk_ref[rows, sl] = rope(k[:, sl]).astype(BF16)
    dvt = lax.dot_general(wvt_ref[...], h, NT_DIMS, preferred_element_type=F32).astype(BF16)
    for hd in range(DIFF_HEADS):
        base = hd * VT_ROWS
        dvt_ref[base:base + DIFF_DV, rows] = dvt[hd * DIFF_DV:(hd + 1) * DIFF_DV, :]
        dvt_ref[base + DIFF_DV:base + VT_ROWS, rows] = jnp.ones((SUM_ROWS, dvt.shape[1]), BF16)

    logit = jnp.dot(ga, wa2_ref[...], preferred_element_type=F32) + ba2_ref[...]
    log_sig = jnp.minimum(logit, 0.0) - jnp.log1p(jnp.exp(-jnp.abs(logit)))
    g_ref[rows, :] = log_sig * (1.0 / GLA_TAU)

    dgs_ref[rows, :] = _silu(proj_b(4)).astype(BF16)


def _in_proj(x2d, norm_w, wa, wa2, ba2, wb, wvt, cs_t, sel):
    T, D = x2d.shape
    tm = TM_PROJ
    row = lambda i: (i, 0)
    col = lambda i: (0, i)
    fixed = lambda i: (0, 0)
    out_shapes = (
        jax.ShapeDtypeStruct((T, GLA_QK), F32),
        jax.ShapeDtypeStruct((T, GLA_QK), F32),
        jax.ShapeDtypeStruct((T, GLA_WIDTH), BF16),
        jax.ShapeDtypeStruct((T, GLA_WIDTH), BF16),
        jax.ShapeDtypeStruct((T, GLA_QK), F32),
        jax.ShapeDtypeStruct((2 * T, DIFF_WIDTH), BF16),
        jax.ShapeDtypeStruct((T, DIFF_WIDTH), BF16),
        jax.ShapeDtypeStruct((DIFF_HEADS * VT_ROWS, T), BF16),
        jax.ShapeDtypeStruct((T, DIFF_WIDTH), BF16),
    )
    out_specs = [pl.BlockSpec((tm, s.shape[1]), row) for s in out_shapes]
    out_specs[5] = pl.BlockSpec((2 * tm, DIFF_WIDTH), row)
    out_specs[7] = pl.BlockSpec((DIFF_HEADS * VT_ROWS, tm), col)
    return pl.pallas_call(
        _in_proj_kernel,
        grid=(T // tm,),
        in_specs=[
            pl.BlockSpec((tm, D), row),
            pl.BlockSpec((1, D), fixed),
            pl.BlockSpec(wa.shape, fixed),
            pl.BlockSpec(wa2.shape, fixed),
            pl.BlockSpec(ba2.shape, fixed),
            pl.BlockSpec(wb.shape, fixed),
            pl.BlockSpec(wvt.shape, fixed),
            pl.BlockSpec((tm, cs_t.shape[1]), row),
            pl.BlockSpec(sel.shape, fixed),
        ],
        out_specs=tuple(out_specs),
        out_shape=out_shapes,
        compiler_params=pltpu.CompilerParams(dimension_semantics=("parallel",),
                                             vmem_limit_bytes=VMEM_LIMIT),
        name="in_proj",
    )(x2d, norm_w, wa, wa2, ba2, wb, wvt, cs_t, sel)


def _gla_kernel(gq_ref, gk_ref, g_ref, gv_ref, ggs_ref, nw_ref, o_ref,
                qd4_s, ki_s, kl4_s, dec_s, state_s):
    tm = gq_ref.shape[0]
    n_chunks = tm // CHUNK

    @pl.when(pl.program_id(1) == 0)
    def _():
        state_s[...] = jnp.zeros_like(state_s)

    g = g_ref[...]
    row_in_chunk = lax.broadcasted_iota(jnp.int32, g.shape, 0) % CHUNK
    G = g
    shift = 1
    while shift < CHUNK:
        G = G + jnp.where(row_in_chunk >= shift, pltpu.roll(G, shift, 0), 0.0)
        shift *= 2
    q = gq_ref[...]
    k = gk_ref[...]
    qd = (q * jnp.exp(G)).astype(BF16)
    ki_s[...] = (k * jnp.exp(-G)).astype(BF16)

    lane_head = lax.broadcasted_iota(jnp.int32, (CHUNK, GLA_QK), 1) // GLA_DK
    zero = jnp.zeros((CHUNK, GLA_QK), BF16)
    for c in range(n_chunks):
        rows = slice(c * CHUNK, (c + 1) * CHUNK)
        g_last = G[(c + 1) * CHUNK - 1:(c + 1) * CHUNK, :]
        kl = (k[rows, :] * jnp.exp(g_last - G[rows, :])).astype(BF16)
        dec_s[c:c + 1, :] = jnp.exp(g_last)
        for hd in range(GLA_HEADS):
            dst = slice((c * GLA_HEADS + hd) * CHUNK, (c * GLA_HEADS + hd + 1) * CHUNK)
            qd4_s[dst, :] = jnp.where(lane_head == hd, qd[rows, :], zero)
            kl4_s[dst, :] = jnp.where(lane_head == hd, kl, zero)

    r_i = lax.broadcasted_iota(jnp.int32, (GLA_HEADS * CHUNK, GLA_HEADS * CHUNK), 0)
    c_i = lax.broadcasted_iota(jnp.int32, (GLA_HEADS * CHUNK, GLA_HEADS * CHUNK), 1)
    keep = (r_i // CHUNK == c_i // CHUNK) & (c_i % CHUNK <= r_i % CHUNK)
    nw = nw_ref[...]

    def chunk_rows4(c):
        return slice(c * GLA_HEADS * CHUNK, (c + 1) * GLA_HEADS * CHUNK)

    def chunk_values(c):
        rows = slice(c * CHUNK, (c + 1) * CHUNK)
        return jnp.concatenate([gv_ref[rows, hd * GLA_DV:(hd + 1) * GLA_DV]
                                for hd in range(GLA_HEADS)], axis=0)

    def state_free_products(c):
        ki = ki_s[c * CHUNK:(c + 1) * CHUNK, :]
        k4 = jnp.concatenate([ki] * GLA_HEADS, axis=0)
        scores = lax.dot_general(qd4_s[chunk_rows4(c), :], k4, NT_DIMS, preferred_element_type=F32)
        kv_t = lax.dot_general(chunk_values(c), kl4_s[chunk_rows4(c), :], TN_DIMS,
                               preferred_element_type=F32)
        return scores, kv_t

    ahead = state_free_products(0)
    for c in range(n_chunks):
        rows = slice(c * CHUNK, (c + 1) * CHUNK)
        scores, kv_t = ahead
        if c + 1 < n_chunks:
            ahead = state_free_products(c + 1)
        q4 = qd4_s[chunk_rows4(c), :]
        st = state_s[...]
        scores = jnp.where(keep, scores, 0.0).astype(BF16)
        o = (jnp.dot(scores, chunk_values(c), preferred_element_type=F32)
             + lax.dot_general(q4, st.astype(BF16), NT_DIMS, preferred_element_type=F32))
        state_s[...] = dec_s[c:c + 1, :] * st + kv_t
        gate = jnp.concatenate([ggs_ref[rows, hd * GLA_DV:(hd + 1) * GLA_DV]
                                for hd in range(GLA_HEADS)], axis=0).astype(F32)
        o = (_rms_scale(o, nw) * gate).astype(BF16)
        for hd in range(GLA_HEADS):
            o_ref[rows, hd * GLA_DV:(hd + 1) * GLA_DV] = o[hd * CHUNK:(hd + 1) * CHUNK, :]


def _gla(gq, gk, g, gv, ggs, norm_w, batch, seq):
    tm = TM_GLA
    nt = seq // tm
    row = lambda b, i: (b * nt + i, 0)
    fixed = lambda b, i: (0, 0)
    T = gq.shape[0]
    return pl.pallas_call(
        _gla_kernel,
        grid=(batch, nt),
        in_specs=[
            pl.BlockSpec((tm, GLA_QK), row),
            pl.BlockSpec((tm, GLA_QK), row),
            pl.BlockSpec((tm, GLA_QK), row),
            pl.BlockSpec((tm, GLA_WIDTH), row),
            pl.BlockSpec((tm, GLA_WIDTH), row),
            pl.BlockSpec((1, GLA_DV), fixed),
        ],
        out_specs=pl.BlockSpec((tm, GLA_WIDTH), row),
        out_shape=jax.ShapeDtypeStruct((T, GLA_WIDTH), BF16),
        scratch_shapes=[
            pltpu.VMEM((GLA_HEADS * tm, GLA_QK), BF16),
            pltpu.VMEM((tm, GLA_QK), BF16),
            pltpu.VMEM((GLA_HEADS * tm, GLA_QK), BF16),
            pltpu.VMEM((tm // CHUNK, GLA_QK), F32),
            pltpu.VMEM((GLA_DV, GLA_QK), F32),
        ],
        compiler_params=pltpu.CompilerParams(dimension_semantics=("parallel", "arbitrary"),
                                             vmem_limit_bytes=VMEM_LIMIT),
        name="gla",
    )(gq, gk, g, gv, ggs, norm_w)


def _diff_attn_kernel(lam_ref, qq_ref, k_ref, vt_ref, gs_ref, nw_ref, o_ref, *, lam_init):
    seq = k_ref.shape[0]
    tk = TK_ATT
    n_blk = seq // tk
    cw = 2 * tk

    lv = lam_ref[...]
    lam = (jnp.exp(jnp.sum(lv[0:1] * lv[1:2], axis=-1, keepdims=True))
           - jnp.exp(jnp.sum(lv[2:3] * lv[3:4], axis=-1, keepdims=True)) + lam_init)
    nw = nw_ref[...]

    qc = (lax.broadcasted_iota(jnp.int32, (cw, tk), 0) % tk) // CHUNK
    kc = lax.broadcasted_iota(jnp.int32, (cw, tk), 1) // CHUNK
    stair_bias = jnp.where(kc <= qc, 0.0, -jnp.inf).astype(F32)

    def scores(cb):
        nk = (cb + 1) * tk
        s = lax.dot_general(qq_ref[cb * cw:(cb + 1) * cw, :], k_ref[0:nk, :], NT_DIMS,
                            preferred_element_type=F32)
        parts = [s[:, j * tk:(j + 1) * tk] for j in range(cb)] + [s[:, cb * tk:] + stair_bias]
        mm = parts[0]
        for part in parts[1:]:
            mm = jnp.maximum(mm, part)
        return parts, jnp.max(mm, axis=-1, keepdims=True)

    def finish(cb, parts, m):
        nk = (cb + 1) * tk
        p = jnp.concatenate([jnp.exp2(part - m).astype(BF16) for part in parts], axis=1)
        pv = lax.dot_general(p, vt_ref[:, 0:nk], NT_DIMS, preferred_element_type=F32)
        o0 = pv[0:tk, 0:DIFF_DV] / pv[0:tk, DIFF_DV:DIFF_DV + 1]
        o1 = pv[tk:, 0:DIFF_DV] / pv[tk:, DIFF_DV:DIFF_DV + 1]
        o = o0 - lam * o1
        rows = slice(cb * tk, (cb + 1) * tk)
        o = _rms_scale(o, nw) * (1.0 - lam_init) * gs_ref[rows, :].astype(F32)
        o_ref[rows, :] = o.astype(BF16)

    pending = [scores(cb) for cb in range(SCORES_AHEAD)]
    for cb in range(n_blk):
        if cb + SCORES_AHEAD < n_blk:
            pending.append(scores(cb + SCORES_AHEAD))
        finish(cb, *pending.pop(0))


def _diff_attn(lam_vecs, dq, dk, dvt, dgs, norm_w, batch, seq, lam_init):
    assert TK_ATT % CHUNK == 0 and seq % TK_ATT == 0
    T = dk.shape[0]
    bh = lambda b, h: (b, h)
    hb = lambda b, h: (h, b)
    fixed = lambda b, h: (0, 0)
    blk = pl.BlockSpec((seq, DIFF_DV), bh)
    return pl.pallas_call(
        functools.partial(_diff_attn_kernel, lam_init=lam_init),
        grid=(batch, DIFF_HEADS),
        in_specs=[pl.BlockSpec(lam_vecs.shape, fixed),
                  pl.BlockSpec((2 * seq, DIFF_DV), bh), blk,
                  pl.BlockSpec((VT_ROWS, seq), hb), blk,
                  pl.BlockSpec((1, DIFF_DV), fixed)],
        out_specs=blk,
        out_shape=jax.ShapeDtypeStruct((T, DIFF_WIDTH), BF16),
        compiler_params=pltpu.CompilerParams(dimension_semantics=("parallel", "parallel"),
                                             vmem_limit_bytes=VMEM_LIMIT),
        name="diff_attn",
    )(lam_vecs, dq, dk, dvt, dgs, norm_w)


def _mem_kv_kernel(mem_ref, nw_ref, w_ref, k_ref, v_ref):
    d = mem_ref.shape[1]
    mn = _rms_scale(mem_ref[...], nw_ref[...]).astype(BF16)
    k_ref[...] = jnp.dot(mn, w_ref[:, :d], preferred_element_type=F32).astype(BF16)
    v_ref[...] = jnp.dot(mn, w_ref[:, d:], preferred_element_type=F32).astype(BF16)


def _mem_kv(mem2d, norm_w, w_ckv, batch, mem_len):
    D = mem2d.shape[1]
    row = lambda b: (b, 0)
    fixed = lambda b: (0, 0)
    shp = jax.ShapeDtypeStruct((batch * mem_len, D), BF16)
    return pl.pallas_call(
        _mem_kv_kernel,
        grid=(batch,),
        in_specs=[pl.BlockSpec((mem_len, D), row), pl.BlockSpec((1, D), fixed),
                  pl.BlockSpec(w_ckv.shape, fixed)],
        out_specs=(pl.BlockSpec((mem_len, D), row), pl.BlockSpec((mem_len, D), row)),
        out_shape=(shp, shp),
        compiler_params=pltpu.CompilerParams(dimension_semantics=("parallel",),
                                             vmem_limit_bytes=VMEM_LIMIT),
        name="mem_kv",
    )(mem2d, norm_w, w_ckv)


def _out_cross_kernel(x_ref, og_ref, od_ref, wo_ref, ncw_ref, wq_ref, mem_ref, nmw_ref, wkv_ref,
                      wco_ref, fnw_ref, o_ref, ctx_s, k_ref, v_ref, *, final_norm, tiles_per_batch):
    d = x_ref.shape[1]
    dh = d // CROSS_HEADS

    @pl.when(lax.rem(pl.program_id(0), tiles_per_batch) == 0)
    def _():
        mn = _rms_scale(mem_ref[...], nmw_ref[...]).astype(BF16)
        k_ref[...] = jnp.dot(mn, wkv_ref[:, :d], preferred_element_type=F32).astype(BF16)
        v_ref[...] = jnp.dot(mn, wkv_ref[:, d:], preferred_element_type=F32).astype(BF16)

    pm = x_ref.shape[0] // OUT_ROW_PARTS
    parts = [slice(p * pm, (p + 1) * pm) for p in range(OUT_ROW_PARTS)]
    x1 = [x_ref[r, :]
          + jnp.dot(og_ref[r, :], wo_ref[0:GLA_WIDTH, :], preferred_element_type=F32)
          + jnp.dot(od_ref[r, :], wo_ref[GLA_WIDTH:, :], preferred_element_type=F32)
          for r in parts]
    q = [(jnp.dot(_rms_scale(x, ncw_ref[...]).astype(BF16), wq_ref[...],
                  preferred_element_type=F32) * (dh ** -0.5 * LOG2E)).astype(BF16) for x in x1]
    def head_scores(hd, part):
        sl = slice(hd * dh, (hd + 1) * dh)
        return lax.dot_general(q[part][:, sl], k_ref[:, sl], NT_DIMS,
                               preferred_element_type=F32)

    items = [(hd, part) for hd in range(CROSS_HEADS) for part in range(OUT_ROW_PARTS)]
    s_next = head_scores(*items[0])
    for n, (hd, part) in enumerate(items):
        s = s_next
        if n + 1 < len(items):
            s_next = head_scores(*items[n + 1])
        sl = slice(hd * dh, (hd + 1) * dh)
        p = jnp.exp2(s - jnp.max(s, axis=-1, keepdims=True))
        ctx = jnp.dot(p.astype(BF16), v_ref[:, sl], preferred_element_type=F32)
        ctx_s[parts[part], sl] = (ctx / jnp.sum(p, axis=-1, keepdims=True)).astype(BF16)
    for r, x in zip(parts, x1):
        x2 = x + jnp.dot(ctx_s[r, :], wco_ref[...], preferred_element_type=F32)
        if final_norm:
            x2 = _rms_scale(x2, fnw_ref[...])
        o_ref[r, :] = x2


def _out_cross(x2d, og, od, w_out, ncw, w_cq, mem2d, nmw, w_ckv, w_co, fnw, seq, mem_len,
               final_norm):
    T, D = x2d.shape
    tm = TM_OUT
    per_b = seq // tm
    row = lambda i: (i, 0)
    fixed = lambda i: (0, 0)
    bmap = lambda i: (i // per_b, 0)
    return pl.pallas_call(
        functools.partial(_out_cross_kernel, final_norm=final_norm, tiles_per_batch=per_b),
        grid=(T // tm,),
        in_specs=[
            pl.BlockSpec((tm, D), row),
            pl.BlockSpec((tm, GLA_WIDTH), row),
            pl.BlockSpec((tm, DIFF_WIDTH), row),
            pl.BlockSpec(w_out.shape, fixed),
            pl.BlockSpec((1, D), fixed),
            pl.BlockSpec(w_cq.shape, fixed),
            pl.BlockSpec((mem_len, D), bmap),
            pl.BlockSpec((1, D), fixed),
            pl.BlockSpec(w_ckv.shape, fixed),
            pl.BlockSpec(w_co.shape, fixed),
            pl.BlockSpec((1, D), fixed),
        ],
        out_specs=pl.BlockSpec((tm, D), row),
        out_shape=jax.ShapeDtypeStruct((T, D), F32),
        scratch_shapes=[pltpu.VMEM((tm, D), BF16),
                        pltpu.VMEM((mem_len, D), BF16),
                        pltpu.VMEM((mem_len, D), BF16)],
        compiler_params=pltpu.CompilerParams(dimension_semantics=("arbitrary",),
                                             vmem_limit_bytes=VMEM_LIMIT),
        name="out_cross",
    )(x2d, og, od, w_out, ncw, w_cq, mem2d, nmw, w_ckv, w_co, fnw)


def _rope_tables(positions):
    half = ROPE_DIM // 2
    T = positions.size
    per_row = LANES // half
    inv_freq = ROPE_THETA ** (-(jnp.arange(0, ROPE_DIM, 2, dtype=F32) / ROPE_DIM))
    pos = positions.reshape(T // per_row, per_row).astype(F32)
    ang = jnp.repeat(pos, half, axis=1) * jnp.tile(inv_freq, per_row)
    pieces = []
    rest_c, rest_s = jnp.cos(ang), jnp.sin(ang)
    for _ in range(ROPE_SPLIT):
        pc, ps = rest_c.astype(BF16), rest_s.astype(BF16)
        pieces += [pc.reshape(T, half), ps.reshape(T, half)]
        rest_c, rest_s = rest_c - pc.astype(F32), rest_s - ps.astype(F32)
    return jnp.concatenate(pieces, axis=1)


def _rope_select_matrix():
    half = ROPE_DIM // 2
    sel = np.zeros((ROPE_DIM, 2 * LANES), np.float32)
    for j in range(LANES):
        jm = j % DIFF_DQK
        if jm < ROPE_DIM:
            sel[jm % half, j] = 1.0
            sel[half + jm % half, LANES + j] = -1.0 if jm < half else 1.0
    return jnp.asarray(np.tile(sel, (ROPE_SPLIT, 1)), BF16)


def kernel(x, mem, positions, norm_mix_w, w_in, w_alpha2, b_alpha2, gla_norm_w,
           lam_q1, lam_k1, lam_q2, lam_k2, diff_norm_w, w_out,
           norm_cross_w, norm_mem_w, w_cq, w_ckv, w_co, final_norm_w):
    B, S, D = x.shape
    M = mem.shape[1]
    depth = w_in.shape[0]
    T = B * S
    cs_t = _rope_tables(positions)
    sel = _rope_select_matrix()
    x2d = x.reshape(T, D)
    mem2d = mem.reshape(B * M, D)

    c_gv = 2 * GLA_QK
    c_ga = c_gv + 2 * GLA_WIDTH
    c_dq = c_ga + GLA_LOWRANK

    for l in range(depth):
        wl = w_in[l]
        ga_pad = jnp.pad(wl[:, c_ga:c_dq], ((0, 0), (0, LANES - GLA_LOWRANK)))
        wa = jnp.concatenate([wl[:, :c_gv], ga_pad], axis=1).astype(BF16)
        c_dv = c_dq + 2 * DIFF_WIDTH
        c_dg = c_dv + DIFF_WIDTH
        wb = jnp.concatenate([wl[:, c_gv:c_ga], wl[:, c_dq:c_dv], wl[:, c_dg:]],
                             axis=1).astype(BF16)
        wvt = wl[:, c_dv:c_dg].T.astype(BF16)
        wa2 = jnp.pad(w_alpha2[l], ((0, LANES - GLA_LOWRANK), (0, 0))).astype(BF16)
        ba2 = b_alpha2[l].reshape(1, GLA_QK).astype(F32)

        gq, gk, gv, ggs, g, dq, dk, dvt, dgs = _in_proj(
            x2d, norm_mix_w[l].reshape(1, D), wa, wa2, ba2, wb, wvt, cs_t, sel)

        o_gla = _gla(gq, gk, g, gv, ggs, gla_norm_w[l].reshape(1, GLA_DV), B, S)

        lam_init = 0.8 - 0.6 * math.exp(-0.3 * l)
        lam_vecs = jnp.stack([lam_q1[l], lam_k1[l], lam_q2[l], lam_k2[l]]).astype(F32)
        o_diff = _diff_attn(lam_vecs, dq, dk, dvt, dgs, diff_norm_w[l].reshape(1, DIFF_DV),
                            B, S, lam_init)

        x2d = _out_cross(x2d, o_gla, o_diff, w_out[l].astype(BF16),
                         norm_cross_w[l].reshape(1, D), w_cq[l].astype(BF16),
                         mem2d, norm_mem_w[l].reshape(1, D), w_ckv[l].astype(BF16),
                         w_co[l].astype(BF16), final_norm_w.reshape(1, D), S, M,
                         final_norm=(l == depth - 1))
    return x2d.reshape(B, S, D)
```

```python
import functools
import math

import jax
import jax.numpy as jnp
import numpy as np
from jax import lax
from jax.experimental import pallas as pl
from jax.experimental.pallas import tpu as pltpu

F32 = jnp.float32
BF16 = jnp.bfloat16

CHUNK = 64
GLA_HEADS = 4
GLA_DK = 64
GLA_DV = 128
GLA_QK = GLA_HEADS * GLA_DK
GLA_WIDTH = GLA_HEADS * GLA_DV
GLA_LOWRANK = 16
GLA_TAU = 16.0
DIFF_HEADS = 4
DIFF_DQK = 64
DIFF_DV = 128
DIFF_WIDTH = DIFF_HEADS * DIFF_DV
ROPE_DIM = 16
ROPE_THETA = 500000.0
CROSS_HEADS = 4
EPS = 1e-6
LOG2E = math.log2(math.e)
ROPE_SPLIT = 3

LANES = 128

TM_PROJ = 1024
TM_GLA = 2048
TK_ATT = 256
SUM_ROWS = 16
VT_ROWS = DIFF_DV + SUM_ROWS
SCORES_AHEAD = 3
TM_OUT = 1024
ROW_PARTS = 2
OUT_ROW_PARTS = 2
VMEM_LIMIT = 56 * 1024 * 1024

NT_DIMS = (((1,), (1,)), ((), ()))
TN_DIMS = (((0,), (0,)), ((), ()))


def _rms_scale(xf, w):
    return xf * lax.rsqrt(jnp.mean(xf * xf, axis=-1, keepdims=True) + EPS) * w


def _silu(t):
    return t * (1.0 / (1.0 + jnp.exp(-t)))


def _in_proj_kernel(x_ref, nw_ref, wa_ref, wa2_ref, ba2_ref, wb_ref, wvt_ref, cs_ref, sel_ref,
                    gq_ref, gk_ref, gv_ref, ggs_ref, g_ref, dq_ref, dk_ref, dvt_ref, dgs_ref):
    pm = x_ref.shape[0] // ROW_PARTS
    for part in range(ROW_PARTS):
        _in_proj_rows(slice(part * pm, (part + 1) * pm),
                      x_ref, nw_ref, wa_ref, wa2_ref, ba2_ref, wb_ref, wvt_ref, cs_ref, sel_ref,
                      gq_ref, gk_ref, gv_ref, ggs_ref, g_ref, dq_ref, dk_ref, dvt_ref, dgs_ref)


def _in_proj_rows(rows, x_ref, nw_ref, wa_ref, wa2_ref, ba2_ref, wb_ref, wvt_ref, cs_ref, sel_ref,
                  gq_ref, gk_ref, gv_ref, ggs_ref, g_ref, dq_ref, dk_ref, dvt_ref, dgs_ref):
    h = _rms_scale(x_ref[rows, :], nw_ref[...]).astype(BF16)

    def proj_b(group):
        return jnp.dot(h, wb_ref[:, group * GLA_WIDTH:(group + 1) * GLA_WIDTH],
                       preferred_element_type=F32)

    qkg = jnp.dot(h, wa_ref[...], preferred_element_type=F32)
    gq_ref[rows, :] = qkg[:, 0:GLA_QK] * (GLA_DK ** -0.5)
    gk_ref[rows, :] = qkg[:, GLA_QK:2 * GLA_QK]
    ga = qkg[:, 2 * GLA_QK:].astype(BF16)

    gv_ref[rows, :] = proj_b(0).astype(BF16)
    ggs_ref[rows, :] = _silu(proj_b(1)).astype(BF16)

    tabs = jnp.dot(cs_ref[rows, :], sel_ref[...], preferred_element_type=F32)
    lane = lax.broadcasted_iota(jnp.int32, (tabs.shape[0], LANES), 1)
    cos = tabs[:, :LANES] + jnp.where((lane % DIFF_DQK) < ROPE_DIM, 0.0, 1.0)
    sin = tabs[:, LANES:]
    first_half = (lane % DIFF_DQK) < (ROPE_DIM // 2)

    def rope(t):
        partner = jnp.where(first_half, pltpu.roll(t, LANES - ROPE_DIM // 2, 1),
                            pltpu.roll(t, ROPE_DIM // 2, 1))
        return t * cos + partner * sin

    q = proj_b(2)
    k = proj_b(3)
    for hd in range(DIFF_HEADS):
        sl = slice(hd * DIFF_DV, (hd + 1) * DIFF_DV)
        qr = rope(q[:, sl]) * (DIFF_DQK ** -0.5 * LOG2E)
        q0 = jnp.where(lane < DIFF_DQK, qr, 0.0).astype(BF16)
        q1 = jnp.where(lane >= DIFF_DQK, qr, 0.0).astype(BF16)
        for r in range(qr.shape[0] // TK_ATT):
            src = slice(r * TK_ATT, (r + 1) * TK_ATT)
            base = 2 * (rows.start + r * TK_ATT)
            dq_ref[base:base + TK_ATT, sl] = q0[src, :]
            dq_ref[base + TK_ATT:base + 2 * TK_ATT, sl] = q1[src, :]
        dk_ref[rows, sl] = rope(k[:, sl]).astype(BF16)
    dvt = lax.dot_general(wvt_ref[...], h, NT_DIMS, preferred_element_type=F32).astype(BF16)
    for hd in range(DIFF_HEADS):
        base = hd * VT_ROWS
        dvt_ref[base:base + DIFF_DV, rows] = dvt[hd * DIFF_DV:(hd + 1) * DIFF_DV, :]
        dvt_ref[base + DIFF_DV:base + VT_ROWS, rows] = jnp.ones((SUM_ROWS, dvt.shape[1]), BF16)

    logit = jnp.dot(ga, wa2_ref[...], preferred_element_type=F32) + ba2_ref[...]
    log_sig = jnp.minimum(logit, 0.0) - jnp.log1p(jnp.exp(-jnp.abs(logit)))
    g_ref[rows, :] = log_sig * (1.0 / GLA_TAU)

    dgs_ref[rows, :] = _silu(proj_b(4)).astype(BF16)


def _in_proj(x2d, norm_w, wa, wa2, ba2, wb, wvt, cs_t, sel):
    T, D = x2d.shape
    tm = TM_PROJ
    row = lambda i: (i, 0)
    col = lambda i: (0, i)
    fixed = lambda i: (0, 0)
    out_shapes = (
        jax.ShapeDtypeStruct((T, GLA_QK), F32),
        jax.ShapeDtypeStruct((T, GLA_QK), F32),
        jax.ShapeDtypeStruct((T, GLA_WIDTH), BF16),
        jax.ShapeDtypeStruct((T, GLA_WIDTH), BF16),
        jax.ShapeDtypeStruct((T, GLA_QK), F32),
        jax.ShapeDtypeStruct((2 * T, DIFF_WIDTH), BF16),
        jax.ShapeDtypeStruct((T, DIFF_WIDTH), BF16),
        jax.ShapeDtypeStruct((DIFF_HEADS * VT_ROWS, T), BF16),
        jax.ShapeDtypeStruct((T, DIFF_WIDTH), BF16),
    )
    out_specs = [pl.BlockSpec((tm, s.shape[1]), row) for s in out_shapes]
    out_specs[5] = pl.BlockSpec((2 * tm, DIFF_WIDTH), row)
    out_specs[7] = pl.BlockSpec((DIFF_HEADS * VT_ROWS, tm), col)
    return pl.pallas_call(
        _in_proj_kernel,
        grid=(T // tm,),
        in_specs=[
            pl.BlockSpec((tm, D), row),
            pl.BlockSpec((1, D), fixed),
            pl.BlockSpec(wa.shape, fixed),
            pl.BlockSpec(wa2.shape, fixed),
            pl.BlockSpec(ba2.shape, fixed),
            pl.BlockSpec(wb.shape, fixed),
            pl.BlockSpec(wvt.shape, fixed),
            pl.BlockSpec((tm, cs_t.shape[1]), row),
            pl.BlockSpec(sel.shape, fixed),
        ],
        out_specs=tuple(out_specs),
        out_shape=out_shapes,
        compiler_params=pltpu.CompilerParams(dimension_semantics=("parallel",),
                                             vmem_limit_bytes=VMEM_LIMIT),
        name="in_proj",
    )(x2d, norm_w, wa, wa2, ba2, wb, wvt, cs_t, sel)


def _gla_kernel(gq_ref, gk_ref, g_ref, gv_ref, ggs_ref, nw_ref, o_ref,
                qd4_s, ki_s, kl4_s, dec_s, state_s):
    tm = gq_ref.shape[0]
    n_chunks = tm // CHUNK

    @pl.when(pl.program_id(1) == 0)
    def _():
        state_s[...] = jnp.zeros_like(state_s)

    g = g_ref[...]
    row_in_chunk = lax.broadcasted_iota(jnp.int32, g.shape, 0) % CHUNK
    G = g
    shift = 1
    while shift < CHUNK:
        G = G + jnp.where(row_in_chunk >= shift, pltpu.roll(G, shift, 0), 0.0)
        shift *= 2
    q = gq_ref[...]
    k = gk_ref[...]
    qd = (q * jnp.exp(G)).astype(BF16)
    ki_s[...] = (k * jnp.exp(-G)).astype(BF16)

    lane_head = lax.broadcasted_iota(jnp.int32, (CHUNK, GLA_QK), 1) // GLA_DK
    zero = jnp.zeros((CHUNK, GLA_QK), BF16)
    for c in range(n_chunks):
        rows = slice(c * CHUNK, (c + 1) * CHUNK)
        g_last = G[(c + 1) * CHUNK - 1:(c + 1) * CHUNK, :]
        kl = (k[rows, :] * jnp.exp(g_last - G[rows, :])).astype(BF16)
        dec_s[c:c + 1, :] = jnp.exp(g_last)
        for hd in range(GLA_HEADS):
            dst = slice((c * GLA_HEADS + hd) * CHUNK, (c * GLA_HEADS + hd + 1) * CHUNK)
            qd4_s[dst, :] = jnp.where(lane_head == hd, qd[rows, :], zero)
            kl4_s[dst, :] = jnp.where(lane_head == hd, kl, zero)

    r_i = lax.broadcasted_iota(jnp.int32, (GLA_HEADS * CHUNK, GLA_HEADS * CHUNK), 0)
    c_i = lax.broadcasted_iota(jnp.int32, (GLA_HEADS * CHUNK, GLA_HEADS * CHUNK), 1)
    keep = (r_i // CHUNK == c_i // CHUNK) & (c_i % CHUNK <= r_i % CHUNK)
    nw = nw_ref[...]

    def chunk_rows4(c):
        return slice(c * GLA_HEADS * CHUNK, (c + 1) * GLA_HEADS * CHUNK)

    def chunk_values(c):
        rows = slice(c * CHUNK, (c + 1) * CHUNK)
        return jnp.concatenate([gv_ref[rows, hd * GLA_DV:(hd + 1) * GLA_DV]
                                for hd in range(GLA_HEADS)], axis=0)

    def state_free_products(c):
        ki = ki_s[c * CHUNK:(c + 1) * CHUNK, :]
        k4 = jnp.concatenate([ki] * GLA_HEADS, axis=0)
        scores = lax.dot_general(qd4_s[chunk_rows4(c), :], k4, NT_DIMS, preferred_element_type=F32)
        kv_t = lax.dot_general(chunk_values(c), kl4_s[chunk_rows4(c), :], TN_DIMS,
                               preferred_element_type=F32)
        return scores, kv_t

    ahead = state_free_products(0)
    for c in range(n_chunks):
        rows = slice(c * CHUNK, (c + 1) * CHUNK)
        scores, kv_t = ahead
        if c + 1 < n_chunks:
            ahead = state_free_products(c + 1)
        q4 = qd4_s[chunk_rows4(c), :]
        st = state_s[...]
        scores = jnp.where(keep, scores, 0.0).astype(BF16)
        o = (jnp.dot(scores, chunk_values(c), preferred_element_type=F32)
             + lax.dot_general(q4, st.astype(BF16), NT_DIMS, preferred_element_type=F32))
        state_s[...] = dec_s[c:c + 1, :] * st + kv_t
        gate = jnp.concatenate([ggs_ref[rows, hd * GLA_DV:(hd + 1) * GLA_DV]
                                for hd in range(GLA_HEADS)], axis=0).astype(F32)
        o = (_rms_scale(o, nw) * gate).astype(BF16)
        for hd in range(GLA_HEADS):
            o_ref[rows, hd * GLA_DV:(hd + 1) * GLA_DV] = o[hd * CHUNK:(hd + 1) * CHUNK, :]


def _gla(gq, gk, g, gv, ggs, norm_w, batch, seq):
    tm = TM_GLA
    nt = seq // tm
    row = lambda b, i: (b * nt + i, 0)
    fixed = lambda b, i: (0, 0)
    T = gq.shape[0]
    return pl.pallas_call(
        _gla_kernel,
        grid=(batch, nt),
        in_specs=[
            pl.BlockSpec((tm, GLA_QK), row),
            pl.BlockSpec((tm, GLA_QK), row),
            pl.BlockSpec((tm, GLA_QK), row),
            pl.BlockSpec((tm, GLA_WIDTH), row),
            pl.BlockSpec((tm, GLA_WIDTH), row),
            pl.BlockSpec((1, GLA_DV), fixed),
        ],
        out_specs=pl.BlockSpec((tm, GLA_WIDTH), row),
        out_shape=jax.ShapeDtypeStruct((T, GLA_WIDTH), BF16),
        scratch_shapes=[
            pltpu.VMEM((GLA_HEADS * tm, GLA_QK), BF16),
            pltpu.VMEM((tm, GLA_QK), BF16),
            pltpu.VMEM((GLA_HEADS * tm, GLA_QK), BF16),
            pltpu.VMEM((tm // CHUNK, GLA_QK), F32),
            pltpu.VMEM((GLA_DV, GLA_QK), F32),
        ],
        compiler_params=pltpu.CompilerParams(dimension_semantics=("parallel", "arbitrary"),
                                             vmem_limit_bytes=VMEM_LIMIT),
        name="gla",
    )(gq, gk, g, gv, ggs, norm_w)


def _diff_attn_kernel(lam_ref, qq_ref, k_ref, vt_ref, gs_ref, nw_ref, o_ref, *, lam_init):
    seq = k_ref.shape[0]
    tk = TK_ATT
    n_blk = seq // tk
    cw = 2 * tk

    lv = lam_ref[...]
    lam = (jnp.exp(jnp.sum(lv[0:1] * lv[1:2], axis=-1, keepdims=True))
           - jnp.exp(jnp.sum(lv[2:3] * lv[3:4], axis=-1, keepdims=True)) + lam_init)
    nw = nw_ref[...]

    qc = (lax.broadcasted_iota(jnp.int32, (cw, tk), 0) % tk) // CHUNK
    kc = lax.broadcasted_iota(jnp.int32, (cw, tk), 1) // CHUNK
    stair_bias = jnp.where(kc <= qc, 0.0, -jnp.inf).astype(F32)

    def scores(cb):
        nk = (cb + 1) * tk
        s = lax.dot_general(qq_ref[cb * cw:(cb + 1) * cw, :], k_ref[0:nk, :], NT_DIMS,
                            preferred_element_type=F32)
        parts = [s[:, j * tk:(j + 1) * tk] for j in range(cb)] + [s[:, cb * tk:] + stair_bias]
        mm = parts[0]
        for part in parts[1:]:
            mm = jnp.maximum(mm, part)
        return parts, jnp.max(mm, axis=-1, keepdims=True)

    def finish(cb, parts, m):
        nk = (cb + 1) * tk
        p = jnp.concatenate([jnp.exp2(part - m).astype(BF16) for part in parts], axis=1)
        pv = lax.dot_general(p, vt_ref[:, 0:nk], NT_DIMS, preferred_element_type=F32)
        o0 = pv[0:tk, 0:DIFF_DV] / pv[0:tk, DIFF_DV:DIFF_DV + 1]
        o1 = pv[tk:, 0:DIFF_DV] / pv[tk:, DIFF_DV:DIFF_DV + 1]
        o = o0 - lam * o1
        rows = slice(cb * tk, (cb + 1) * tk)
        o = _rms_scale(o, nw) * (1.0 - lam_init) * gs_ref[rows, :].astype(F32)
        o_ref[rows, :] = o.astype(BF16)

    pending = [scores(cb) for cb in range(SCORES_AHEAD)]
    for cb in range(n_blk):
        if cb + SCORES_AHEAD < n_blk:
            pending.append(scores(cb + SCORES_AHEAD))
        finish(cb, *pending.pop(0))


def _diff_attn(lam_vecs, dq, dk, dvt, dgs, norm_w, batch, seq, lam_init):
    assert TK_ATT % CHUNK == 0 and seq % TK_ATT == 0
    T = dk.shape[0]
    bh = lambda b, h: (b, h)
    hb = lambda b, h: (h, b)
    fixed = lambda b, h: (0, 0)
    blk = pl.BlockSpec((seq, DIFF_DV), bh)
    return pl.pallas_call(
        functools.partial(_diff_attn_kernel, lam_init=lam_init),
        grid=(batch, DIFF_HEADS),
        in_specs=[pl.BlockSpec(lam_vecs.shape, fixed),
                  pl.BlockSpec((2 * seq, DIFF_DV), bh), blk,
                  pl.BlockSpec((VT_ROWS, seq), hb), blk,
                  pl.BlockSpec((1, DIFF_DV), fixed)],
        out_specs=blk,
        out_shape=jax.ShapeDtypeStruct((T, DIFF_WIDTH), BF16),
        compiler_params=pltpu.CompilerParams(dimension_semantics=("parallel", "parallel"),
                                             vmem_limit_bytes=VMEM_LIMIT),
        name="diff_attn",
    )(lam_vecs, dq, dk, dvt, dgs, norm_w)


def _mem_kv_kernel(mem_ref, nw_ref, w_ref, k_ref, v_ref):
    d = mem_ref.shape[1]
    mn = _rms_scale(mem_ref[...], nw_ref[...]).astype(BF16)
    k_ref[...] = jnp.dot(mn, w_ref[:, :d], preferred_element_type=F32).astype(BF16)
    v_ref[...] = jnp.dot(mn, w_ref[:, d:], preferred_element_type=F32).astype(BF16)


def _mem_kv(mem2d, norm_w, w_ckv, batch, mem_len):
    D = mem2d.shape[1]
    row = lambda b: (b, 0)
    fixed = lambda b: (0, 0)
    shp = jax.ShapeDtypeStruct((batch * mem_len, D), BF16)
    return pl.pallas_call(
        _mem_kv_kernel,
        grid=(batch,),
        in_specs=[pl.BlockSpec((mem_len, D), row), pl.BlockSpec((1, D), fixed),
                  pl.BlockSpec(w_ckv.shape, fixed)],
        out_specs=(pl.BlockSpec((mem_len, D), row), pl.BlockSpec((mem_len, D), row)),
        out_shape=(shp, shp),
        compiler_params=pltpu.CompilerParams(dimension_semantics=("parallel",),
                                             vmem_limit_bytes=VMEM_LIMIT),
        name="mem_kv",
    )(mem2d, norm_w, w_ckv)


def _out_cross_kernel(x_ref, og_ref, od_ref, wo_ref, ncw_ref, wq_ref, mem_ref, nmw_ref, wkv_ref,
                      wco_ref, fnw_ref, o_ref, ctx_s, k_ref, v_ref, *, final_norm, tiles_per_batch):
    d = x_ref.shape[1]
    dh = d // CROSS_HEADS

    @pl.when(lax.rem(pl.program_id(0), tiles_per_batch) == 0)
    def _():
        mn = _rms_scale(mem_ref[...], nmw_ref[...]).astype(BF16)
        k_ref[...] = jnp.dot(mn, wkv_ref[:, :d], preferred_element_type=F32).astype(BF16)
        v_ref[...] = jnp.dot(mn, wkv_ref[:, d:], preferred_element_type=F32).astype(BF16)

    pm = x_ref.shape[0] // OUT_ROW_PARTS
    parts = [slice(p * pm, (p + 1) * pm) for p in range(OUT_ROW_PARTS)]
    x1 = [x_ref[r, :]
          + jnp.dot(og_ref[r, :], wo_ref[0:GLA_WIDTH, :], preferred_element_type=F32)
          + jnp.dot(od_ref[r, :], wo_ref[GLA_WIDTH:, :], preferred_element_type=F32)
          for r in parts]
    q = [(jnp.dot(_rms_scale(x, ncw_ref[...]).astype(BF16), wq_ref[...],
                  preferred_element_type=F32) * (dh ** -0.5 * LOG2E)).astype(BF16) for x in x1]
    def head_scores(hd, part):
        sl = slice(hd * dh, (hd + 1) * dh)
        return lax.dot_general(q[part][:, sl], k_ref[:, sl], NT_DIMS,
                               preferred_element_type=F32)

    items = [(hd, part) for hd in range(CROSS_HEADS) for part in range(OUT_ROW_PARTS)]
    s_next = head_scores(*items[0])
    for n, (hd, part) in enumerate(items):
        s = s_next
        if n + 1 < len(items):
            s_next = head_scores(*items[n + 1])
        sl = slice(hd * dh, (hd + 1) * dh)
        p = jnp.exp2(s - jnp.max(s, axis=-1, keepdims=True))
        ctx = jnp.dot(p.astype(BF16), v_ref[:, sl], preferred_element_type=F32)
        ctx_s[parts[part], sl] = (ctx / jnp.sum(p, axis=-1, keepdims=True)).astype(BF16)
    for r, x in zip(parts, x1):
        x2 = x + jnp.dot(ctx_s[r, :], wco_ref[...], preferred_element_type=F32)
        if final_norm:
            x2 = _rms_scale(x2, fnw_ref[...])
        o_ref[r, :] = x2


def _out_cross(x2d, og, od, w_out, ncw, w_cq, mem2d, nmw, w_ckv, w_co, fnw, seq, mem_len,
               final_norm):
    T, D = x2d.shape
    tm = TM_OUT
    per_b = seq // tm
    row = lambda i: (i, 0)
    fixed = lambda i: (0, 0)
    bmap = lambda i: (i // per_b, 0)
    resident = lambda w: pl.BlockSpec(w.shape, fixed, pipeline_mode=pl.Buffered(1))
    return pl.pallas_call(
        functools.partial(_out_cross_kernel, final_norm=final_norm, tiles_per_batch=per_b),
        grid=(T // tm,),
        in_specs=[
            pl.BlockSpec((tm, D), row),
            pl.BlockSpec((tm, GLA_WIDTH), row),
            pl.BlockSpec((tm, DIFF_WIDTH), row),
            resident(w_out),
            pl.BlockSpec((1, D), fixed),
            resident(w_cq),
            pl.BlockSpec((mem_len, D), bmap),
            pl.BlockSpec((1, D), fixed),
            resident(w_ckv),
            resident(w_co),
            pl.BlockSpec((1, D), fixed),
        ],
        out_specs=pl.BlockSpec((tm, D), row),
        out_shape=jax.ShapeDtypeStruct((T, D), F32),
        scratch_shapes=[pltpu.VMEM((tm, D), BF16),
                        pltpu.VMEM((mem_len, D), BF16),
                        pltpu.VMEM((mem_len, D), BF16)],
        compiler_params=pltpu.CompilerParams(dimension_semantics=("arbitrary",),
                                             vmem_limit_bytes=VMEM_LIMIT),
        name="out_cross",
    )(x2d, og, od, w_out, ncw, w_cq, mem2d, nmw, w_ckv, w_co, fnw)


def _rope_tables(positions):
    half = ROPE_DIM // 2
    T = positions.size
    per_row = LANES // half
    inv_freq = ROPE_THETA ** (-(jnp.arange(0, ROPE_DIM, 2, dtype=F32) / ROPE_DIM))
    pos = positions.reshape(T // per_row, per_row).astype(F32)
    ang = jnp.repeat(pos, half, axis=1) * jnp.tile(inv_freq, per_row)
    pieces = []
    rest_c, rest_s = jnp.cos(ang), jnp.sin(ang)
    for _ in range(ROPE_SPLIT):
        pc, ps = rest_c.astype(BF16), rest_s.astype(BF16)
        pieces += [pc.reshape(T, half), ps.reshape(T, half)]
        rest_c, rest_s = rest_c - pc.astype(F32), rest_s - ps.astype(F32)
    return jnp.concatenate(pieces, axis=1)


def _rope_select_matrix():
    half = ROPE_DIM // 2
    sel = np.zeros((ROPE_DIM, 2 * LANES), np.float32)
    for j in range(LANES):
        jm = j % DIFF_DQK
        if jm < ROPE_DIM:
            sel[jm % half, j] = 1.0
            sel[half + jm % half, LANES + j] = -1.0 if jm < half else 1.0
    return jnp.asarray(np.tile(sel, (ROPE_SPLIT, 1)), BF16)


def kernel(x, mem, positions, norm_mix_w, w_in, w_alpha2, b_alpha2, gla_norm_w,
           lam_q1, lam_k1, lam_q2, lam_k2, diff_norm_w, w_out,
           norm_cross_w, norm_mem_w, w_cq, w_ckv, w_co, final_norm_w):
    B, S, D = x.shape
    M = mem.shape[1]
    depth = w_in.shape[0]
    T = B * S
    cs_t = _rope_tables(positions)
    sel = _rope_select_matrix()
    x2d = x.reshape(T, D)
    mem2d = mem.reshape(B * M, D)

    c_gv = 2 * GLA_QK
    c_ga = c_gv + 2 * GLA_WIDTH
    c_dq = c_ga + GLA_LOWRANK

    for l in range(depth):
        wl = w_in[l]
        ga_pad = jnp.pad(wl[:, c_ga:c_dq], ((0, 0), (0, LANES - GLA_LOWRANK)))
        wa = jnp.concatenate([wl[:, :c_gv], ga_pad], axis=1).astype(BF16)
        c_dv = c_dq + 2 * DIFF_WIDTH
        c_dg = c_dv + DIFF_WIDTH
        wb = jnp.concatenate([wl[:, c_gv:c_ga], wl[:, c_dq:c_dv], wl[:, c_dg:]],
                             axis=1).astype(BF16)
        wvt = wl[:, c_dv:c_dg].T.astype(BF16)
        wa2 = jnp.pad(w_alpha2[l], ((0, LANES - GLA_LOWRANK), (0, 0))).astype(BF16)
        ba2 = b_alpha2[l].reshape(1, GLA_QK).astype(F32)

        gq, gk, gv, ggs, g, dq, dk, dvt, dgs = _in_proj(
            x2d, norm_mix_w[l].reshape(1, D), wa, wa2, ba2, wb, wvt, cs_t, sel)

        o_gla = _gla(gq, gk, g, gv, ggs, gla_norm_w[l].reshape(1, GLA_DV), B, S)

        lam_init = 0.8 - 0.6 * math.exp(-0.3 * l)
        lam_vecs = jnp.stack([lam_q1[l], lam_k1[l], lam_q2[l], lam_k2[l]]).astype(F32)
        o_diff = _diff_attn(lam_vecs, dq, dk, dvt, dgs, diff_norm_w[l].reshape(1, DIFF_DV),
                            B, S, lam_init)

        x2d = _out_cross(x2d, o_gla, o_diff, w_out[l].astype(BF16),
                         norm_cross_w[l].reshape(1, D), w_cq[l].astype(BF16),
                         mem2d, norm_mem_w[l].reshape(1, D), w_ckv[l].astype(BF16),
                         w_co[l].astype(BF16), final_norm_w.reshape(1, D), S, M,
                         final_norm=(l == depth - 1))
    return x2d.reshape(B, S, D)
```
